```python
import jax, jax.numpy as jnp
from jax import lax
import numpy as np

D_MODEL = 1024
BATCH = 16
SEQ = 2048
DEPTH = 1

PLE_DIM = 256
D_MIX = D_MODEL
CONV_CH = D_MIX // 2
CONV_GROUPS = 8
CONV_WIDTH = 31
ATTN_WIDTH = D_MIX - CONV_CH
HEAD_DIM = 64
N_HEADS = ATTN_WIDTH // HEAD_DIM
ROT_DIM = HEAD_DIM // 4
ROPE_THETA = 500000.0
MOBA_BLOCK = 256
MOBA_TOPK = 3
Q_CHUNK = 16
N_GROUPS = 4
EXPERTS_PER_GROUP = 8
N_EXPERTS = N_GROUPS * EXPERTS_PER_GROUP
TOPK_IN_GROUP = 2
D_EXPERT = D_MODEL // 4
IN_COLS = 2 * CONV_CH + 3 * ATTN_WIDTH
EPS = 1e-6

kernel_name = 'hymba_conformer_moba_hmoe_block'


def rms_norm(x, g):
    xf = x.astype(jnp.float32)
    y = xf * lax.rsqrt(jnp.mean(xf * xf, axis=-1, keepdims=True) + EPS)
    return (y * g.astype(jnp.float32)).astype(x.dtype)


def layer_norm(x, g, b):
    xf = x.astype(jnp.float32)
    mu = jnp.mean(xf, axis=-1, keepdims=True)
    var = jnp.mean(jnp.square(xf - mu), axis=-1, keepdims=True)
    y = (xf - mu) * lax.rsqrt(var + EPS) * g.astype(jnp.float32) + b.astype(jnp.float32)
    return y.astype(x.dtype)


def rope_tables(seq):
    pos = jnp.arange(seq, dtype=jnp.float32)
    inv_freq = jnp.power(jnp.float32(ROPE_THETA), -jnp.arange(0, ROT_DIM, 2, dtype=jnp.float32) / ROT_DIM)
    ang = pos[:, None] * inv_freq[None, :]
    return jnp.cos(ang), jnp.sin(ang)


def partial_rope(t, cos, sin):
    half = ROT_DIM // 2
    tf = t.astype(jnp.float32)
    t1 = tf[..., :half]
    t2 = tf[..., half:ROT_DIM]
    out = jnp.concatenate([t1 * cos - t2 * sin, t2 * cos + t1 * sin, tf[..., ROT_DIM:]], axis=-1)
    return out.astype(t.dtype)


def conformer_conv(a, gate, w_dw, b_dw, ln_g, ln_b):
    u = a * jax.nn.sigmoid(gate)
    kern = w_dw[:, None, :].astype(u.dtype)
    y = lax.conv_general_dilated(u, kern, window_strides=(1,), padding=[(CONV_WIDTH - 1, 0)],
                                 dimension_numbers=('NWC', 'WIO', 'NWC'),
                                 feature_group_count=CONV_CH)
    y = y + b_dw.astype(y.dtype)
    y = layer_norm(y, ln_g, ln_b)
    return jax.nn.silu(y)


def moba_attention(q, k, v):
    B, H, S, dh = q.shape
    nb = -(-S // MOBA_BLOCK)
    s_pad = nb * MOBA_BLOCK
    pad = s_pad - S
    padw = ((0, 0), (0, 0), (0, pad), (0, 0))
    qp, kp, vp = jnp.pad(q, padw), jnp.pad(k, padw), jnp.pad(v, padw)
    kb = kp.reshape(B, H, nb, MOBA_BLOCK, dh)
    vb = vp.reshape(B, H, nb, MOBA_BLOCK, dh)
    neg = jnp.finfo(jnp.float32).min

    k_mean = jnp.mean(kb.astype(jnp.float32), axis=3)
    q_blk = jnp.arange(s_pad) // MOBA_BLOCK
    gate = jnp.einsum('bhsd,bhnd->bhsn', qp.astype(jnp.float32), k_mean)
    past = jnp.arange(nb)[None, :] < q_blk[:, None]
    gate = jnp.where(past, gate, neg)
    n_sel = min(MOBA_TOPK, nb)
    _, sel = lax.top_k(gate, n_sel)
    sel_valid = sel < q_blk[:, None]

    n_chunks = s_pad // Q_CHUNK
    q_c = jnp.moveaxis(qp.reshape(B, H, n_chunks, Q_CHUNK, dh), 2, 0)
    sel_c = jnp.moveaxis(sel.reshape(B, H, n_chunks, Q_CHUNK, n_sel), 2, 0)
    val_c = jnp.moveaxis(sel_valid.reshape(B, H, n_chunks, Q_CHUNK, n_sel), 2, 0)
    scale = HEAD_DIM ** -0.5
    gather_blocks = jax.vmap(jax.vmap(lambda blocks, idx: blocks[idx]))
    q_off = jnp.arange(Q_CHUNK)
    k_off = jnp.arange(MOBA_BLOCK)

    def attend_chunk(args):
        c, qc, selc, validc = args
        start = c * Q_CHUNK
        blk = start // MOBA_BLOCK
        in_blk = start % MOBA_BLOCK + q_off
        k_sel = gather_blocks(kb, selc).astype(jnp.float32)
        v_sel = gather_blocks(vb, selc).astype(jnp.float32)
        k_own = lax.dynamic_index_in_dim(kb, blk, axis=2, keepdims=False).astype(jnp.float32)
        v_own = lax.dynamic_index_in_dim(vb, blk, axis=2, keepdims=False).astype(jnp.float32)
        qf = qc.astype(jnp.float32) * scale
        s_sel = jnp.einsum('bhqd,bhqnkd->bhqnk', qf, k_sel)
        s_sel = jnp.where(validc[..., None], s_sel, neg).reshape(B, H, Q_CHUNK, n_sel * MOBA_BLOCK)
        s_own = jnp.einsum('bhqd,bhkd->bhqk', qf, k_own)
        s_own = jnp.where(k_off[None, :] <= in_blk[:, None], s_own, neg)
        probs = jax.nn.softmax(jnp.concatenate([s_sel, s_own], axis=-1), axis=-1)
        p_sel = probs[..., :n_sel * MOBA_BLOCK].reshape(B, H, Q_CHUNK, n_sel, MOBA_BLOCK)
        p_own = probs[..., n_sel * MOBA_BLOCK:]
        out = (jnp.einsum('bhqnk,bhqnkd->bhqd', p_sel, v_sel)
               + jnp.einsum('bhqk,bhkd->bhqd', p_own, v_own))
        return out.astype(qc.dtype)

    out = lax.map(attend_chunk, (jnp.arange(n_chunks), q_c, sel_c, val_c))
    out = jnp.moveaxis(out, 0, 2).reshape(B, H, s_pad, dh)
    return out[:, :, :S]


def hier_moe(h, w_group, b_group, w_erouter, b_erouter, w_gate, w_up, w_down):
    hf = h.astype(jnp.float32)
    g_logits = hf @ w_group.astype(jnp.float32) + b_group.astype(jnp.float32)
    g_probs = jax.nn.softmax(g_logits, axis=-1)
    g_idx = jnp.argmax(g_logits, axis=-1)
    g_w = jnp.take_along_axis(g_probs, g_idx[..., None], axis=-1)
    e_all = jnp.einsum('bsd,gde->bsge', hf, w_erouter.astype(jnp.float32)) + b_erouter.astype(jnp.float32)
    e_logits = jnp.take_along_axis(e_all, g_idx[..., None, None], axis=2)[:, :, 0]
    top_vals, top_idx = lax.top_k(e_logits, TOPK_IN_GROUP)
    top_w = jax.nn.softmax(top_vals, axis=-1) * g_w
    expert_id = g_idx[..., None] * EXPERTS_PER_GROUP + top_idx
    gates = jnp.sum(jax.nn.one_hot(expert_id, N_EXPERTS, dtype=jnp.float32) * top_w[..., None], axis=-2)
    y = jnp.zeros(h.shape, jnp.float32)
    for e in range(N_EXPERTS):
        hid = jax.nn.silu(h @ w_gate[e]) * (h @ w_up[e])
        y = y + gates[..., e:e + 1] * (hid @ w_down[e]).astype(jnp.float32)
    return y.astype(h.dtype)


def setup_inputs(seed: int = 0) -> dict:
    key = jax.random.key(seed)
    ks = jax.random.split(key, 24)
    f32 = jnp.float32
    nrm = lambda k, shape, s: jax.random.normal(k, shape, f32) * s
    L = DEPTH
    return {
        'x': nrm(ks[0], (BATCH, SEQ, D_MODEL), 1.0),
        'p': nrm(ks[1], (DEPTH, BATCH, SEQ, PLE_DIM), 1.0),
        'g_mix': 1.0 + nrm(ks[2], (L, D_MODEL), 0.02),
        'w_in': nrm(ks[3], (L, D_MODEL, IN_COLS), D_MODEL ** -0.5),
        'w_dw': nrm(ks[4], (L, CONV_WIDTH, CONV_CH), CONV_WIDTH ** -0.5),
        'b_dw': nrm(ks[5], (L, CONV_CH), 0.02),
        'conv_ln_g': 1.0 + nrm(ks[6], (L, CONV_CH), 0.02),
        'conv_ln_b': nrm(ks[7], (L, CONV_CH), 0.02),
        'beta_conv': 1.0 + nrm(ks[8], (L, CONV_CH), 0.02),
        'beta_attn': 1.0 + nrm(ks[9], (L, ATTN_WIDTH), 0.02),
        'w_out': nrm(ks[10], (L, D_MIX, D_MODEL), D_MIX ** -0.5),
        'g_ffn': 1.0 + nrm(ks[11], (L, D_MODEL), 0.02),
        'w_group': nrm(ks[12], (L, D_MODEL, N_GROUPS), D_MODEL ** -0.5),
        'b_group': nrm(ks[13], (L, N_GROUPS), 0.01),
        'w_erouter': nrm(ks[14], (L, N_GROUPS, D_MODEL, EXPERTS_PER_GROUP), D_MODEL ** -0.5),
        'b_erouter': nrm(ks[15], (L, N_GROUPS, EXPERTS_PER_GROUP), 0.01),
        'w_gate': nrm(ks[16], (L, N_EXPERTS, D_MODEL, D_EXPERT), D_MODEL ** -0.5),
        'w_up': nrm(ks[17], (L, N_EXPERTS, D_MODEL, D_EXPERT), D_MODEL ** -0.5),
        'w_down': nrm(ks[18], (L, N_EXPERTS, D_EXPERT, D_MODEL), D_EXPERT ** -0.5),
        'g_ple': 1.0 + nrm(ks[19], (L, D_MODEL), 0.02),
        'w_ple_gate': nrm(ks[20], (L, D_MODEL, D_MODEL), D_MODEL ** -0.5),
        'b_ple_gate': nrm(ks[21], (L, D_MODEL), 0.02),
        'w_ple_proj': nrm(ks[22], (L, PLE_DIM, D_MODEL), PLE_DIM ** -0.5),
        'g_final': 1.0 + nrm(ks[23], (D_MODEL,), 0.02),
    }


def reference(x, p, g_mix, w_in, w_dw, b_dw, conv_ln_g, conv_ln_b, beta_conv, beta_attn, w_out,
              g_ffn, w_group, b_group, w_erouter, b_erouter, w_gate, w_up, w_down,
              g_ple, w_ple_gate, b_ple_gate, w_ple_proj, g_final):
    B, S, _ = x.shape
    cos, sin = rope_tables(S)
    splits = [CONV_CH, 2 * CONV_CH, 2 * CONV_CH + ATTN_WIDTH, 2 * CONV_CH + 2 * ATTN_WIDTH]
    to_heads = lambda t: t.reshape(B, S, N_HEADS, HEAD_DIM).transpose(0, 2, 1, 3)
    for i in range(DEPTH):
        h = rms_norm(x, g_mix[i])
        proj = h @ w_in[i]
        c_val, c_gate, q, k, v = jnp.split(proj, splits, axis=-1)
        conv_out = conformer_conv(c_val, c_gate, w_dw[i], b_dw[i], conv_ln_g[i], conv_ln_b[i])
        q = partial_rope(to_heads(q), cos, sin)
        k = partial_rope(to_heads(k), cos, sin)
        attn = moba_attention(q, k, to_heads(v))
        attn = attn.transpose(0, 2, 1, 3).reshape(B, S, ATTN_WIDTH)
        mixed = jnp.concatenate([rms_norm(conv_out, beta_conv[i]), rms_norm(attn, beta_attn[i])], axis=-1)
        x = x + mixed @ w_out[i]
        x = x + hier_moe(rms_norm(x, g_ffn[i]), w_group[i], b_group[i], w_erouter[i], b_erouter[i],
                         w_gate[i], w_up[i], w_down[i])
        gate = jax.nn.sigmoid(rms_norm(x, g_ple[i]) @ w_ple_gate[i] + b_ple_gate[i])
        x = x + (p[i] @ w_ple_proj[i]) * gate
    return rms_norm(x, g_final)
```

```python
import functools

import jax
import jax.numpy as jnp
from jax import lax
from jax.experimental import pallas as pl
from jax.experimental.pallas import tpu as pltpu

F32 = jnp.float32
BF16 = jnp.bfloat16

D_MODEL = 1024
CONV_CH = 512
CONV_WIDTH = 31
ATTN_WIDTH = 512
HEAD_DIM = 64
N_HEADS = 8
ROT_DIM = 16
ROPE_THETA = 500000.0
MOBA_BLOCK = 256
MOBA_TOPK = 3
N_GROUPS = 4
EXPERTS_PER_GROUP = 8
N_EXPERTS = 32
D_EXPERT = 256
PLE_DIM = 256
IN_COLS = 2 * CONV_CH + 3 * ATTN_WIDTH
EPS = 1e-6

LANES = 128
HEAD_PAD = LANES
QKV_PAD = N_HEADS * HEAD_PAD
MASK_NEG = -1e30
CONV_HALO = 32
CONV_ROWS = 32
VMEM_LIMIT = 56 * 1024 * 1024


def _rms(x, g):
    return x * lax.rsqrt(jnp.mean(x * x, axis=-1, keepdims=True) + EPS) * g


def _in_proj_kernel(x_ref, g_ref, w_ref, rc_ref, ra_ref, rb_ref, u_ref, q_ref, k_ref, v_ref):
    h = _rms(x_ref[...], g_ref[...]).astype(BF16)
    proj = jnp.dot(h, w_ref[...], preferred_element_type=F32)
    u_ref[...] = proj[:, :CONV_CH] * jax.nn.sigmoid(proj[:, CONV_CH:2 * CONV_CH])
    rc, ra, rb = rc_ref[...], ra_ref[...], rb_ref[...]
    low = lax.broadcasted_iota(jnp.int32, (x_ref.shape[0], LANES), 1) < HEAD_DIM

    def put(dst_ref, base, rope, scale):
        for c in range(ATTN_WIDTH // LANES):
            t = proj[:, base + c * LANES: base + (c + 1) * LANES]
            if rope:
                t = t * rc + pltpu.roll(t, LANES - ROT_DIM // 2, 1) * ra + pltpu.roll(t, ROT_DIM // 2, 1) * rb
            if scale != 1.0:
                t = t * scale
            even = jnp.where(low, t, 0.0)
            odd = jnp.where(low, pltpu.roll(t, HEAD_DIM, 1), 0.0)
            dst_ref[:, (2 * c) * HEAD_PAD:(2 * c + 1) * HEAD_PAD] = even.astype(dst_ref.dtype)
            dst_ref[:, (2 * c + 1) * HEAD_PAD:(2 * c + 2) * HEAD_PAD] = odd.astype(dst_ref.dtype)

    put(q_ref, 2 * CONV_CH, True, HEAD_DIM ** -0.5)
    put(k_ref, 2 * CONV_CH + ATTN_WIDTH, True, 1.0)
    put(v_ref, 2 * CONV_CH + 2 * ATTN_WIDTH, False, 1.0)


def _rope_coeff_tables(seq):
    half = ROT_DIM // 2
    pos = jnp.arange(seq, dtype=F32)
    inv_freq = jnp.power(jnp.float32(ROPE_THETA), -jnp.arange(0, ROT_DIM, 2, dtype=F32) / ROT_DIM)
    ang = pos[:, None] * inv_freq[None, :]
    cos, sin = jnp.cos(ang), jnp.sin(ang)
    zeros = jnp.zeros((seq, HEAD_DIM - ROT_DIM), F32)
    c_head = jnp.concatenate([cos, cos, jnp.ones_like(zeros)], axis=1)
    a_head = jnp.concatenate([-sin, jnp.zeros_like(sin), zeros], axis=1)
    b_head = jnp.concatenate([jnp.zeros_like(sin), sin, zeros], axis=1)
    rep = LANES // HEAD_DIM
    return jnp.tile(c_head, (1, rep)), jnp.tile(a_head, (1, rep)), jnp.tile(b_head, (1, rep))


def _in_proj(x2, g_mix, w_in_bf, seq, tm):
    t_tokens = x2.shape[0]
    rc, ra, rb = _rope_coeff_tables(seq)
    n_seq_tiles = seq // tm
    row = lambda i: (i, 0)
    fixed = lambda i: (0, 0)
    tab = lambda i: (i % n_seq_tiles, 0)
    return pl.pallas_call(
        _in_proj_kernel,
        grid=(t_tokens // tm,),
        in_specs=[pl.BlockSpec((tm, D_MODEL), row), pl.BlockSpec((1, D_MODEL), fixed),
                  pl.BlockSpec((D_MODEL, IN_COLS), fixed),
                  pl.BlockSpec((tm, LANES), tab), pl.BlockSpec((tm, LANES), tab), pl.BlockSpec((tm, LANES), tab)],
        out_specs=[pl.BlockSpec((tm, CONV_CH), row), pl.BlockSpec((tm, QKV_PAD), row),
                   pl.BlockSpec((tm, QKV_PAD), row), pl.BlockSpec((tm, QKV_PAD), row)],
        out_shape=[jax.ShapeDtypeStruct((t_tokens, CONV_CH), F32),
                   jax.ShapeDtypeStruct((t_tokens, QKV_PAD), BF16),
                   jax.ShapeDtypeStruct((t_tokens, QKV_PAD), BF16),
                   jax.ShapeDtypeStruct((t_tokens, QKV_PAD), BF16)],
        compiler_params=pltpu.CompilerParams(dimension_semantics=("arbitrary",), vmem_limit_bytes=VMEM_LIMIT),
        name="in_proj",
    )(x2, g_mix.reshape(1, D_MODEL), w_in_bf, rc, ra, rb)


def _conv_kernel(u_ref, w_ref, b_ref, lg_ref, lb_ref, beta_ref, o_ref, pad_ref):
    ts = u_ref.shape[0]

    @pl.when(pl.program_id(1) == 0)
    def _():
        pad_ref[0:CONV_HALO, :] = jnp.zeros((CONV_HALO, CONV_CH), F32)

    pad_ref[CONV_HALO:CONV_HALO + ts, :] = u_ref[...]
    w = w_ref[...]
    first = CONV_HALO - (CONV_WIDTH - 1)
    for r0 in range(0, ts, CONV_ROWS):
        acc = jnp.zeros((CONV_ROWS, CONV_CH), F32)
        for tap in range(CONV_WIDTH):
            acc = acc + pad_ref[r0 + first + tap: r0 + first + tap + CONV_ROWS, :] * w[tap:tap + 1, :]
        y = acc + b_ref[...]
        mu = jnp.mean(y, axis=-1, keepdims=True)
        d = y - mu
        var = jnp.mean(d * d, axis=-1, keepdims=True)
        y = d * lax.rsqrt(var + EPS) * lg_ref[...] + lb_ref[...]
        y = y * jax.nn.sigmoid(y)
        o_ref[r0:r0 + CONV_ROWS, :] = _rms(y, beta_ref[...]).astype(o_ref.dtype)
    pad_ref[0:CONV_HALO, :] = pad_ref[ts:ts + CONV_HALO, :]


def _conv(u2, w_dw, b_dw, ln_g, ln_b, beta, batch, seq, ts):
    n_s = seq // ts
    row = lambda b, s: (b * n_s + s, 0)
    fixed = lambda b, s: (0, 0)
    vec = pl.BlockSpec((1, CONV_CH), fixed)
    return pl.pallas_call(
        _conv_kernel,
        grid=(batch, n_s),
        in_specs=[pl.BlockSpec((ts, CONV_CH), row), pl.BlockSpec((CONV_WIDTH, CONV_CH), fixed), vec, vec, vec, vec],
        out_specs=pl.BlockSpec((ts, CONV_CH), row),
        out_shape=jax.ShapeDtypeStruct((batch * seq, CONV_CH), BF16),
        scratch_shapes=[pltpu.VMEM((ts + CONV_HALO, CONV_CH), F32)],
        compiler_params=pltpu.CompilerParams(dimension_semantics=("arbitrary", "arbitrary"),
                                             vmem_limit_bytes=VMEM_LIMIT),
        name="conformer_conv",
    )(u2, w_dw, b_dw.reshape(1, -1), ln_g.reshape(1, -1), ln_b.reshape(1, -1), beta.reshape(1, -1))


def _moba_kernel(q_ref, k_ref, v_ref, blk_ref, place_ref, o_ref):
    seq = q_ref.shape[0]
    nb = seq // MOBA_BLOCK
    blk_onehot = blk_ref[...]
    place = place_ref[...]
    sub = lax.broadcasted_iota(jnp.int32, (nb, MOBA_BLOCK), 0)
    r_io = lax.broadcasted_iota(jnp.int32, (MOBA_BLOCK, MOBA_BLOCK), 0)
    c_io = lax.broadcasted_iota(jnp.int32, (MOBA_BLOCK, MOBA_BLOCK), 1)
    causal = c_io <= r_io
    heads = []
    for hh in range(2):
        lo, hi = hh * HEAD_PAD, (hh + 1) * HEAD_PAD
        k = k_ref[:, lo:hi]
        v = v_ref[:, lo:hi]
        k_aug = k + blk_onehot
        kf = k.astype(F32)
        k_mean = jnp.concatenate(
            [jnp.mean(kf[j * MOBA_BLOCK:(j + 1) * MOBA_BLOCK], axis=0, keepdims=True) for j in range(nb)], axis=0)
        km_hi = k_mean.astype(BF16)
        km_lo = (k_mean - km_hi.astype(F32)).astype(BF16)
        outs = []
        for i in range(nb):
            q = q_ref[i * MOBA_BLOCK:(i + 1) * MOBA_BLOCK, lo:hi]
            if i > 0:
                dn = (((1,), (1,)), ((), ()))
                gate = (lax.dot_general(km_hi, q, dn, preferred_element_type=F32)
                        + lax.dot_general(km_lo, q, dn, preferred_element_type=F32))
                past = sub < i
                sel_rows = []
                for j in range(i):
                    gj = gate[j:j + 1, :]
                    beats = ((gate > gj) | ((gate == gj) & (sub < j))) & past
                    cnt = jnp.sum(beats.astype(F32), axis=0, keepdims=True)
                    sel_rows.append(cnt < float(MOBA_TOPK))
                bias_t = jnp.where(sub == i, 0.0, MASK_NEG)
                for j in range(i):
                    bias_t = jnp.where((sub == j) & sel_rows[j], 0.0, bias_t)
                bias = jnp.dot(bias_t.T, place, preferred_element_type=F32)
                q_aug = q + bias.astype(BF16)
                kn = (i + 1) * MOBA_BLOCK
                s = lax.dot_general(q_aug, k_aug[:kn], (((1,), (1,)), ((), ())), preferred_element_type=F32)
                s_past = s[:, :i * MOBA_BLOCK]
                s_own = jnp.where(causal, s[:, i * MOBA_BLOCK:], MASK_NEG)
                m = jnp.maximum(jnp.max(s_past, axis=-1, keepdims=True), jnp.max(s_own, axis=-1, keepdims=True))
                p_past = jnp.exp(s_past - m)
                p_own = jnp.exp(s_own - m)
                l = jnp.sum(p_past, axis=-1, keepdims=True) + jnp.sum(p_own, axis=-1, keepdims=True)
                acc = (jnp.dot(p_past.astype(BF16), v[:i * MOBA_BLOCK], preferred_element_type=F32)
                       + jnp.dot(p_own.astype(BF16), v[i * MOBA_BLOCK:kn], preferred_element_type=F32))
            else:
                s = lax.dot_general(q, k[:MOBA_BLOCK], (((1,), (1,)), ((), ())), preferred_element_type=F32)
                s_own = jnp.where(causal, s, MASK_NEG)
                m = jnp.max(s_own, axis=-1, keepdims=True)
                p_own = jnp.exp(s_own - m)
                l = jnp.sum(p_own, axis=-1, keepdims=True)
                acc = jnp.dot(p_own.astype(BF16), v[:MOBA_BLOCK], preferred_element_type=F32)
            outs.append(acc / l)
        heads.append(outs)
    for i in range(nb):
        o_ref[i * MOBA_BLOCK:(i + 1) * MOBA_BLOCK, :] = heads[0][i] + pltpu.roll(heads[1][i], HEAD_DIM, 1)


def _moba(qp, kp, vp, batch, seq):
    nb = seq // MOBA_BLOCK
    lane = jnp.arange(LANES)[None, :]
    blk = (jnp.arange(seq) // MOBA_BLOCK)[:, None]
    blk_onehot = (lane == HEAD_DIM + blk).astype(BF16)
    place = (lane == HEAD_DIM + jnp.arange(nb)[:, None]).astype(F32)
    pair = lambda b, c: (b, c)
    fixed = lambda b, c: (0, 0)
    spec = pl.BlockSpec((seq, 2 * HEAD_PAD), pair)
    return pl.pallas_call(
        _moba_kernel,
        grid=(batch, N_HEADS // 2),
        in_specs=[spec, spec, spec, pl.BlockSpec((seq, LANES), fixed), pl.BlockSpec((nb, LANES), fixed)],
        out_specs=pl.BlockSpec((seq, LANES), pair),
        out_shape=jax.ShapeDtypeStruct((batch * seq, ATTN_WIDTH), F32),
        compiler_params=pltpu.CompilerParams(dimension_semantics=("arbitrary", "arbitrary"),
                                             vmem_limit_bytes=VMEM_LIMIT),
        name="moba_attention",
    )(qp, kp, vp, blk_onehot, place)


def _out_proj_kernel(x_ref, c_ref, a_ref, beta_ref, wc_ref, wa_ref, gf_ref, wr_ref, br_ref,
                     x1_ref, h_ref, gates_ref):
    an = _rms(a_ref[...], beta_ref[...]).astype(BF16)
    x1 = (x_ref[...] + jnp.dot(c_ref[...], wc_ref[...], preferred_element_type=F32)
          + jnp.dot(an, wa_ref[...], preferred_element_type=F32))
    x1_ref[...] = x1
    h = _rms(x1, gf_ref[...])
    hb = h.astype(BF16)
    h_ref[...] = hb
    logits = jnp.dot(hb, wr_ref[...], preferred_element_type=F32) + br_ref[...]
    tm = logits.shape[0]
    gl = logits[:, :N_GROUPS]
    g_io = lax.broadcasted_iota(jnp.int32, (tm, N_GROUPS), 1)
    g_max = jnp.max(gl, axis=-1, keepdims=True)
    g_idx = jnp.min(jnp.where(gl == g_max, g_io, N_GROUPS), axis=-1, keepdims=True)
    g_w = 1.0 / jnp.sum(jnp.exp(gl - g_max), axis=-1, keepdims=True)
    el = jnp.zeros((tm, EXPERTS_PER_GROUP), F32)
    for g in range(N_GROUPS):
        lo = N_GROUPS + g * EXPERTS_PER_GROUP
        el = jnp.where(g_idx == g, logits[:, lo:lo + EXPERTS_PER_GROUP], el)
    e_io = lax.broadcasted_iota(jnp.int32, (tm, EXPERTS_PER_GROUP), 1)
    v1 = jnp.max(el, axis=-1, keepdims=True)
    i1 = jnp.min(jnp.where(el == v1, e_io, EXPERTS_PER_GROUP), axis=-1, keepdims=True)
    el2 = jnp.where(e_io == i1, -jnp.inf, el)
    v2 = jnp.max(el2, axis=-1, keepdims=True)
    i2 = jnp.min(jnp.where(el2 == v2, e_io, EXPERTS_PER_GROUP), axis=-1, keepdims=True)
    t = jnp.exp(v2 - v1)
    w1 = g_w / (1.0 + t)
    w2 = g_w * t / (1.0 + t)
    lane = lax.broadcasted_iota(jnp.int32, (tm, LANES), 1)
    id1 = g_idx * EXPERTS_PER_GROUP + i1
    id2 = g_idx * EXPERTS_PER_GROUP + i2
    gates_ref[...] = jnp.where(lane == id1, w1, 0.0) + jnp.where(lane == id2, w2, 0.0)


def _out_proj(x2, conv_n, attn, beta_attn, w_out_bf, g_ffn, w_router_bf, b_router, tm):
    t_tokens = x2.shape[0]
    row = lambda i: (i, 0)
    fixed = lambda i: (0, 0)
    return pl.pallas_call(
        _out_proj_kernel,
        grid=(t_tokens // tm,),
        in_specs=[pl.BlockSpec((tm, D_MODEL), row), pl.BlockSpec((tm, CONV_CH), row),
                  pl.BlockSpec((tm, ATTN_WIDTH), row), pl.BlockSpec((1, ATTN_WIDTH), fixed),
                  pl.BlockSpec((CONV_CH, D_MODEL), fixed), pl.BlockSpec((ATTN_WIDTH, D_MODEL), fixed),
                  pl.BlockSpec((1, D_MODEL), fixed), pl.BlockSpec((D_MODEL, LANES), fixed),
                  pl.BlockSpec((1, LANES), fixed)],
        out_specs=[pl.BlockSpec((tm, D_MODEL), row), pl.BlockSpec((tm, D_MODEL), row),
                   pl.BlockSpec((tm, LANES), row)],
        out_shape=[jax.ShapeDtypeStruct((t_tokens, D_MODEL), F32),
                   jax.ShapeDtypeStruct((t_tokens, D_MODEL), BF16),
                   jax.ShapeDtypeStruct((t_tokens, LANES), F32)],
        compiler_params=pltpu.CompilerParams(dimension_semantics=("arbitrary",), vmem_limit_bytes=VMEM_LIMIT),
        name="out_proj_router",
    )(x2, conv_n, attn, beta_attn.reshape(1, -1), w_out_bf[:CONV_CH], w_out_bf[CONV_CH:],
      g_ffn.reshape(1, -1), w_router_bf, b_router)


def _moe_kernel(h_ref, gates_ref, wg_ref, wu_ref, wd_ref, y_ref, acc_ref):
    e = pl.program_id(1)

    @pl.when(e == 0)
    def _():
        acc_ref[...] = jnp.zeros_like(acc_ref)

    h = h_ref[...]
    gates = gates_ref[...]
    lane = lax.broadcasted_iota(jnp.int32, gates.shape, 1)
    g_col = jnp.sum(jnp.where(lane == e, gates, 0.0), axis=-1, keepdims=True)
    a = jnp.dot(h, wg_ref[0], preferred_element_type=F32)
    b = jnp.dot(h, wu_ref[0], preferred_element_type=F32)
    hid = a * jax.nn.sigmoid(a) * b * g_col
    acc_ref[...] += jnp.dot(hid.astype(BF16), wd_ref[0], preferred_element_type=F32)

    @pl.when(e == pl.num_programs(1) - 1)
    def _():
        y_ref[...] = acc_ref[...]


def _moe(h2, gates, wg_bf, wu_bf, wd_bf, tm):
    t_tokens = h2.shape[0]
    row = lambda i, e: (i, 0)
    wsel = lambda i, e: (e, 0, 0)
    return pl.pallas_call(
        _moe_kernel,
        grid=(t_tokens // tm, N_EXPERTS),
        in_specs=[pl.BlockSpec((tm, D_MODEL), row), pl.BlockSpec((tm, LANES), row),
                  pl.BlockSpec((1, D_MODEL, D_EXPERT), wsel), pl.BlockSpec((1, D_MODEL, D_EXPERT), wsel),
                  pl.BlockSpec((1, D_EXPERT, D_MODEL), wsel)],
        out_specs=pl.BlockSpec((tm, D_MODEL), row),
        out_shape=jax.ShapeDtypeStruct((t_tokens, D_MODEL), F32),
        scratch_shapes=[pltpu.VMEM((tm, D_MODEL), F32)],
        compiler_params=pltpu.CompilerParams(dimension_semantics=("arbitrary", "arbitrary"),
                                             vmem_limit_bytes=VMEM_LIMIT),
        name="hier_moe",
    )(h2, gates, wg_bf, wu_bf, wd_bf)


def _ple_final_kernel(x1_ref, y_ref, p_ref, gp_ref, wg_ref, bg_ref, wp_ref, gfin_ref, o_ref):
    x2 = x1_ref[...] + y_ref[...]
    hg = _rms(x2, gp_ref[...]).astype(BF16)
    gate = jax.nn.sigmoid(jnp.dot(hg, wg_ref[...], preferred_element_type=F32) + bg_ref[...])
    emb = jnp.dot(p_ref[...].astype(BF16), wp_ref[...], preferred_element_type=F32)
    x3 = x2 + emb * gate
    o_ref[...] = _rms(x3, gfin_ref[...])


def _ple_final(x1, y, p2, g_ple, w_gate_bf, b_gate, w_proj_bf, g_final, tm):
    t_tokens = x1.shape[0]
    row = lambda i: (i, 0)
    fixed = lambda i: (0, 0)
    vec = pl.BlockSpec((1, D_MODEL), fixed)
    return pl.pallas_call(
        _ple_final_kernel,
        grid=(t_tokens // tm,),
        in_specs=[pl.BlockSpec((tm, D_MODEL), row), pl.BlockSpec((tm, D_MODEL), row),
                  pl.BlockSpec((tm, PLE_DIM), row), vec, pl.BlockSpec((D_MODEL, D_MODEL), fixed), vec,
                  pl.BlockSpec((PLE_DIM, D_MODEL), fixed), vec],
        out_specs=pl.BlockSpec((tm, D_MODEL), row),
        out_shape=jax.ShapeDtypeStruct((t_tokens, D_MODEL), F32),
        compiler_params=pltpu.CompilerParams(dimension_semantics=("arbitrary",), vmem_limit_bytes=VMEM_LIMIT),
        name="ple_final",
    )(x1, y, p2, g_ple.reshape(1, -1), w_gate_bf, b_gate.reshape(1, -1), w_proj_bf, g_final.reshape(1, -1))


def _layer(x2, p2, batch, seq, g_mix, w_in, w_dw, b_dw, conv_ln_g, conv_ln_b, beta_conv, beta_attn, w_out,
           g_ffn, w_group, b_group, w_erouter, b_erouter, w_gate, w_up, w_down,
           g_ple, w_ple_gate, b_ple_gate, w_ple_proj):
    u, qp, kp, vp = _in_proj(x2, g_mix, w_in.astype(BF16), seq, tm=512)
    conv_n = _conv(u, w_dw, b_dw, conv_ln_g, conv_ln_b, beta_conv, batch, seq, ts=256)
    attn = _moba(qp, kp, vp, batch, seq)
    w_router = jnp.concatenate(
        [w_group, jnp.transpose(w_erouter, (1, 0, 2)).reshape(D_MODEL, N_EXPERTS),
         jnp.zeros((D_MODEL, LANES - N_GROUPS - N_EXPERTS), F32)], axis=1).astype(BF16)
    b_router = jnp.concatenate([b_group, b_erouter.reshape(-1),
                                jnp.zeros((LANES - N_GROUPS - N_EXPERTS,), F32)]).reshape(1, LANES)
    x1, h2, gates = _out_proj(x2, conv_n, attn, beta_attn, w_out.astype(BF16), g_ffn, w_router, b_router, tm=512)
    y = _moe(h2, gates, w_gate.astype(BF16), w_up.astype(BF16), w_down.astype(BF16), tm=2048)
    return x1, y


def kernel(x, p, g_mix, w_in, w_dw, b_dw, conv_ln_g, conv_ln_b, beta_conv, beta_attn, w_out, g_ffn, w_group, b_group, w_erouter, b_erouter, w_gate, w_up, w_down, g_ple, w_ple_gate, b_ple_gate, w_ple_proj, g_final):
    batch, seq, _ = x.shape
    depth = p.shape[0]
    assert depth == 1 and seq % MOBA_BLOCK == 0
    x2 = x.reshape(batch * seq, D_MODEL)
    i = 0
    x1, y = _layer(x2, None, batch, seq, g_mix[i], w_in[i], w_dw[i], b_dw[i], conv_ln_g[i], conv_ln_b[i],
                   beta_conv[i], beta_attn[i], w_out[i], g_ffn[i], w_group[i], b_group[i], w_erouter[i],
                   b_erouter[i], w_gate[i], w_up[i], w_down[i], g_ple[i], w_ple_gate[i], b_ple_gate[i],
                   w_ple_proj[i])
    out = _ple_final(x1, y, p[i].reshape(batch * seq, PLE_DIM), g_ple[i], w_ple_gate[i].astype(BF16),
                     b_ple_gate[i], w_ple_proj[i].astype(BF16), g_final, tm=512)
    return out.reshape(batch, seq, D_MODEL)
```

```python
import functools

import jax
import jax.numpy as jnp
from jax import lax
from jax.experimental import pallas as pl
from jax.experimental.pallas import tpu as pltpu

F32 = jnp.float32
BF16 = jnp.bfloat16

D_MODEL = 1024
CONV_CH = 512
CONV_WIDTH = 31
ATTN_WIDTH = 512
HEAD_DIM = 64
N_HEADS = 8
ROT_DIM = 16
ROPE_THETA = 500000.0
MOBA_BLOCK = 256
MOBA_TOPK = 3
N_GROUPS = 4
EXPERTS_PER_GROUP = 8
N_EXPERTS = 32
D_EXPERT = 256
PLE_DIM = 256
N_BUCKETS = N_GROUPS * EXPERTS_PER_GROUP * EXPERTS_PER_GROUP
N_PAIR_BUCKETS = N_GROUPS * EXPERTS_PER_GROUP * (EXPERTS_PER_GROUP - 1) // 2
IN_COLS = 2 * CONV_CH + 3 * ATTN_WIDTH
EPS = 1e-6

LANES = 128
HEAD_PAD = LANES
QKV_PAD = N_HEADS * HEAD_PAD
HX_COLS = D_MODEL + LANES
MASK_NEG = -1e30
CONV_HALO = 32
CONV_ROWS = 32
VMEM_LIMIT = 56 * 1024 * 1024


def _rms(x, g):
    return x * lax.rsqrt(jnp.mean(x * x, axis=-1, keepdims=True) + EPS) * g


def _in_proj_kernel(x_ref, g_ref, w_ref, rc_ref, ra_ref, rb_ref, u_ref, q_ref, k_ref, v_ref):
    h = _rms(x_ref[...], g_ref[...]).astype(BF16)
    proj = jnp.dot(h, w_ref[...], preferred_element_type=F32)
    u_ref[...] = proj[:, :CONV_CH] * jax.nn.sigmoid(proj[:, CONV_CH:2 * CONV_CH])
    rc, ra, rb = rc_ref[...], ra_ref[...], rb_ref[...]
    low = lax.broadcasted_iota(jnp.int32, (x_ref.shape[0], LANES), 1) < HEAD_DIM

    def put(dst_ref, base, rope, scale):
        for c in range(ATTN_WIDTH // LANES):
            t = proj[:, base + c * LANES: base + (c + 1) * LANES]
            if rope:
                t = t * rc + pltpu.roll(t, LANES - ROT_DIM // 2, 1) * ra + pltpu.roll(t, ROT_DIM // 2, 1) * rb
            if scale != 1.0:
                t = t * scale
            even = jnp.where(low, t, 0.0)
            odd = jnp.where(low, pltpu.roll(t, HEAD_DIM, 1), 0.0)
            dst_ref[:, (2 * c) * HEAD_PAD:(2 * c + 1) * HEAD_PAD] = even.astype(dst_ref.dtype)
            dst_ref[:, (2 * c + 1) * HEAD_PAD:(2 * c + 2) * HEAD_PAD] = odd.astype(dst_ref.dtype)

    put(q_ref, 2 * CONV_CH, True, HEAD_DIM ** -0.5)
    put(k_ref, 2 * CONV_CH + ATTN_WIDTH, True, 1.0)
    put(v_ref, 2 * CONV_CH + 2 * ATTN_WIDTH, False, 1.0)


def _rope_coeff_tables(seq):
    half = ROT_DIM // 2
    pos = jnp.arange(seq, dtype=F32)
    inv_freq = jnp.power(jnp.float32(ROPE_THETA), -jnp.arange(0, ROT_DIM, 2, dtype=F32) / ROT_DIM)
    ang = pos[:, None] * inv_freq[None, :]
    cos, sin = jnp.cos(ang), jnp.sin(ang)
    zeros = jnp.zeros((seq, HEAD_DIM - ROT_DIM), F32)
    c_head = jnp.concatenate([cos, cos, jnp.ones_like(zeros)], axis=1)
    a_head = jnp.concatenate([-sin, jnp.zeros_like(sin), zeros], axis=1)
    b_head = jnp.concatenate([jnp.zeros_like(sin), sin, zeros], axis=1)
    rep = LANES // HEAD_DIM
    return jnp.tile(c_head, (1, rep)), jnp.tile(a_head, (1, rep)), jnp.tile(b_head, (1, rep))


def _in_proj(x2, g_mix, w_in_bf, seq, tm):
    t_tokens = x2.shape[0]
    rc, ra, rb = _rope_coeff_tables(seq)
    n_seq_tiles = seq // tm
    row = lambda i: (i, 0)
    fixed = lambda i: (0, 0)
    tab = lambda i: (i % n_seq_tiles, 0)
    return pl.pallas_call(
        _in_proj_kernel,
        grid=(t_tokens // tm,),
        in_specs=[pl.BlockSpec((tm, D_MODEL), row), pl.BlockSpec((1, D_MODEL), fixed),
                  pl.BlockSpec((D_MODEL, IN_COLS), fixed),
                  pl.BlockSpec((tm, LANES), tab), pl.BlockSpec((tm, LANES), tab), pl.BlockSpec((tm, LANES), tab)],
        out_specs=[pl.BlockSpec((tm, CONV_CH), row), pl.BlockSpec((tm, QKV_PAD), row),
                   pl.BlockSpec((tm, QKV_PAD), row), pl.BlockSpec((tm, QKV_PAD), row)],
        out_shape=[jax.ShapeDtypeStruct((t_tokens, CONV_CH), F32),
                   jax.ShapeDtypeStruct((t_tokens, QKV_PAD), BF16),
                   jax.ShapeDtypeStruct((t_tokens, QKV_PAD), BF16),
                   jax.ShapeDtypeStruct((t_tokens, QKV_PAD), BF16)],
        compiler_params=pltpu.CompilerParams(dimension_semantics=("arbitrary",), vmem_limit_bytes=VMEM_LIMIT),
        name="in_proj",
    )(x2, g_mix.reshape(1, D_MODEL), w_in_bf, rc, ra, rb)


def _conv_kernel(u_ref, w_ref, b_ref, lg_ref, lb_ref, beta_ref, o_ref, pad_ref):
    ts = u_ref.shape[0]

    @pl.when(pl.program_id(1) == 0)
    def _():
        pad_ref[0:CONV_HALO, :] = jnp.zeros((CONV_HALO, CONV_CH), F32)

    pad_ref[CONV_HALO:CONV_HALO + ts, :] = u_ref[...]
    w = w_ref[...]
    first = CONV_HALO - (CONV_WIDTH - 1)
    for r0 in range(0, ts, CONV_ROWS):
        acc = jnp.zeros((CONV_ROWS, CONV_CH), F32)
        for tap in range(CONV_WIDTH):
            acc = acc + pad_ref[r0 + first + tap: r0 + first + tap + CONV_ROWS, :] * w[tap:tap + 1, :]
        y = acc + b_ref[...]
        mu = jnp.mean(y, axis=-1, keepdims=True)
        d = y - mu
        var = jnp.mean(d * d, axis=-1, keepdims=True)
        y = d * lax.rsqrt(var + EPS) * lg_ref[...] + lb_ref[...]
        y = y * jax.nn.sigmoid(y)
        o_ref[r0:r0 + CONV_ROWS, :] = _rms(y, beta_ref[...]).astype(o_ref.dtype)
    pad_ref[0:CONV_HALO, :] = pad_ref[ts:ts + CONV_HALO, :]


def _conv(u2, w_dw, b_dw, ln_g, ln_b, beta, batch, seq, ts):
    n_s = seq // ts
    row = lambda b, s: (b * n_s + s, 0)
    fixed = lambda b, s: (0, 0)
    vec = pl.BlockSpec((1, CONV_CH), fixed)
    return pl.pallas_call(
        _conv_kernel,
        grid=(batch, n_s),
        in_specs=[pl.BlockSpec((ts, CONV_CH), row), pl.BlockSpec((CONV_WIDTH, CONV_CH), fixed), vec, vec, vec, vec],
        out_specs=pl.BlockSpec((ts, CONV_CH), row),
        out_shape=jax.ShapeDtypeStruct((batch * seq, CONV_CH), BF16),
        scratch_shapes=[pltpu.VMEM((ts + CONV_HALO, CONV_CH), F32)],
        compiler_params=pltpu.CompilerParams(dimension_semantics=("arbitrary", "arbitrary"),
                                             vmem_limit_bytes=VMEM_LIMIT),
        name="conformer_conv",
    )(u2, w_dw, b_dw.reshape(1, -1), ln_g.reshape(1, -1), ln_b.reshape(1, -1), beta.reshape(1, -1))


def _moba_kernel(q_ref, k_ref, v_ref, blk_ref, place_ref, o_ref):
    seq = q_ref.shape[0]
    nb = seq // MOBA_BLOCK
    blk_onehot = blk_ref[...]
    place = place_ref[...]
    sub = lax.broadcasted_iota(jnp.int32, (nb, MOBA_BLOCK), 0)
    r_io = lax.broadcasted_iota(jnp.int32, (MOBA_BLOCK, MOBA_BLOCK), 0)
    c_io = lax.broadcasted_iota(jnp.int32, (MOBA_BLOCK, MOBA_BLOCK), 1)
    causal = c_io <= r_io
    heads = []
    for hh in range(2):
        lo, hi = hh * HEAD_PAD, (hh + 1) * HEAD_PAD
        k = k_ref[:, lo:hi]
        v = v_ref[:, lo:hi]
        k_aug = k + blk_onehot
        kf = k.astype(F32)
        k_mean = jnp.concatenate(
            [jnp.mean(kf[j * MOBA_BLOCK:(j + 1) * MOBA_BLOCK], axis=0, keepdims=True) for j in range(nb)], axis=0)
        km_hi = k_mean.astype(BF16)
        km_lo = (k_mean - km_hi.astype(F32)).astype(BF16)
        outs = []
        for i in range(nb):
            q = q_ref[i * MOBA_BLOCK:(i + 1) * MOBA_BLOCK, lo:hi]
            if i > 0:
                dn = (((1,), (1,)), ((), ()))
                gate = (lax.dot_general(km_hi, q, dn, preferred_element_type=F32)
                        + lax.dot_general(km_lo, q, dn, preferred_element_type=F32))
                past = sub < i
                sel_rows = []
                for j in range(i):
                    gj = gate[j:j + 1, :]
                    beats = ((gate > gj) | ((gate == gj) & (sub < j))) & past
                    cnt = jnp.sum(beats.astype(F32), axis=0, keepdims=True)
                    sel_rows.append(cnt < float(MOBA_TOPK))
                bias_t = jnp.where(sub == i, 0.0, MASK_NEG)
                for j in range(i):
                    bias_t = jnp.where((sub == j) & sel_rows[j], 0.0, bias_t)
                bias = jnp.dot(bias_t.T, place, preferred_element_type=F32)
                q_aug = q + bias.astype(BF16)
                kn = (i + 1) * MOBA_BLOCK
                s = lax.dot_general(q_aug, k_aug[:kn], (((1,), (1,)), ((), ())), preferred_element_type=F32)
                s_past = s[:, :i * MOBA_BLOCK]
                s_own = jnp.where(causal, s[:, i * MOBA_BLOCK:], MASK_NEG)
                m = jnp.maximum(jnp.max(s_past, axis=-1, keepdims=True), jnp.max(s_own, axis=-1, keepdims=True))
                p_past = jnp.exp(s_past - m)
                p_own = jnp.exp(s_own - m)
                l = jnp.sum(p_past, axis=-1, keepdims=True) + jnp.sum(p_own, axis=-1, keepdims=True)
                acc = (jnp.dot(p_past.astype(BF16), v[:i * MOBA_BLOCK], preferred_element_type=F32)
                       + jnp.dot(p_own.astype(BF16), v[i * MOBA_BLOCK:kn], preferred_element_type=F32))
            else:
                s = lax.dot_general(q, k[:MOBA_BLOCK], (((1,), (1,)), ((), ())), preferred_element_type=F32)
                s_own = jnp.where(causal, s, MASK_NEG)
                m = jnp.max(s_own, axis=-1, keepdims=True)
                p_own = jnp.exp(s_own - m)
                l = jnp.sum(p_own, axis=-1, keepdims=True)
                acc = jnp.dot(p_own.astype(BF16), v[:MOBA_BLOCK], preferred_element_type=F32)
            outs.append(acc / l)
        heads.append(outs)
    for i in range(nb):
        o_ref[i * MOBA_BLOCK:(i + 1) * MOBA_BLOCK, :] = heads[0][i] + pltpu.roll(heads[1][i], HEAD_DIM, 1)


def _moba(qp, kp, vp, batch, seq):
    nb = seq // MOBA_BLOCK
    lane = jnp.arange(LANES)[None, :]
    blk = (jnp.arange(seq) // MOBA_BLOCK)[:, None]
    blk_onehot = (lane == HEAD_DIM + blk).astype(BF16)
    place = (lane == HEAD_DIM + jnp.arange(nb)[:, None]).astype(F32)
    pair = lambda b, c: (b, c)
    fixed = lambda b, c: (0, 0)
    spec = pl.BlockSpec((seq, 2 * HEAD_PAD), pair)
    return pl.pallas_call(
        _moba_kernel,
        grid=(batch, N_HEADS // 2),
        in_specs=[spec, spec, spec, pl.BlockSpec((seq, LANES), fixed), pl.BlockSpec((nb, LANES), fixed)],
        out_specs=pl.BlockSpec((seq, LANES), pair),
        out_shape=jax.ShapeDtypeStruct((batch * seq, ATTN_WIDTH), F32),
        compiler_params=pltpu.CompilerParams(dimension_semantics=("arbitrary", "arbitrary"),
                                             vmem_limit_bytes=VMEM_LIMIT),
        name="moba_attention",
    )(qp, kp, vp, blk_onehot, place)


def _out_proj_kernel(x_ref, c_ref, a_ref, beta_ref, wc_ref, wa_ref, gf_ref, wr_ref, br_ref, tri_ref,
                     x1_ref, hx_ref, meta_ref, cnt_ref, carry_ref):
    @pl.when(pl.program_id(0) == 0)
    def _():
        carry_ref[...] = jnp.zeros_like(carry_ref)

    an = _rms(a_ref[...], beta_ref[...]).astype(BF16)
    x1 = (x_ref[...] + jnp.dot(c_ref[...], wc_ref[...], preferred_element_type=F32)
          + jnp.dot(an, wa_ref[...], preferred_element_type=F32))
    x1_ref[...] = x1
    h = _rms(x1, gf_ref[...])
    hb = h.astype(BF16)
    hx_ref[:, :D_MODEL] = h
    logits = jnp.dot(hb, wr_ref[...], preferred_element_type=F32) + br_ref[...]
    tm = logits.shape[0]
    gl = logits[:, :N_GROUPS]
    g_io = lax.broadcasted_iota(jnp.int32, (tm, N_GROUPS), 1)
    g_max = jnp.max(gl, axis=-1, keepdims=True)
    g_idx = jnp.min(jnp.where(gl == g_max, g_io, N_GROUPS), axis=-1, keepdims=True)
    g_w = 1.0 / jnp.sum(jnp.exp(gl - g_max), axis=-1, keepdims=True)
    el = jnp.zeros((tm, EXPERTS_PER_GROUP), F32)
    for g in range(N_GROUPS):
        lo = N_GROUPS + g * EXPERTS_PER_GROUP
        el = jnp.where(g_idx == g, logits[:, lo:lo + EXPERTS_PER_GROUP], el)
    e_io = lax.broadcasted_iota(jnp.int32, (tm, EXPERTS_PER_GROUP), 1)
    v1 = jnp.max(el, axis=-1, keepdims=True)
    i1 = jnp.min(jnp.where(el == v1, e_io, EXPERTS_PER_GROUP), axis=-1, keepdims=True)
    el2 = jnp.where(e_io == i1, -jnp.inf, el)
    v2 = jnp.max(el2, axis=-1, keepdims=True)
    i2 = jnp.min(jnp.where(el2 == v2, e_io, EXPERTS_PER_GROUP), axis=-1, keepdims=True)
    t = jnp.exp(v2 - v1)
    w1 = g_w / (1.0 + t)
    w2 = g_w * t / (1.0 + t)
    swap = i2 < i1
    e_lo, e_hi = jnp.where(swap, i2, i1), jnp.where(swap, i1, i2)
    w_lo, w_hi = jnp.where(swap, w2, w1), jnp.where(swap, w1, w2)
    bucket = g_idx * (EXPERTS_PER_GROUP * EXPERTS_PER_GROUP) + e_lo * EXPERTS_PER_GROUP + e_hi
    lane = lax.broadcasted_iota(jnp.int32, (tm, LANES), 1)
    hx_ref[:, D_MODEL:] = jnp.where(lane == 0, w_lo, jnp.where(lane == 1, w_hi, 0.0))
    onehot = lax.broadcasted_iota(jnp.int32, (tm, N_BUCKETS), 1) == bucket
    before = jnp.dot(tri_ref[...], onehot.astype(BF16), preferred_element_type=F32)
    carry = carry_ref[...]
    rank = jnp.sum(jnp.where(onehot, before + carry, 0.0), axis=-1, keepdims=True)
    carry = carry + jnp.sum(onehot.astype(F32), axis=0, keepdims=True)
    carry_ref[...] = carry
    cnt_ref[...] = carry
    meta_ref[...] = jnp.where(lane == 0, bucket, jnp.where(lane == 1, rank.astype(jnp.int32), 0))


def _out_proj(x2, conv_n, attn, beta_attn, w_out_bf, g_ffn, w_router_bf, b_router, tm):
    t_tokens = x2.shape[0]
    row = lambda i: (i, 0)
    fixed = lambda i: (0, 0)
    tri = (jnp.arange(tm)[None, :] < jnp.arange(tm)[:, None]).astype(BF16)
    return pl.pallas_call(
        _out_proj_kernel,
        grid=(t_tokens // tm,),
        in_specs=[pl.BlockSpec((tm, D_MODEL), row), pl.BlockSpec((tm, CONV_CH), row),
                  pl.BlockSpec((tm, ATTN_WIDTH), row), pl.BlockSpec((1, ATTN_WIDTH), fixed),
                  pl.BlockSpec((CONV_CH, D_MODEL), fixed), pl.BlockSpec((ATTN_WIDTH, D_MODEL), fixed),
                  pl.BlockSpec((1, D_MODEL), fixed), pl.BlockSpec((D_MODEL, LANES), fixed),
                  pl.BlockSpec((1, LANES), fixed), pl.BlockSpec((tm, tm), fixed)],
        out_specs=[pl.BlockSpec((tm, D_MODEL), row), pl.BlockSpec((tm, HX_COLS), row),
                   pl.BlockSpec((tm, LANES), row), pl.BlockSpec((1, N_BUCKETS), fixed)],
        out_shape=[jax.ShapeDtypeStruct((t_tokens, D_MODEL), F32),
                   jax.ShapeDtypeStruct((t_tokens, HX_COLS), F32),
                   jax.ShapeDtypeStruct((t_tokens, LANES), jnp.int32),
                   jax.ShapeDtypeStruct((1, N_BUCKETS), F32)],
        scratch_shapes=[pltpu.VMEM((1, N_BUCKETS), F32)],
        compiler_params=pltpu.CompilerParams(dimension_semantics=("arbitrary",), vmem_limit_bytes=VMEM_LIMIT),
        name="out_proj_router",
    )(x2, conv_n, attn, beta_attn.reshape(1, -1), w_out_bf[:CONV_CH], w_out_bf[CONV_CH:],
      g_ffn.reshape(1, -1), w_router_bf, b_router, tri)


def _dispatch_kernel(pos_ref, hx_ref, out_hbm, sem):
    tm = hx_ref.shape[0]
    base = pl.program_id(0) * tm

    def issue(r, carry):
        dst = pos_ref[base + r]
        pltpu.make_async_copy(hx_ref.at[pl.ds(r, 1), :], out_hbm.at[pl.ds(dst, 1), :], sem).start()
        return carry

    lax.fori_loop(0, tm, issue, 0, unroll=8)

    def drain(r, carry):
        pltpu.make_async_copy(hx_ref.at[pl.ds(0, 1), :], out_hbm.at[pl.ds(0, 1), :], sem).wait()
        return carry

    lax.fori_loop(0, tm, drain, 0, unroll=8)


def _dispatch(pos, hx, tm):
    t_tokens = hx.shape[0]
    return pl.pallas_call(
        _dispatch_kernel,
        grid_spec=pltpu.PrefetchScalarGridSpec(
            num_scalar_prefetch=1,
            grid=(t_tokens // tm,),
            in_specs=[pl.BlockSpec((tm, HX_COLS), lambda i, pos: (i, 0))],
            out_specs=pl.BlockSpec(memory_space=pl.ANY),
            scratch_shapes=[pltpu.SemaphoreType.DMA(())],
        ),
        out_shape=jax.ShapeDtypeStruct((t_tokens, HX_COLS), F32),
        compiler_params=pltpu.CompilerParams(dimension_semantics=("arbitrary",), vmem_limit_bytes=VMEM_LIMIT,
                                             has_side_effects=True),
        name="moe_dispatch",
    )(pos, hx)


def _moe_schedule(counts, n_tiles, tm):
    n_steps = n_tiles + N_PAIR_BUCKETS
    ends = jnp.cumsum(counts)
    starts = ends - counts
    first_tile = starts // tm
    last_tile = (ends - 1) // tm
    steps_b = jnp.where(counts > 0, last_tile - first_tile + 1, 0)
    step_end = jnp.cumsum(steps_b)
    step_start = step_end - steps_b
    total = step_end[-1]
    s = jnp.minimum(jnp.arange(n_steps, dtype=jnp.int32), total - 1)
    b = jnp.searchsorted(step_end, s, side="right").astype(jnp.int32)
    tile = (first_tile[b] + (s - step_start[b])).astype(jnp.int32)
    live = jnp.arange(n_steps) < total
    row_lo = jnp.where(live, starts[b], 0).astype(jnp.int32)
    row_hi = jnp.where(live, ends[b], 0).astype(jnp.int32)
    group = b // (EXPERTS_PER_GROUP * EXPERTS_PER_GROUP)
    e_lo = (group * EXPERTS_PER_GROUP + (b // EXPERTS_PER_GROUP) % EXPERTS_PER_GROUP).astype(jnp.int32)
    e_hi = (group * EXPERTS_PER_GROUP + b % EXPERTS_PER_GROUP).astype(jnp.int32)
    return starts, (tile, e_lo, e_hi, row_lo, row_hi, total.astype(jnp.int32).reshape(1))


def _moe_kernel(tile_ref, elo_ref, ehi_ref, rlo_ref, rhi_ref, total_ref,
                hx_ref, wgl_ref, wul_ref, wdl_ref, wgh_ref, wuh_ref, wdh_ref, y_ref):
    s = pl.program_id(0)
    tm = hx_ref.shape[0]
    tile = tile_ref[s]
    first = jnp.logical_or(s == 0, tile != tile_ref[jnp.maximum(s - 1, 0)])

    @pl.when(s < total_ref[0])
    def _():
        rows = tile * tm + lax.broadcasted_iota(jnp.int32, (tm, 1), 0)
        inside = (rows >= rlo_ref[s]) & (rows < rhi_ref[s])
        h = hx_ref[:, :D_MODEL].astype(BF16)
        w_lo = jnp.where(inside, hx_ref[:, D_MODEL:D_MODEL + 1], 0.0)
        w_hi = jnp.where(inside, hx_ref[:, D_MODEL + 1:D_MODEL + 2], 0.0)

        def expert(wg_ref, wu_ref, wd_ref, w_row):
            a = jnp.dot(h, wg_ref[0], preferred_element_type=F32)
            b = jnp.dot(h, wu_ref[0], preferred_element_type=F32)
            hid = a * jax.nn.sigmoid(a) * b * w_row
            return jnp.dot(hid.astype(BF16), wd_ref[0], preferred_element_type=F32)

        y = expert(wgl_ref, wul_ref, wdl_ref, w_lo) + expert(wgh_ref, wuh_ref, wdh_ref, w_hi)

        @pl.when(first)
        def _():
            y_ref[...] = y

        @pl.when(jnp.logical_not(first))
        def _():
            y_ref[...] += y


def _moe(schedule, hx_sorted, wg_bf, wu_bf, wd_bf, tm):
    t_tokens = hx_sorted.shape[0]
    n_steps = schedule[0].shape[0]
    row = lambda s, tile, elo, ehi, rlo, rhi, tot: (tile[s], 0)
    w_lo = lambda s, tile, elo, ehi, rlo, rhi, tot: (elo[s], 0, 0)
    w_hi = lambda s, tile, elo, ehi, rlo, rhi, tot: (ehi[s], 0, 0)
    up_spec = lambda sel: pl.BlockSpec((1, D_MODEL, D_EXPERT), sel)
    down_spec = lambda sel: pl.BlockSpec((1, D_EXPERT, D_MODEL), sel)
    return pl.pallas_call(
        _moe_kernel,
        grid_spec=pltpu.PrefetchScalarGridSpec(
            num_scalar_prefetch=6,
            grid=(n_steps,),
            in_specs=[pl.BlockSpec((tm, HX_COLS), row), up_spec(w_lo), up_spec(w_lo), down_spec(w_lo),
                      up_spec(w_hi), up_spec(w_hi), down_spec(w_hi)],
            out_specs=pl.BlockSpec((tm, D_MODEL), row),
        ),
        out_shape=jax.ShapeDtypeStruct((t_tokens, D_MODEL), F32),
        compiler_params=pltpu.CompilerParams(dimension_semantics=("arbitrary",), vmem_limit_bytes=VMEM_LIMIT),
        name="hier_moe",
    )(*schedule, hx_sorted, wg_bf, wu_bf, wd_bf, wg_bf, wu_bf, wd_bf)


def _ple_final_kernel(pos_ref, x1_ref, ys_hbm, p_ref, gp_ref, wg_ref, bg_ref, wp_ref, gfin_ref, o_ref,
                      ybuf, sems):
    i = pl.program_id(0)
    n = pl.num_programs(0)
    tm = x1_ref.shape[0]

    def row_copy(src, r, slot):
        return pltpu.make_async_copy(ys_hbm.at[pl.ds(src, 1), :], ybuf.at[slot, pl.ds(r, 1), :], sems.at[slot])

    def gather(tile, slot):
        def issue(r, carry):
            row_copy(pos_ref[tile * tm + r], r, slot).start()
            return carry
        lax.fori_loop(0, tm, issue, 0, unroll=8)

    @pl.when(i == 0)
    def _():
        gather(0, 0)

    @pl.when(i + 1 < n)
    def _():
        gather(i + 1, (i + 1) % 2)

    slot = i % 2

    def drain(r, carry):
        row_copy(0, r, slot).wait()
        return carry

    lax.fori_loop(0, tm, drain, 0, unroll=8)
    x2 = x1_ref[...] + ybuf[slot]
    hg = _rms(x2, gp_ref[...]).astype(BF16)
    gate = jax.nn.sigmoid(jnp.dot(hg, wg_ref[...], preferred_element_type=F32) + bg_ref[...])
    emb = jnp.dot(p_ref[...].astype(BF16), wp_ref[...], preferred_element_type=F32)
    x3 = x2 + emb * gate
    o_ref[...] = _rms(x3, gfin_ref[...])


def _ple_final(pos, x1, y_sorted, p2, g_ple, w_gate_bf, b_gate, w_proj_bf, g_final, tm):
    t_tokens = x1.shape[0]
    row = lambda i, pos: (i, 0)
    fixed = lambda i, pos: (0, 0)
    vec = pl.BlockSpec((1, D_MODEL), fixed)
    return pl.pallas_call(
        _ple_final_kernel,
        grid_spec=pltpu.PrefetchScalarGridSpec(
            num_scalar_prefetch=1,
            grid=(t_tokens // tm,),
            in_specs=[pl.BlockSpec((tm, D_MODEL), row), pl.BlockSpec(memory_space=pl.ANY),
                      pl.BlockSpec((tm, PLE_DIM), row), vec, pl.BlockSpec((D_MODEL, D_MODEL), fixed), vec,
                      pl.BlockSpec((PLE_DIM, D_MODEL), fixed), vec],
            out_specs=pl.BlockSpec((tm, D_MODEL), row),
            scratch_shapes=[pltpu.VMEM((2, tm, D_MODEL), F32), pltpu.SemaphoreType.DMA((2,))],
        ),
        out_shape=jax.ShapeDtypeStruct((t_tokens, D_MODEL), F32),
        compiler_params=pltpu.CompilerParams(dimension_semantics=("arbitrary",), vmem_limit_bytes=VMEM_LIMIT),
        name="ple_final",
    )(pos, x1, y_sorted, p2, g_ple.reshape(1, -1), w_gate_bf, b_gate.reshape(1, -1), w_proj_bf,
      g_final.reshape(1, -1))


def _layer(x2, p2, batch, seq, g_mix, w_in, w_dw, b_dw, conv_ln_g, conv_ln_b, beta_conv, beta_attn, w_out,
           g_ffn, w_group, b_group, w_erouter, b_erouter, w_gate, w_up, w_down,
           g_ple, w_ple_gate, b_ple_gate, w_ple_proj):
    u, qp, kp, vp = _in_proj(x2, g_mix, w_in.astype(BF16), seq, tm=512)
    conv_n = _conv(u, w_dw, b_dw, conv_ln_g, conv_ln_b, beta_conv, batch, seq, ts=256)
    attn = _moba(qp, kp, vp, batch, seq)
    w_router = jnp.concatenate(
        [w_group, jnp.transpose(w_erouter, (1, 0, 2)).reshape(D_MODEL, N_EXPERTS),
         jnp.zeros((D_MODEL, LANES - N_GROUPS - N_EXPERTS), F32)], axis=1).astype(BF16)
    b_router = jnp.concatenate([b_group, b_erouter.reshape(-1),
                                jnp.zeros((LANES - N_GROUPS - N_EXPERTS,), F32)]).reshape(1, LANES)
    x1, hx, meta, counts = _out_proj(x2, conv_n, attn, beta_attn, w_out.astype(BF16), g_ffn, w_router, b_router,
                                     tm=512)
    moe_tm = 256
    starts, schedule = _moe_schedule(counts[0].astype(jnp.int32), x2.shape[0] // moe_tm, moe_tm)
    pos = (starts[meta[:, 0]] + meta[:, 1]).astype(jnp.int32)
    hx_sorted = _dispatch(pos, hx, tm=1024)
    y_sorted = _moe(schedule, hx_sorted, w_gate.astype(BF16), w_up.astype(BF16), w_down.astype(BF16), moe_tm)
    return x1, y_sorted, pos


def kernel(x, p, g_mix, w_in, w_dw, b_dw, conv_ln_g, conv_ln_b, beta_conv, beta_attn, w_out, g_ffn, w_group, b_group, w_erouter, b_erouter, w_gate, w_up, w_down, g_ple, w_ple_gate, b_ple_gate, w_ple_proj, g_final):
    batch, seq, _ = x.shape
    depth = p.shape[0]
    assert depth == 1 and seq % MOBA_BLOCK == 0
    x2 = x.reshape(batch * seq, D_MODEL)
    i = 0
    x1, y_sorted, pos = _layer(x2, None, batch, seq, g_mix[i], w_in[i], w_dw[i], b_dw[i], conv_ln_g[i],
                               conv_ln_b[i], beta_conv[i], beta_attn[i], w_out[i], g_ffn[i], w_group[i],
                               b_group[i], w_erouter[i], b_erouter[i], w_gate[i], w_up[i], w_down[i], g_ple[i],
                               w_ple_gate[i], b_ple_gate[i], w_ple_proj[i])
    out = _ple_final(pos, x1, y_sorted, p[i].reshape(batch * seq, PLE_DIM), g_ple[i],
                     w_ple_gate[i].astype(BF16), b_ple_gate[i], w_ple_proj[i].astype(BF16), g_final, tm=512)
    return out.reshape(batch, seq, D_MODEL)
```

```python
import jax
import jax.numpy as jnp
from jax import lax
from jax.experimental import pallas as pl
from jax.experimental.pallas import tpu as pltpu

F32 = jnp.float32
BF16 = jnp.bfloat16

D_MODEL = 1024
CONV_CH = 512
CONV_WIDTH = 31
ATTN_WIDTH = 512
HEAD_DIM = 64
N_HEADS = 8
ROT_DIM = 16
ROPE_THETA = 500000.0
MOBA_BLOCK = 256
MOBA_TOPK = 3
N_GROUPS = 4
EXPERTS_PER_GROUP = 8
N_EXPERTS = 32
D_EXPERT = 256
PLE_DIM = 256
N_BUCKETS = N_GROUPS * EXPERTS_PER_GROUP * EXPERTS_PER_GROUP
N_PAIR_BUCKETS = N_GROUPS * EXPERTS_PER_GROUP * (EXPERTS_PER_GROUP - 1) // 2
IN_COLS = 2 * CONV_CH + 3 * ATTN_WIDTH
EPS = 1e-6

LANES = 128
SUBLANES = 8
HEAD_PAD = LANES
QKV_PAD = N_HEADS * HEAD_PAD
ROW_CHUNKS = D_MODEL // LANES
assert ROW_CHUNKS == SUBLANES
MASK_NEG = -1e30
CONV_HALO = 32
CONV_ROWS = 32
VMEM_LIMIT = 56 * 1024 * 1024


def _rms(x, g):
    return x * lax.rsqrt(jnp.mean(x * x, axis=-1, keepdims=True) + EPS) * g


def _load_row_tiles(ref, n_rows):
    return jnp.concatenate([ref[pl.ds(c, n_rows, stride=ROW_CHUNKS), :] for c in range(ROW_CHUNKS)], axis=1)


def _store_row_tiles(ref, val, accumulate=False):
    n_rows = val.shape[0]
    for c in range(ROW_CHUNKS):
        idx = (pl.ds(c, n_rows, stride=ROW_CHUNKS), slice(None))
        piece = val[:, c * LANES:(c + 1) * LANES]
        ref[idx] = ref[idx] + piece if accumulate else piece


def _router_logits(h_bf16, wr_ref, br_ref):
    return jnp.dot(h_bf16, wr_ref[...], preferred_element_type=F32) + br_ref[...]


def _in_proj_kernel(x_ref, g_ref, w_ref, rc_ref, ra_ref, rb_ref, u_ref, q_ref, k_ref, v_ref):
    h = _rms(x_ref[...], g_ref[...]).astype(BF16)
    proj = jnp.dot(h, w_ref[...], preferred_element_type=F32)
    u_ref[...] = proj[:, :CONV_CH] * jax.nn.sigmoid(proj[:, CONV_CH:2 * CONV_CH])
    rc, ra, rb = rc_ref[...], ra_ref[...], rb_ref[...]
    low = lax.broadcasted_iota(jnp.int32, (x_ref.shape[0], LANES), 1) < HEAD_DIM

    def put(dst_ref, base, rope, scale):
        for c in range(ATTN_WIDTH // LANES):
            t = proj[:, base + c * LANES: base + (c + 1) * LANES]
            if rope:
                t = t * rc + pltpu.roll(t, LANES - ROT_DIM // 2, 1) * ra + pltpu.roll(t, ROT_DIM // 2, 1) * rb
            if scale != 1.0:
                t = t * scale
            even = jnp.where(low, t, 0.0)
            odd = jnp.where(low, pltpu.roll(t, HEAD_DIM, 1), 0.0)
            dst_ref[:, (2 * c) * HEAD_PAD:(2 * c + 1) * HEAD_PAD] = even.astype(dst_ref.dtype)
            dst_ref[:, (2 * c + 1) * HEAD_PAD:(2 * c + 2) * HEAD_PAD] = odd.astype(dst_ref.dtype)

    put(q_ref, 2 * CONV_CH, True, HEAD_DIM ** -0.5)
    put(k_ref, 2 * CONV_CH + ATTN_WIDTH, True, 1.0)
    put(v_ref, 2 * CONV_CH + 2 * ATTN_WIDTH, False, 1.0)


def _rope_coeff_tables(seq):
    pos = jnp.arange(seq, dtype=F32)
    inv_freq = jnp.power(jnp.float32(ROPE_THETA), -jnp.arange(0, ROT_DIM, 2, dtype=F32) / ROT_DIM)
    ang = pos[:, None] * inv_freq[None, :]
    cos, sin = jnp.cos(ang), jnp.sin(ang)
    zeros = jnp.zeros((seq, HEAD_DIM - ROT_DIM), F32)
    c_head = jnp.concatenate([cos, cos, jnp.ones_like(zeros)], axis=1)
    a_head = jnp.concatenate([-sin, jnp.zeros_like(sin), zeros], axis=1)
    b_head = jnp.concatenate([jnp.zeros_like(sin), sin, zeros], axis=1)
    rep = LANES // HEAD_DIM
    return jnp.tile(c_head, (1, rep)), jnp.tile(a_head, (1, rep)), jnp.tile(b_head, (1, rep))


def _in_proj(x2, g_mix, w_in_bf, seq, tm):
    t_tokens = x2.shape[0]
    rc, ra, rb = _rope_coeff_tables(seq)
    n_seq_tiles = seq // tm
    row = lambda i: (i, 0)
    fixed = lambda i: (0, 0)
    tab = lambda i: (i % n_seq_tiles, 0)
    return pl.pallas_call(
        _in_proj_kernel,
        grid=(t_tokens // tm,),
        in_specs=[pl.BlockSpec((tm, D_MODEL), row), pl.BlockSpec((1, D_MODEL), fixed),
                  pl.BlockSpec((D_MODEL, IN_COLS), fixed),
                  pl.BlockSpec((tm, LANES), tab), pl.BlockSpec((tm, LANES), tab), pl.BlockSpec((tm, LANES), tab)],
        out_specs=[pl.BlockSpec((tm, CONV_CH), row), pl.BlockSpec((tm, QKV_PAD), row),
                   pl.BlockSpec((tm, QKV_PAD), row), pl.BlockSpec((tm, QKV_PAD), row)],
        out_shape=[jax.ShapeDtypeStruct((t_tokens, CONV_CH), F32),
                   jax.ShapeDtypeStruct((t_tokens, QKV_PAD), BF16),
                   jax.ShapeDtypeStruct((t_tokens, QKV_PAD), BF16),
                   jax.ShapeDtypeStruct((t_tokens, QKV_PAD), BF16)],
        compiler_params=pltpu.CompilerParams(dimension_semantics=("arbitrary",), vmem_limit_bytes=VMEM_LIMIT),
        name="in_proj",
    )(x2, g_mix.reshape(1, D_MODEL), w_in_bf, rc, ra, rb)


def _conv_kernel(u_ref, w_ref, b_ref, lg_ref, lb_ref, beta_ref, o_ref, pad_ref):
    ts = u_ref.shape[0]

    @pl.when(pl.program_id(1) == 0)
    def _():
        pad_ref[0:CONV_HALO, :] = jnp.zeros((CONV_HALO, CONV_CH), F32)

    pad_ref[CONV_HALO:CONV_HALO + ts, :] = u_ref[...]
    w = w_ref[...]
    first = CONV_HALO - (CONV_WIDTH - 1)
    for r0 in range(0, ts, CONV_ROWS):
        acc = jnp.zeros((CONV_ROWS, CONV_CH), F32)
        for tap in range(CONV_WIDTH):
            acc = acc + pad_ref[r0 + first + tap: r0 + first + tap + CONV_ROWS, :] * w[tap:tap + 1, :]
        y = acc + b_ref[...]
        mu = jnp.mean(y, axis=-1, keepdims=True)
        d = y - mu
        var = jnp.mean(d * d, axis=-1, keepdims=True)
        y = d * lax.rsqrt(var + EPS) * lg_ref[...] + lb_ref[...]
        y = y * jax.nn.sigmoid(y)
        o_ref[r0:r0 + CONV_ROWS, :] = _rms(y, beta_ref[...]).astype(o_ref.dtype)
    pad_ref[0:CONV_HALO, :] = pad_ref[ts:ts + CONV_HALO, :]


def _conv(u2, w_dw, b_dw, ln_g, ln_b, beta, batch, seq, ts):
    n_s = seq // ts
    row = lambda b, s: (b * n_s + s, 0)
    fixed = lambda b, s: (0, 0)
    vec = pl.BlockSpec((1, CONV_CH), fixed)
    return pl.pallas_call(
        _conv_kernel,
        grid=(batch, n_s),
        in_specs=[pl.BlockSpec((ts, CONV_CH), row), pl.BlockSpec((CONV_WIDTH, CONV_CH), fixed), vec, vec, vec, vec],
        out_specs=pl.BlockSpec((ts, CONV_CH), row),
        out_shape=jax.ShapeDtypeStruct((batch * seq, CONV_CH), BF16),
        scratch_shapes=[pltpu.VMEM((ts + CONV_HALO, CONV_CH), F32)],
        compiler_params=pltpu.CompilerParams(dimension_semantics=("arbitrary", "arbitrary"),
                                             vmem_limit_bytes=VMEM_LIMIT),
        name="conformer_conv",
    )(u2, w_dw, b_dw.reshape(1, -1), ln_g.reshape(1, -1), ln_b.reshape(1, -1), beta.reshape(1, -1))


def _moba_kernel(q_ref, k_ref, v_ref, blk_ref, place_ref, o_ref):
    seq = q_ref.shape[0]
    nb = seq // MOBA_BLOCK
    blk_onehot = blk_ref[...]
    place = place_ref[...]
    sub = lax.broadcasted_iota(jnp.int32, (nb, MOBA_BLOCK), 0)
    r_io = lax.broadcasted_iota(jnp.int32, (MOBA_BLOCK, MOBA_BLOCK), 0)
    c_io = lax.broadcasted_iota(jnp.int32, (MOBA_BLOCK, MOBA_BLOCK), 1)
    causal = c_io <= r_io
    heads = []
    for hh in range(2):
        lo, hi = hh * HEAD_PAD, (hh + 1) * HEAD_PAD
        k = k_ref[:, lo:hi]
        v = v_ref[:, lo:hi]
        k_aug = k + blk_onehot
        kf = k.astype(F32)
        k_mean = jnp.concatenate(
            [jnp.mean(kf[j * MOBA_BLOCK:(j + 1) * MOBA_BLOCK], axis=0, keepdims=True) for j in range(nb)], axis=0)
        km_hi = k_mean.astype(BF16)
        km_lo = (k_mean - km_hi.astype(F32)).astype(BF16)
        outs = []
        for i in range(nb):
            q = q_ref[i * MOBA_BLOCK:(i + 1) * MOBA_BLOCK, lo:hi]
            if i > 0:
                dn = (((1,), (1,)), ((), ()))
                gate = (lax.dot_general(km_hi, q, dn, preferred_element_type=F32)
                        + lax.dot_general(km_lo, q, dn, preferred_element_type=F32))
                past = sub < i
                sel_rows = []
                for j in range(i):
                    gj = gate[j:j + 1, :]
                    beats = ((gate > gj) | ((gate == gj) & (sub < j))) & past
                    cnt = jnp.sum(beats.astype(F32), axis=0, keepdims=True)
                    sel_rows.append(cnt < float(MOBA_TOPK))
                bias_t = jnp.where(sub == i, 0.0, MASK_NEG)
                for j in range(i):
                    bias_t = jnp.where((sub == j) & sel_rows[j], 0.0, bias_t)
                bias = jnp.dot(bias_t.T, place, preferred_element_type=F32)
                q_aug = q + bias.astype(BF16)
                kn = (i + 1) * MOBA_BLOCK
                s = lax.dot_general(q_aug, k_aug[:kn], (((1,), (1,)), ((), ())), preferred_element_type=F32)
                s_past = s[:, :i * MOBA_BLOCK]
                s_own = jnp.where(causal, s[:, i * MOBA_BLOCK:], MASK_NEG)
                m = jnp.maximum(jnp.max(s_past, axis=-1, keepdims=True), jnp.max(s_own, axis=-1, keepdims=True))
                p_past = jnp.exp(s_past - m)
                p_own = jnp.exp(s_own - m)
                l = jnp.sum(p_past, axis=-1, keepdims=True) + jnp.sum(p_own, axis=-1, keepdims=True)
                acc = (jnp.dot(p_past.astype(BF16), v[:i * MOBA_BLOCK], preferred_element_type=F32)
                       + jnp.dot(p_own.astype(BF16), v[i * MOBA_BLOCK:kn], preferred_element_type=F32))
            else:
                s = lax.dot_general(q, k[:MOBA_BLOCK], (((1,), (1,)), ((), ())), preferred_element_type=F32)
                s_own = jnp.where(causal, s, MASK_NEG)
                m = jnp.max(s_own, axis=-1, keepdims=True)
                p_own = jnp.exp(s_own - m)
                l = jnp.sum(p_own, axis=-1, keepdims=True)
                acc = jnp.dot(p_own.astype(BF16), v[:MOBA_BLOCK], preferred_element_type=F32)
            outs.append(acc / l)
        heads.append(outs)
    for i in range(nb):
        o_ref[i * MOBA_BLOCK:(i + 1) * MOBA_BLOCK, :] = heads[0][i] + pltpu.roll(heads[1][i], HEAD_DIM, 1)


def _moba(qp, kp, vp, batch, seq):
    nb = seq // MOBA_BLOCK
    lane = jnp.arange(LANES)[None, :]
    blk = (jnp.arange(seq) // MOBA_BLOCK)[:, None]
    blk_onehot = (lane == HEAD_DIM + blk).astype(BF16)
    place = (lane == HEAD_DIM + jnp.arange(nb)[:, None]).astype(F32)
    pair = lambda b, c: (b, c)
    fixed = lambda b, c: (0, 0)
    spec = pl.BlockSpec((seq, 2 * HEAD_PAD), pair)
    return pl.pallas_call(
        _moba_kernel,
        grid=(batch, N_HEADS // 2),
        in_specs=[spec, spec, spec, pl.BlockSpec((seq, LANES), fixed), pl.BlockSpec((nb, LANES), fixed)],
        out_specs=pl.BlockSpec((seq, LANES), pair),
        out_shape=jax.ShapeDtypeStruct((batch * seq, ATTN_WIDTH), F32),
        compiler_params=pltpu.CompilerParams(dimension_semantics=("arbitrary", "arbitrary"),
                                             vmem_limit_bytes=VMEM_LIMIT),
        name="moba_attention",
    )(qp, kp, vp, blk_onehot, place)


def _out_proj_kernel(x_ref, c_ref, a_ref, beta_ref, wc_ref, wa_ref, gf_ref, wr_ref, br_ref, tri_ref,
                     x1_ref, meta_ref, cnt_ref, carry_ref):
    @pl.when(pl.program_id(0) == 0)
    def _():
        carry_ref[...] = jnp.zeros_like(carry_ref)

    an = _rms(a_ref[...], beta_ref[...]).astype(BF16)
    x1 = (x_ref[...] + jnp.dot(c_ref[...], wc_ref[...], preferred_element_type=F32)
          + jnp.dot(an, wa_ref[...], preferred_element_type=F32))
    _store_row_tiles(x1_ref, x1)
    logits = _router_logits(_rms(x1, gf_ref[...]).astype(BF16), wr_ref, br_ref)
    tm = logits.shape[0]
    gl = logits[:, :N_GROUPS]
    g_io = lax.broadcasted_iota(jnp.int32, (tm, N_GROUPS), 1)
    g_max = jnp.max(gl, axis=-1, keepdims=True)
    g_idx = jnp.min(jnp.where(gl == g_max, g_io, N_GROUPS), axis=-1, keepdims=True)
    el = jnp.zeros((tm, EXPERTS_PER_GROUP), F32)
    for g in range(N_GROUPS):
        lo = N_GROUPS + g * EXPERTS_PER_GROUP
        el = jnp.where(g_idx == g, logits[:, lo:lo + EXPERTS_PER_GROUP], el)
    e_io = lax.broadcasted_iota(jnp.int32, (tm, EXPERTS_PER_GROUP), 1)
    v1 = jnp.max(el, axis=-1, keepdims=True)
    i1 = jnp.min(jnp.where(el == v1, e_io, EXPERTS_PER_GROUP), axis=-1, keepdims=True)
    el2 = jnp.where(e_io == i1, -jnp.inf, el)
    v2 = jnp.max(el2, axis=-1, keepdims=True)
    i2 = jnp.min(jnp.where(el2 == v2, e_io, EXPERTS_PER_GROUP), axis=-1, keepdims=True)
    bucket = (g_idx * (EXPERTS_PER_GROUP * EXPERTS_PER_GROUP) + jnp.minimum(i1, i2) * EXPERTS_PER_GROUP
              + jnp.maximum(i1, i2))
    onehot = lax.broadcasted_iota(jnp.int32, (tm, N_BUCKETS), 1) == bucket
    before = jnp.dot(tri_ref[...], onehot.astype(BF16), preferred_element_type=F32)
    carry = carry_ref[...]
    rank = jnp.sum(jnp.where(onehot, before + carry, 0.0), axis=-1, keepdims=True)
    carry = carry + jnp.sum(onehot.astype(F32), axis=0, keepdims=True)
    carry_ref[...] = carry
    cnt_ref[...] = carry
    lane = lax.broadcasted_iota(jnp.int32, (tm, LANES), 1)
    meta_ref[...] = jnp.where(lane == 0, bucket, jnp.where(lane == 1, rank.astype(jnp.int32), 0))


def _out_proj(x2, conv_n, attn, beta_attn, w_out_bf, g_ffn, w_router_bf, b_router, tm):
    t_tokens = x2.shape[0]
    row = lambda i: (i, 0)
    fixed = lambda i: (0, 0)
    tri = (jnp.arange(tm)[None, :] < jnp.arange(tm)[:, None]).astype(BF16)
    return pl.pallas_call(
        _out_proj_kernel,
        grid=(t_tokens // tm,),
        in_specs=[pl.BlockSpec((tm, D_MODEL), row), pl.BlockSpec((tm, CONV_CH), row),
                  pl.BlockSpec((tm, ATTN_WIDTH), row), pl.BlockSpec((1, ATTN_WIDTH), fixed),
                  pl.BlockSpec((CONV_CH, D_MODEL), fixed), pl.BlockSpec((ATTN_WIDTH, D_MODEL), fixed),
                  pl.BlockSpec((1, D_MODEL), fixed), pl.BlockSpec((D_MODEL, LANES), fixed),
                  pl.BlockSpec((1, LANES), fixed), pl.BlockSpec((tm, tm), fixed)],
        out_specs=[pl.BlockSpec((tm * ROW_CHUNKS, LANES), row), pl.BlockSpec((tm, LANES), row),
                   pl.BlockSpec((1, N_BUCKETS), fixed)],
        out_shape=[jax.ShapeDtypeStruct((t_tokens * ROW_CHUNKS, LANES), F32),
                   jax.ShapeDtypeStruct((t_tokens, LANES), jnp.int32),
                   jax.ShapeDtypeStruct((1, N_BUCKETS), F32)],
        scratch_shapes=[pltpu.VMEM((1, N_BUCKETS), F32)],
        compiler_params=pltpu.CompilerParams(dimension_semantics=("arbitrary",), vmem_limit_bytes=VMEM_LIMIT),
        name="out_proj_router",
    )(x2, conv_n, attn, beta_attn.reshape(1, -1), w_out_bf[:CONV_CH], w_out_bf[CONV_CH:],
      g_ffn.reshape(1, -1), w_router_bf, b_router, tri)


def _positions_kernel(meta_ref, starts_ref, pos_ref):
    meta_t = meta_ref[...].T
    bucket_t = meta_t[0:1, :]
    rank_t = meta_t[1:2, :]
    ids = lax.broadcasted_iota(jnp.int32, (N_BUCKETS, meta_t.shape[1]), 0)
    start_t = jnp.sum(jnp.where(ids == bucket_t, starts_ref[...], 0), axis=0, keepdims=True)
    pos_ref[0] = start_t + rank_t


def _positions(meta, starts, tm):
    t_tokens = meta.shape[0]
    pos = pl.pallas_call(
        _positions_kernel,
        grid=(t_tokens // tm,),
        in_specs=[pl.BlockSpec((tm, LANES), lambda i: (i, 0)), pl.BlockSpec((N_BUCKETS, 1), lambda i: (0, 0))],
        out_specs=pl.BlockSpec((1, 1, tm), lambda i: (i, 0, 0)),
        out_shape=jax.ShapeDtypeStruct((t_tokens // tm, 1, tm), jnp.int32),
        compiler_params=pltpu.CompilerParams(dimension_semantics=("arbitrary",), vmem_limit_bytes=VMEM_LIMIT),
        name="moe_positions",
    )(meta, starts.reshape(N_BUCKETS, 1))
    return pos.reshape(t_tokens)


def _row_tile(ref, r):
    return ref.at[pl.ds(pl.multiple_of(r * ROW_CHUNKS, ROW_CHUNKS), ROW_CHUNKS), :]


def _dispatch_kernel(pos_ref, x_ref, out_hbm, sem):
    tm = x_ref.shape[0] // ROW_CHUNKS
    base = pl.program_id(0) * tm

    def row_copy(r, dst):
        return pltpu.make_async_copy(_row_tile(x_ref, r), _row_tile(out_hbm, dst), sem)

    def issue(r, carry):
        row_copy(r, pos_ref[base + r]).start()
        return carry

    lax.fori_loop(0, tm, issue, 0, unroll=8)

    def drain(r, carry):
        row_copy(0, 0).wait()
        return carry

    lax.fori_loop(0, tm, drain, 0, unroll=8)


def _dispatch(pos, x_rows, tm):
    n_rows = x_rows.shape[0]
    return pl.pallas_call(
        _dispatch_kernel,
        grid_spec=pltpu.PrefetchScalarGridSpec(
            num_scalar_prefetch=1,
            grid=(n_rows // (tm * ROW_CHUNKS),),
            in_specs=[pl.BlockSpec((tm * ROW_CHUNKS, LANES), lambda i, pos: (i, 0))],
            out_specs=pl.BlockSpec(memory_space=pl.ANY),
            scratch_shapes=[pltpu.SemaphoreType.DMA(())],
        ),
        out_shape=jax.ShapeDtypeStruct((n_rows, LANES), F32),
        compiler_params=pltpu.CompilerParams(dimension_semantics=("arbitrary",), vmem_limit_bytes=VMEM_LIMIT,
                                             has_side_effects=True),
        name="moe_dispatch",
    )(pos, x_rows)


def _moe_schedule(counts, n_tiles, tm):
    n_steps = n_tiles + N_PAIR_BUCKETS
    ends = jnp.cumsum(counts)
    starts = ends - counts
    first_tile = starts // tm
    steps_b = jnp.where(counts > 0, (ends - 1) // tm - first_tile + 1, 0)
    step_end = jnp.cumsum(steps_b)
    step_start = step_end - steps_b
    total = step_end[-1]
    s = jnp.minimum(jnp.arange(n_steps, dtype=jnp.int32), total - 1)[:, None]
    mine = (s >= step_start[None, :]) & (s < step_end[None, :])
    pick = lambda v: jnp.sum(jnp.where(mine, v[None, :], 0), axis=1).astype(jnp.int32)
    ids = jnp.arange(N_BUCKETS, dtype=jnp.int32)
    pair = EXPERTS_PER_GROUP * EXPERTS_PER_GROUP
    tile = pick(first_tile - step_start) + s[:, 0]
    return starts, (tile, pick(ids // pair), pick((ids % pair) // EXPERTS_PER_GROUP), pick(ids % EXPERTS_PER_GROUP),
                    pick(starts), pick(ends), total.astype(jnp.int32).reshape(1))


def _moe_kernel(tile_ref, grp_ref, elo_ref, ehi_ref, rlo_ref, rhi_ref, total_ref,
                x_ref, gf_ref, wr_ref, br_ref, wgl_ref, wul_ref, wdl_ref, wgh_ref, wuh_ref, wdh_ref, y_ref):
    s = pl.program_id(0)
    tm = x_ref.shape[0] // ROW_CHUNKS
    tile = tile_ref[s]
    first = jnp.logical_or(s == 0, tile != tile_ref[jnp.maximum(s - 1, 0)])

    @pl.when(s < total_ref[0])
    def _():
        rows = tile * tm + lax.broadcasted_iota(jnp.int32, (tm, 1), 0)
        inside = (rows >= rlo_ref[s]) & (rows < rhi_ref[s])
        h = _rms(_load_row_tiles(x_ref, tm), gf_ref[...]).astype(BF16)
        logits = _router_logits(h, wr_ref, br_ref)
        lane = lax.broadcasted_iota(jnp.int32, (tm, LANES), 1)
        col = N_GROUPS + grp_ref[s] * EXPERTS_PER_GROUP
        lane_val = lambda idx: jnp.sum(jnp.where(lane == idx, logits, 0.0), axis=-1, keepdims=True)
        gl = jnp.where(lane < N_GROUPS, logits, -jnp.inf)
        g_w = 1.0 / jnp.sum(jnp.exp(gl - jnp.max(gl, axis=-1, keepdims=True)), axis=-1, keepdims=True)
        v_lo, v_hi = lane_val(col + elo_ref[s]), lane_val(col + ehi_ref[s])
        v_max = jnp.maximum(v_lo, v_hi)
        p_lo, p_hi = jnp.exp(v_lo - v_max), jnp.exp(v_hi - v_max)
        scale = jnp.where(inside, g_w / (p_lo + p_hi), 0.0)

        def expert(wg_ref, wu_ref, wd_ref, w_row):
            a = jnp.dot(h, wg_ref[0], preferred_element_type=F32)
            b = jnp.dot(h, wu_ref[0], preferred_element_type=F32)
            hid = a * jax.nn.sigmoid(a) * b * w_row
            return jnp.dot(hid.astype(BF16), wd_ref[0], preferred_element_type=F32)

        y = expert(wgl_ref, wul_ref, wdl_ref, p_lo * scale) + expert(wgh_ref, wuh_ref, wdh_ref, p_hi * scale)

        @pl.when(first)
        def _():
            _store_row_tiles(y_ref, y)

        @pl.when(jnp.logical_not(first))
        def _():
            _store_row_tiles(y_ref, y, accumulate=True)


def _moe(schedule, x_sorted, g_ffn, w_router_bf, b_router, wg_bf, wu_bf, wd_bf, tm):
    n_rows = x_sorted.shape[0]
    n_steps = schedule[0].shape[0]
    row = lambda s, tile, *_: (tile[s], 0)
    fixed = lambda s, *_: (0, 0)
    w_lo = lambda s, tile, grp, elo, ehi, *_: (grp[s] * EXPERTS_PER_GROUP + elo[s], 0, 0)
    w_hi = lambda s, tile, grp, elo, ehi, *_: (grp[s] * EXPERTS_PER_GROUP + ehi[s], 0, 0)
    up_spec = lambda sel: pl.BlockSpec((1, D_MODEL, D_EXPERT), sel)
    down_spec = lambda sel: pl.BlockSpec((1, D_EXPERT, D_MODEL), sel)
    return pl.pallas_call(
        _moe_kernel,
        grid_spec=pltpu.PrefetchScalarGridSpec(
            num_scalar_prefetch=7,
            grid=(n_steps,),
            in_specs=[pl.BlockSpec((tm * ROW_CHUNKS, LANES), row), pl.BlockSpec((1, D_MODEL), fixed),
                      pl.BlockSpec((D_MODEL, LANES), fixed), pl.BlockSpec((1, LANES), fixed),
                      up_spec(w_lo), up_spec(w_lo), down_spec(w_lo), up_spec(w_hi), up_spec(w_hi), down_spec(w_hi)],
            out_specs=pl.BlockSpec((tm * ROW_CHUNKS, LANES), row),
        ),
        out_shape=jax.ShapeDtypeStruct((n_rows, LANES), F32),
        compiler_params=pltpu.CompilerParams(dimension_semantics=("arbitrary",), vmem_limit_bytes=VMEM_LIMIT),
        name="hier_moe",
    )(*schedule, x_sorted, g_ffn.reshape(1, -1), w_router_bf, b_router, wg_bf, wu_bf, wd_bf, wg_bf, wu_bf, wd_bf)


def _ple_final_kernel(pos_ref, x1_ref, ys_hbm, p_ref, gp_ref, wg_ref, bg_ref, wp_ref, gfin_ref, o_ref,
                      ybuf, sems):
    i = pl.program_id(0)
    n = pl.num_programs(0)
    tm = o_ref.shape[0]

    def row_copy(src, r, slot):
        return pltpu.make_async_copy(_row_tile(ys_hbm, src), _row_tile(ybuf.at[slot], r), sems.at[slot])

    def gather(tile, slot):
        def issue(r, carry):
            row_copy(pos_ref[tile * tm + r], r, slot).start()
            return carry
        lax.fori_loop(0, tm, issue, 0, unroll=8)

    @pl.when(i == 0)
    def _():
        gather(0, 0)

    @pl.when(i + 1 < n)
    def _():
        gather(i + 1, (i + 1) % 2)

    slot = i % 2

    def drain(r, carry):
        row_copy(0, r, slot).wait()
        return carry

    lax.fori_loop(0, tm, drain, 0, unroll=8)
    x2 = _load_row_tiles(x1_ref, tm) + _load_row_tiles(ybuf.at[slot], tm)
    hg = _rms(x2, gp_ref[...]).astype(BF16)
    gate = jax.nn.sigmoid(jnp.dot(hg, wg_ref[...], preferred_element_type=F32) + bg_ref[...])
    emb = jnp.dot(p_ref[...].astype(BF16), wp_ref[...], preferred_element_type=F32)
    x3 = x2 + emb * gate
    o_ref[...] = _rms(x3, gfin_ref[...])


def _ple_final(pos, x1_rows, y_sorted, p2, g_ple, w_gate_bf, b_gate, w_proj_bf, g_final, tm):
    t_tokens = p2.shape[0]
    row = lambda i, pos: (i, 0)
    fixed = lambda i, pos: (0, 0)
    vec = pl.BlockSpec((1, D_MODEL), fixed)
    return pl.pallas_call(
        _ple_final_kernel,
        grid_spec=pltpu.PrefetchScalarGridSpec(
            num_scalar_prefetch=1,
            grid=(t_tokens // tm,),
            in_specs=[pl.BlockSpec((tm * ROW_CHUNKS, LANES), row), pl.BlockSpec(memory_space=pl.ANY),
                      pl.BlockSpec((tm, PLE_DIM), row), vec, pl.BlockSpec((D_MODEL, D_MODEL), fixed), vec,
                      pl.BlockSpec((PLE_DIM, D_MODEL), fixed), vec],
            out_specs=pl.BlockSpec((tm, D_MODEL), row),
            scratch_shapes=[pltpu.VMEM((2, tm * ROW_CHUNKS, LANES), F32), pltpu.SemaphoreType.DMA((2,))],
        ),
        out_shape=jax.ShapeDtypeStruct((t_tokens, D_MODEL), F32),
        compiler_params=pltpu.CompilerParams(dimension_semantics=("arbitrary",), vmem_limit_bytes=VMEM_LIMIT),
        name="ple_final",
    )(pos, x1_rows, y_sorted, p2, g_ple.reshape(1, -1), w_gate_bf, b_gate.reshape(1, -1), w_proj_bf,
      g_final.reshape(1, -1))


def kernel(x, p, g_mix, w_in, w_dw, b_dw, conv_ln_g, conv_ln_b, beta_conv, beta_attn, w_out, g_ffn, w_group, b_group, w_erouter, b_erouter, w_gate, w_up, w_down, g_ple, w_ple_gate, b_ple_gate, w_ple_proj, g_final):
    batch, seq, _ = x.shape
    assert p.shape[0] == 1 and seq % MOBA_BLOCK == 0
    t_tokens = batch * seq
    x2 = x.reshape(t_tokens, D_MODEL)
    i = 0
    u, qp, kp, vp = _in_proj(x2, g_mix[i], w_in[i].astype(BF16), seq, tm=512)
    conv_n = _conv(u, w_dw[i], b_dw[i], conv_ln_g[i], conv_ln_b[i], beta_conv[i], batch, seq, ts=256)
    attn = _moba(qp, kp, vp, batch, seq)
    w_router = jnp.concatenate(
        [w_group[i], jnp.transpose(w_erouter[i], (1, 0, 2)).reshape(D_MODEL, N_EXPERTS),
         jnp.zeros((D_MODEL, LANES - N_GROUPS - N_EXPERTS), F32)], axis=1).astype(BF16)
    b_router = jnp.concatenate([b_group[i], b_erouter[i].reshape(-1),
                                jnp.zeros((LANES - N_GROUPS - N_EXPERTS,), F32)]).reshape(1, LANES)
    x1_rows, meta, counts = _out_proj(x2, conv_n, attn, beta_attn[i], w_out[i].astype(BF16), g_ffn[i], w_router,
                                      b_router, tm=512)
    moe_tm = 256
    starts, schedule = _moe_schedule(counts[0].astype(jnp.int32), t_tokens // moe_tm, moe_tm)
    pos = _positions(meta, starts, tm=2048)
    x1_sorted = _dispatch(pos, x1_rows, tm=1024)
    y_sorted = _moe(schedule, x1_sorted, g_ffn[i], w_router, b_router, w_gate[i].astype(BF16),
                    w_up[i].astype(BF16), w_down[i].astype(BF16), moe_tm)
    out = _ple_final(pos, x1_rows, y_sorted, p[i].reshape(t_tokens, PLE_DIM), g_ple[i],
                     w_ple_gate[i].astype(BF16), b_ple_gate[i], w_ple_proj[i].astype(BF16), g_final, tm=512)
    return out.reshape(batch, seq, D_MODEL)
```

```python
import jax
import jax.numpy as jnp
from jax import lax
from jax.experimental import pallas as pl
from jax.experimental.pallas import tpu as pltpu

F32 = jnp.float32
BF16 = jnp.bfloat16

D_MODEL = 1024
CONV_CH = 512
CONV_WIDTH = 31
ATTN_WIDTH = 512
HEAD_DIM = 64
N_HEADS = 8
ROT_DIM = 16
ROPE_THETA = 500000.0
MOBA_BLOCK = 256
MOBA_TOPK = 3
N_GROUPS = 4
EXPERTS_PER_GROUP = 8
N_EXPERTS = 32
D_EXPERT = 256
PLE_DIM = 256
N_BUCKETS = N_GROUPS * EXPERTS_PER_GROUP * EXPERTS_PER_GROUP
N_PAIR_BUCKETS = N_GROUPS * EXPERTS_PER_GROUP * (EXPERTS_PER_GROUP - 1) // 2
IN_COLS = 2 * CONV_CH + 3 * ATTN_WIDTH
EPS = 1e-6

LANES = 128
SUBLANES = 8
HEAD_PAD = LANES
QKV_PAD = N_HEADS * HEAD_PAD
ROW_CHUNKS = D_MODEL // LANES
assert ROW_CHUNKS == SUBLANES
MASK_NEG = -1e30
Q_SCALE = HEAD_DIM ** -0.5 * 1.4426950408889634
CONV_HALO = 32
CONV_ROWS = 32
VMEM_LIMIT = 56 * 1024 * 1024


def _rms(x, g):
    return x * lax.rsqrt(jnp.mean(x * x, axis=-1, keepdims=True) + EPS) * g


def _load_row_tiles(ref, n_rows):
    return jnp.concatenate([ref[pl.ds(c, n_rows, stride=ROW_CHUNKS), :] for c in range(ROW_CHUNKS)], axis=1)


def _store_row_tiles(ref, val, accumulate=False):
    n_rows = val.shape[0]
    for c in range(ROW_CHUNKS):
        idx = (pl.ds(c, n_rows, stride=ROW_CHUNKS), slice(None))
        piece = val[:, c * LANES:(c + 1) * LANES]
        ref[idx] = ref[idx] + piece if accumulate else piece


def _router_logits(h_bf16, wr_ref, br_ref):
    return jnp.dot(h_bf16, wr_ref[...], preferred_element_type=F32) + br_ref[...]


def _in_proj_kernel(x_ref, g_ref, w_ref, rc_ref, ra_ref, rb_ref, u_ref, q_ref, k_ref, v_ref):
    h = _rms(x_ref[...], g_ref[...]).astype(BF16)
    proj = jnp.dot(h, w_ref[...], preferred_element_type=F32)
    u_ref[...] = proj[:, :CONV_CH] * jax.nn.sigmoid(proj[:, CONV_CH:2 * CONV_CH])
    rc, ra, rb = rc_ref[...], ra_ref[...], rb_ref[...]
    low = lax.broadcasted_iota(jnp.int32, (x_ref.shape[0], LANES), 1) < HEAD_DIM

    def put(dst_ref, base, rope, scale, spare):
        for c in range(ATTN_WIDTH // LANES):
            t = proj[:, base + c * LANES: base + (c + 1) * LANES]
            if rope:
                t = t * rc + pltpu.roll(t, LANES - ROT_DIM // 2, 1) * ra + pltpu.roll(t, ROT_DIM // 2, 1) * rb
            if scale != 1.0:
                t = t * scale
            even = jnp.where(low, t, spare)
            odd = jnp.where(low, pltpu.roll(t, HEAD_DIM, 1), spare)
            dst_ref[:, (2 * c) * HEAD_PAD:(2 * c + 1) * HEAD_PAD] = even.astype(dst_ref.dtype)
            dst_ref[:, (2 * c + 1) * HEAD_PAD:(2 * c + 2) * HEAD_PAD] = odd.astype(dst_ref.dtype)

    ones_lane = jnp.where(lax.broadcasted_iota(jnp.int32, (x_ref.shape[0], LANES), 1) == HEAD_DIM, 1.0, 0.0)
    put(q_ref, 2 * CONV_CH, True, Q_SCALE, 0.0)
    put(k_ref, 2 * CONV_CH + ATTN_WIDTH, True, 1.0, 0.0)
    put(v_ref, 2 * CONV_CH + 2 * ATTN_WIDTH, False, 1.0, ones_lane)


def _rope_coeff_tables(seq):
    pos = jnp.arange(seq, dtype=F32)
    inv_freq = jnp.power(jnp.float32(ROPE_THETA), -jnp.arange(0, ROT_DIM, 2, dtype=F32) / ROT_DIM)
    ang = pos[:, None] * inv_freq[None, :]
    cos, sin = jnp.cos(ang), jnp.sin(ang)
    zeros = jnp.zeros((seq, HEAD_DIM - ROT_DIM), F32)
    c_head = jnp.concatenate([cos, cos, jnp.ones_like(zeros)], axis=1)
    a_head = jnp.concatenate([-sin, jnp.zeros_like(sin), zeros], axis=1)
    b_head = jnp.concatenate([jnp.zeros_like(sin), sin, zeros], axis=1)
    rep = LANES // HEAD_DIM
    return jnp.tile(c_head, (1, rep)), jnp.tile(a_head, (1, rep)), jnp.tile(b_head, (1, rep))


def _in_proj(x2, g_mix, w_in_bf, seq, tm):
    t_tokens = x2.shape[0]
    rc, ra, rb = _rope_coeff_tables(seq)
    n_seq_tiles = seq // tm
    row = lambda i: (i, 0)
    fixed = lambda i: (0, 0)
    tab = lambda i: (i % n_seq_tiles, 0)
    return pl.pallas_call(
        _in_proj_kernel,
        grid=(t_tokens // tm,),
        in_specs=[pl.BlockSpec((tm, D_MODEL), row), pl.BlockSpec((1, D_MODEL), fixed),
                  pl.BlockSpec((D_MODEL, IN_COLS), fixed),
                  pl.BlockSpec((tm, LANES), tab), pl.BlockSpec((tm, LANES), tab), pl.BlockSpec((tm, LANES), tab)],
        out_specs=[pl.BlockSpec((tm, CONV_CH), row), pl.BlockSpec((tm, QKV_PAD), row),
                   pl.BlockSpec((tm, QKV_PAD), row), pl.BlockSpec((tm, QKV_PAD), row)],
        out_shape=[jax.ShapeDtypeStruct((t_tokens, CONV_CH), F32),
                   jax.ShapeDtypeStruct((t_tokens, QKV_PAD), BF16),
                   jax.ShapeDtypeStruct((t_tokens, QKV_PAD), BF16),
                   jax.ShapeDtypeStruct((t_tokens, QKV_PAD), BF16)],
        compiler_params=pltpu.CompilerParams(dimension_semantics=("arbitrary",), vmem_limit_bytes=VMEM_LIMIT),
        name="in_proj",
    )(x2, g_mix.reshape(1, D_MODEL), w_in_bf, rc, ra, rb)


def _conv_kernel(u_ref, w_ref, b_ref, lg_ref, lb_ref, beta_ref, o_ref, pad_ref, shift_ref):
    ts = u_ref.shape[0]

    @pl.when(pl.program_id(1) == 0)
    def _():
        pad_ref[0:CONV_HALO, :] = jnp.zeros((CONV_HALO, CONV_CH), F32)

    pad_ref[CONV_HALO:CONV_HALO + ts, :] = u_ref[...]
    kept = ts + CONV_HALO - SUBLANES
    for sh in range(1, SUBLANES):
        shift_ref[sh - 1, 0:kept, :] = pad_ref[sh:sh + kept, :]
    w = w_ref[...]
    first = CONV_HALO - (CONV_WIDTH - 1)

    def window(start):
        sh, base = start % SUBLANES, start - start % SUBLANES
        if sh == 0:
            return pad_ref[base:base + CONV_ROWS, :]
        return shift_ref[sh - 1, base:base + CONV_ROWS, :]

    for r0 in range(0, ts, CONV_ROWS):
        acc = jnp.zeros((CONV_ROWS, CONV_CH), F32)
        for tap in range(CONV_WIDTH):
            acc = acc + window(r0 + first + tap) * w[tap:tap + 1, :]
        y = acc + b_ref[...]
        mu = jnp.mean(y, axis=-1, keepdims=True)
        d = y - mu
        var = jnp.mean(d * d, axis=-1, keepdims=True)
        y = d * lax.rsqrt(var + EPS) * lg_ref[...] + lb_ref[...]
        y = y * jax.nn.sigmoid(y)
        o_ref[r0:r0 + CONV_ROWS, :] = _rms(y, beta_ref[...]).astype(o_ref.dtype)
    pad_ref[0:CONV_HALO, :] = pad_ref[ts:ts + CONV_HALO, :]


def _conv(u2, w_dw, b_dw, ln_g, ln_b, beta, batch, seq, ts):
    n_s = seq // ts
    row = lambda b, s: (b * n_s + s, 0)
    fixed = lambda b, s: (0, 0)
    vec = pl.BlockSpec((1, CONV_CH), fixed)
    return pl.pallas_call(
        _conv_kernel,
        grid=(batch, n_s),
        in_specs=[pl.BlockSpec((ts, CONV_CH), row), pl.BlockSpec((CONV_WIDTH, CONV_CH), fixed), vec, vec, vec, vec],
        out_specs=pl.BlockSpec((ts, CONV_CH), row),
        out_shape=jax.ShapeDtypeStruct((batch * seq, CONV_CH), BF16),
        scratch_shapes=[pltpu.VMEM((ts + CONV_HALO, CONV_CH), F32),
                        pltpu.VMEM((SUBLANES - 1, ts + CONV_HALO, CONV_CH), F32)],
        compiler_params=pltpu.CompilerParams(dimension_semantics=("arbitrary", "arbitrary"),
                                             vmem_limit_bytes=VMEM_LIMIT),
        name="conformer_conv",
    )(u2, w_dw, b_dw.reshape(1, -1), ln_g.reshape(1, -1), ln_b.reshape(1, -1), beta.reshape(1, -1))


def _moba_kernel(q_ref, k_ref, v_ref, blk_ref, place_ref, o_ref):
    seq = q_ref.shape[0]
    nb = seq // MOBA_BLOCK
    blk_onehot = blk_ref[...]
    place = place_ref[...]
    sub = lax.broadcasted_iota(jnp.int32, (nb, seq), 0)
    q_blk = lax.broadcasted_iota(jnp.int32, (nb, seq), 1) // MOBA_BLOCK
    past = sub < q_blk
    r_io = lax.broadcasted_iota(jnp.int32, (MOBA_BLOCK, MOBA_BLOCK), 0)
    c_io = lax.broadcasted_iota(jnp.int32, (MOBA_BLOCK, MOBA_BLOCK), 1)
    causal = c_io <= r_io
    low = lax.broadcasted_iota(jnp.int32, (MOBA_BLOCK, LANES), 1) < HEAD_DIM
    contract_lanes = (((1,), (1,)), ((), ()))
    q_aug, k_aug = [], []
    for hh in range(2):
        lo, hi = hh * HEAD_PAD, (hh + 1) * HEAD_PAD
        k = k_ref[:, lo:hi]
        q = q_ref[:, lo:hi]
        kf = k.astype(F32)
        k_mean = jnp.concatenate(
            [jnp.mean(kf[j * MOBA_BLOCK:(j + 1) * MOBA_BLOCK], axis=0, keepdims=True) for j in range(nb)], axis=0)
        km_hi = k_mean.astype(BF16)
        km_lo = (k_mean - km_hi.astype(F32)).astype(BF16)
        gate = (lax.dot_general(km_hi, q, contract_lanes, preferred_element_type=F32)
                + lax.dot_general(km_lo, q, contract_lanes, preferred_element_type=F32))
        bias_t = jnp.where(sub == q_blk, 0.0, MASK_NEG)
        for j in range(nb - 1):
            gj = gate[j:j + 1, :]
            beats = ((gate > gj) | ((gate == gj) & (sub < j))) & past
            cnt = jnp.sum(beats.astype(F32), axis=0, keepdims=True)
            bias_t = jnp.where((sub == j) & past & (cnt < float(MOBA_TOPK)), 0.0, bias_t)
        bias = jnp.dot(bias_t.T, place, preferred_element_type=F32)
        q_aug.append(q + bias.astype(BF16))
        k_aug.append(k + blk_onehot)

    def scores(hh, i):
        return lax.dot_general(q_aug[hh][i * MOBA_BLOCK:(i + 1) * MOBA_BLOCK], k_aug[hh][:(i + 1) * MOBA_BLOCK],
                               contract_lanes, preferred_element_type=F32)

    def attend(s, hh, i):
        v_lo = hh * HEAD_PAD
        s_own = jnp.where(causal, s[:, i * MOBA_BLOCK:], MASK_NEG)
        m = jnp.max(s_own, axis=-1, keepdims=True)
        if i > 0:
            s_past = s[:, :i * MOBA_BLOCK]
            m = jnp.maximum(m, jnp.max(s_past, axis=-1, keepdims=True))
        acc = jnp.dot(jnp.exp2((s_own - m).astype(BF16)),
                      v_ref[i * MOBA_BLOCK:(i + 1) * MOBA_BLOCK, v_lo:v_lo + HEAD_PAD], preferred_element_type=F32)
        if i > 0:
            acc = acc + jnp.dot(jnp.exp2((s_past - m).astype(BF16)), v_ref[:i * MOBA_BLOCK, v_lo:v_lo + HEAD_PAD],
                                preferred_element_type=F32)
        return jnp.where(low, acc / acc[:, HEAD_DIM:HEAD_DIM + 1], 0.0)

    items = [(hh, i) for i in range(nb) for hh in range(2)]
    s_cur = scores(*items[0])
    even = None
    for n, (hh, i) in enumerate(items):
        s_next = scores(*items[n + 1]) if n + 1 < len(items) else None
        out = attend(s_cur, hh, i)
        if hh == 0:
            even = out
        else:
            o_ref[i * MOBA_BLOCK:(i + 1) * MOBA_BLOCK, :] = even + pltpu.roll(out, HEAD_DIM, 1)
        s_cur = s_next


def _moba(qp, kp, vp, batch, seq):
    nb = seq // MOBA_BLOCK
    lane = jnp.arange(LANES)[None, :]
    blk = (jnp.arange(seq) // MOBA_BLOCK)[:, None]
    blk_onehot = (lane == HEAD_DIM + blk).astype(BF16)
    place = (lane == HEAD_DIM + jnp.arange(nb)[:, None]).astype(F32)
    pair = lambda b, c: (b, c)
    fixed = lambda b, c: (0, 0)
    spec = pl.BlockSpec((seq, 2 * HEAD_PAD), pair)
    return pl.pallas_call(
        _moba_kernel,
        grid=(batch, N_HEADS // 2),
        in_specs=[spec, spec, spec, pl.BlockSpec((seq, LANES), fixed), pl.BlockSpec((nb, LANES), fixed)],
        out_specs=pl.BlockSpec((seq, LANES), pair),
        out_shape=jax.ShapeDtypeStruct((batch * seq, ATTN_WIDTH), F32),
        compiler_params=pltpu.CompilerParams(dimension_semantics=("arbitrary", "arbitrary"),
                                             vmem_limit_bytes=VMEM_LIMIT),
        name="moba_attention",
    )(qp, kp, vp, blk_onehot, place)


def _out_proj_kernel(x_ref, c_ref, a_ref, beta_ref, wc_ref, wa_ref, gf_ref, wr_ref, br_ref, tri_ref,
                     x1_ref, meta_ref, cnt_ref, carry_ref):
    @pl.when(pl.program_id(0) == 0)
    def _():
        carry_ref[...] = jnp.zeros_like(carry_ref)

    an = _rms(a_ref[...], beta_ref[...]).astype(BF16)
    x1 = (x_ref[...] + jnp.dot(c_ref[...], wc_ref[...], preferred_element_type=F32)
          + jnp.dot(an, wa_ref[...], preferred_element_type=F32))
    _store_row_tiles(x1_ref, x1)
    logits = _router_logits(_rms(x1, gf_ref[...]).astype(BF16), wr_ref, br_ref)
    tm = logits.shape[0]
    gl = logits[:, :N_GROUPS]
    g_io = lax.broadcasted_iota(jnp.int32, (tm, N_GROUPS), 1)
    g_max = jnp.max(gl, axis=-1, keepdims=True)
    g_idx = jnp.min(jnp.where(gl == g_max, g_io, N_GROUPS), axis=-1, keepdims=True)
    el = jnp.zeros((tm, EXPERTS_PER_GROUP), F32)
    for g in range(N_GROUPS):
        lo = N_GROUPS + g * EXPERTS_PER_GROUP
        el = jnp.where(g_idx == g, logits[:, lo:lo + EXPERTS_PER_GROUP], el)
    e_io = lax.broadcasted_iota(jnp.int32, (tm, EXPERTS_PER_GROUP), 1)
    v1 = jnp.max(el, axis=-1, keepdims=True)
    i1 = jnp.min(jnp.where(el == v1, e_io, EXPERTS_PER_GROUP), axis=-1, keepdims=True)
    el2 = jnp.where(e_io == i1, -jnp.inf, el)
    v2 = jnp.max(el2, axis=-1, keepdims=True)
    i2 = jnp.min(jnp.where(el2 == v2, e_io, EXPERTS_PER_GROUP), axis=-1, keepdims=True)
    bucket = (g_idx * (EXPERTS_PER_GROUP * EXPERTS_PER_GROUP) + jnp.minimum(i1, i2) * EXPERTS_PER_GROUP
              + jnp.maximum(i1, i2))
    onehot = lax.broadcasted_iota(jnp.int32, (tm, N_BUCKETS), 1) == bucket
    before = jnp.dot(tri_ref[...], onehot.astype(BF16), preferred_element_type=F32)
    carry = carry_ref[...]
    rank = jnp.sum(jnp.where(onehot, before + carry, 0.0), axis=-1, keepdims=True)
    carry = carry + jnp.sum(onehot.astype(F32), axis=0, keepdims=True)
    carry_ref[...] = carry
    cnt_ref[...] = carry
    lane = lax.broadcasted_iota(jnp.int32, (tm, LANES), 1)
    meta_ref[...] = jnp.where(lane == 0, bucket, jnp.where(lane == 1, rank.astype(jnp.int32), 0))


def _out_proj(x2, conv_n, attn, beta_attn, w_out_bf, g_ffn, w_router_bf, b_router, tm):
    t_tokens = x2.shape[0]
    row = lambda i: (i, 0)
    fixed = lambda i: (0, 0)
    tri = (jnp.arange(tm)[None, :] < jnp.arange(tm)[:, None]).astype(BF16)
    return pl.pallas_call(
        _out_proj_kernel,
        grid=(t_tokens // tm,),
        in_specs=[pl.BlockSpec((tm, D_MODEL), row), pl.BlockSpec((tm, CONV_CH), row),
                  pl.BlockSpec((tm, ATTN_WIDTH), row), pl.BlockSpec((1, ATTN_WIDTH), fixed),
                  pl.BlockSpec((CONV_CH, D_MODEL), fixed), pl.BlockSpec((ATTN_WIDTH, D_MODEL), fixed),
                  pl.BlockSpec((1, D_MODEL), fixed), pl.BlockSpec((D_MODEL, LANES), fixed),
                  pl.BlockSpec((1, LANES), fixed), pl.BlockSpec((tm, tm), fixed)],
        out_specs=[pl.BlockSpec((tm * ROW_CHUNKS, LANES), row), pl.BlockSpec((tm, LANES), row),
                   pl.BlockSpec((1, N_BUCKETS), fixed)],
        out_shape=[jax.ShapeDtypeStruct((t_tokens * ROW_CHUNKS, LANES), F32),
                   jax.ShapeDtypeStruct((t_tokens, LANES), jnp.int32),
                   jax.ShapeDtypeStruct((1, N_BUCKETS), F32)],
        scratch_shapes=[pltpu.VMEM((1, N_BUCKETS), F32)],
        compiler_params=pltpu.CompilerParams(dimension_semantics=("arbitrary",), vmem_limit_bytes=VMEM_LIMIT),
        name="out_proj_router",
    )(x2, conv_n, attn, beta_attn.reshape(1, -1), w_out_bf[:CONV_CH], w_out_bf[CONV_CH:],
      g_ffn.reshape(1, -1), w_router_bf, b_router, tri)


def _positions_kernel(meta_ref, starts_ref, pos_ref):
    meta_t = meta_ref[...].T
    bucket_t = meta_t[0:1, :]
    rank_t = meta_t[1:2, :]
    ids = lax.broadcasted_iota(jnp.int32, (N_BUCKETS, meta_t.shape[1]), 0)
    start_t = jnp.sum(jnp.where(ids == bucket_t, starts_ref[...], 0), axis=0, keepdims=True)
    pos_ref[0] = start_t + rank_t


def _positions(meta, starts, tm):
    t_tokens = meta.shape[0]
    pos = pl.pallas_call(
        _positions_kernel,
        grid=(t_tokens // tm,),
        in_specs=[pl.BlockSpec((tm, LANES), lambda i: (i, 0)), pl.BlockSpec((N_BUCKETS, 1), lambda i: (0, 0))],
        out_specs=pl.BlockSpec((1, 1, tm), lambda i: (i, 0, 0)),
        out_shape=jax.ShapeDtypeStruct((t_tokens // tm, 1, tm), jnp.int32),
        compiler_params=pltpu.CompilerParams(dimension_semantics=("arbitrary",), vmem_limit_bytes=VMEM_LIMIT),
        name="moe_positions",
    )(meta, starts.reshape(N_BUCKETS, 1))
    return pos.reshape(t_tokens)


def _row_tile(ref, r):
    return ref.at[pl.ds(pl.multiple_of(r * ROW_CHUNKS, ROW_CHUNKS), ROW_CHUNKS), :]


def _dispatch_kernel(pos_ref, x_ref, out_hbm, sem):
    tm = x_ref.shape[0] // ROW_CHUNKS
    base = pl.program_id(0) * tm

    def row_copy(r, dst):
        return pltpu.make_async_copy(_row_tile(x_ref, r), _row_tile(out_hbm, dst), sem)

    def issue(r, carry):
        row_copy(r, pos_ref[base + r]).start()
        return carry

    lax.fori_loop(0, tm, issue, 0, unroll=8)

    def drain(r, carry):
        row_copy(0, 0).wait()
        return carry

    lax.fori_loop(0, tm, drain, 0, unroll=8)


def _dispatch(pos, x_rows, tm):
    n_rows = x_rows.shape[0]
    return pl.pallas_call(
        _dispatch_kernel,
        grid_spec=pltpu.PrefetchScalarGridSpec(
            num_scalar_prefetch=1,
            grid=(n_rows // (tm * ROW_CHUNKS),),
            in_specs=[pl.BlockSpec((tm * ROW_CHUNKS, LANES), lambda i, pos: (i, 0))],
            out_specs=pl.BlockSpec(memory_space=pl.ANY),
            scratch_shapes=[pltpu.SemaphoreType.DMA(())],
        ),
        out_shape=jax.ShapeDtypeStruct((n_rows, LANES), F32),
        compiler_params=pltpu.CompilerParams(dimension_semantics=("arbitrary",), vmem_limit_bytes=VMEM_LIMIT,
                                             has_side_effects=True),
        name="moe_dispatch",
    )(pos, x_rows)


def _moe_schedule(counts, n_tiles, tm):
    n_steps = n_tiles + N_PAIR_BUCKETS
    ends = jnp.cumsum(counts)
    starts = ends - counts
    first_tile = starts // tm
    steps_b = jnp.where(counts > 0, (ends - 1) // tm - first_tile + 1, 0)
    step_end = jnp.cumsum(steps_b)
    step_start = step_end - steps_b
    total = step_end[-1]
    s = jnp.minimum(jnp.arange(n_steps, dtype=jnp.int32), total - 1)[:, None]
    mine = (s >= step_start[None, :]) & (s < step_end[None, :])
    pick = lambda v: jnp.sum(jnp.where(mine, v[None, :], 0), axis=1).astype(jnp.int32)
    ids = jnp.arange(N_BUCKETS, dtype=jnp.int32)
    pair = EXPERTS_PER_GROUP * EXPERTS_PER_GROUP
    tile = pick(first_tile - step_start) + s[:, 0]
    return starts, (tile, pick(ids // pair), pick((ids % pair) // EXPERTS_PER_GROUP), pick(ids % EXPERTS_PER_GROUP),
                    pick(starts), pick(ends), total.astype(jnp.int32).reshape(1))


def _moe_kernel(tile_ref, grp_ref, elo_ref, ehi_ref, rlo_ref, rhi_ref, total_ref,
                x_ref, gf_ref, wr_ref, br_ref, wgl_ref, wul_ref, wdl_ref, wgh_ref, wuh_ref, wdh_ref, y_ref):
    s = pl.program_id(0)
    tm = x_ref.shape[0] // ROW_CHUNKS
    tile = tile_ref[s]
    first = jnp.logical_or(s == 0, tile != tile_ref[jnp.maximum(s - 1, 0)])

    @pl.when(s < total_ref[0])
    def _():
        rows = tile * tm + lax.broadcasted_iota(jnp.int32, (tm, 1), 0)
        inside = (rows >= rlo_ref[s]) & (rows < rhi_ref[s])
        h = _rms(_load_row_tiles(x_ref, tm), gf_ref[...]).astype(BF16)
        logits = _router_logits(h, wr_ref, br_ref)
        lane = lax.broadcasted_iota(jnp.int32, (tm, LANES), 1)
        col = N_GROUPS + grp_ref[s] * EXPERTS_PER_GROUP
        lane_val = lambda idx: jnp.sum(jnp.where(lane == idx, logits, 0.0), axis=-1, keepdims=True)
        gl = jnp.where(lane < N_GROUPS, logits, -jnp.inf)
        g_w = 1.0 / jnp.sum(jnp.exp(gl - jnp.max(gl, axis=-1, keepdims=True)), axis=-1, keepdims=True)
        v_lo, v_hi = lane_val(col + elo_ref[s]), lane_val(col + ehi_ref[s])
        v_max = jnp.maximum(v_lo, v_hi)
        p_lo, p_hi = jnp.exp(v_lo - v_max), jnp.exp(v_hi - v_max)
        scale = jnp.where(inside, g_w / (p_lo + p_hi), 0.0)

        def expert(wg_ref, wu_ref, wd_ref, w_row):
            a = jnp.dot(h, wg_ref[0], preferred_element_type=F32)
            b = jnp.dot(h, wu_ref[0], preferred_element_type=F32)
            hid = a * jax.nn.sigmoid(a) * b * w_row
            return jnp.dot(hid.astype(BF16), wd_ref[0], preferred_element_type=F32)

        y = expert(wgl_ref, wul_ref, wdl_ref, p_lo * scale) + expert(wgh_ref, wuh_ref, wdh_ref, p_hi * scale)

        @pl.when(first)
        def _():
            _store_row_tiles(y_ref, y)

        @pl.when(jnp.logical_not(first))
        def _():
            _store_row_tiles(y_ref, y, accumulate=True)


def _moe(schedule, x_sorted, g_ffn, w_router_bf, b_router, wg_bf, wu_bf, wd_bf, tm):
    n_rows = x_sorted.shape[0]
    n_steps = schedule[0].shape[0]
    row = lambda s, tile, *_: (tile[s], 0)
    fixed = lambda s, *_: (0, 0)
    w_lo = lambda s, tile, grp, elo, ehi, *_: (grp[s] * EXPERTS_PER_GROUP + elo[s], 0, 0)
    w_hi = lambda s, tile, grp, elo, ehi, *_: (grp[s] * EXPERTS_PER_GROUP + ehi[s], 0, 0)
    up_spec = lambda sel: pl.BlockSpec((1, D_MODEL, D_EXPERT), sel)
    down_spec = lambda sel: pl.BlockSpec((1, D_EXPERT, D_MODEL), sel)
    return pl.pallas_call(
        _moe_kernel,
        grid_spec=pltpu.PrefetchScalarGridSpec(
            num_scalar_prefetch=7,
            grid=(n_steps,),
            in_specs=[pl.BlockSpec((tm * ROW_CHUNKS, LANES), row), pl.BlockSpec((1, D_MODEL), fixed),
                      pl.BlockSpec((D_MODEL, LANES), fixed), pl.BlockSpec((1, LANES), fixed),
                      up_spec(w_lo), up_spec(w_lo), down_spec(w_lo), up_spec(w_hi), up_spec(w_hi), down_spec(w_hi)],
            out_specs=pl.BlockSpec((tm * ROW_CHUNKS, LANES), row),
        ),
        out_shape=jax.ShapeDtypeStruct((n_rows, LANES), F32),
        compiler_params=pltpu.CompilerParams(dimension_semantics=("arbitrary",), vmem_limit_bytes=VMEM_LIMIT),
        name="hier_moe",
    )(*schedule, x_sorted, g_ffn.reshape(1, -1), w_router_bf, b_router, wg_bf, wu_bf, wd_bf, wg_bf, wu_bf, wd_bf)


def _ple_final_kernel(pos_ref, x1_ref, ys_hbm, p_ref, gp_ref, wg_ref, bg_ref, wp_ref, gfin_ref, o_ref,
                      ybuf, sems):
    i = pl.program_id(0)
    n = pl.num_programs(0)
    tm = o_ref.shape[0]

    def row_copy(src, r, slot):
        return pltpu.make_async_copy(_row_tile(ys_hbm, src), _row_tile(ybuf.at[slot], r), sems.at[slot])

    def gather(tile, slot):
        def issue(r, carry):
            row_copy(pos_ref[tile * tm + r], r, slot).start()
            return carry
        lax.fori_loop(0, tm, issue, 0, unroll=8)

    @pl.when(i == 0)
    def _():
        gather(0, 0)

    @pl.when(i + 1 < n)
    def _():
        gather(i + 1, (i + 1) % 2)

    slot = i % 2

    def drain(r, carry):
        row_copy(0, r, slot).wait()
        return carry

    lax.fori_loop(0, tm, drain, 0, unroll=8)
    x2 = _load_row_tiles(x1_ref, tm) + _load_row_tiles(ybuf.at[slot], tm)
    hg = _rms(x2, gp_ref[...]).astype(BF16)
    gate = jax.nn.sigmoid(jnp.dot(hg, wg_ref[...], preferred_element_type=F32) + bg_ref[...])
    emb = jnp.dot(p_ref[...].astype(BF16), wp_ref[...], preferred_element_type=F32)
    x3 = x2 + emb * gate
    o_ref[...] = _rms(x3, gfin_ref[...])


def _ple_final(pos, x1_rows, y_sorted, p2, g_ple, w_gate_bf, b_gate, w_proj_bf, g_final, tm):
    t_tokens = p2.shape[0]
    row = lambda i, pos: (i, 0)
    fixed = lambda i, pos: (0, 0)
    vec = pl.BlockSpec((1, D_MODEL), fixed)
    return pl.pallas_call(
        _ple_final_kernel,
        grid_spec=pltpu.PrefetchScalarGridSpec(
            num_scalar_prefetch=1,
            grid=(t_tokens // tm,),
            in_specs=[pl.BlockSpec((tm * ROW_CHUNKS, LANES), row), pl.BlockSpec(memory_space=pl.ANY),
                      pl.BlockSpec((tm, PLE_DIM), row), vec, pl.BlockSpec((D_MODEL, D_MODEL), fixed), vec,
                      pl.BlockSpec((PLE_DIM, D_MODEL), fixed), vec],
            out_specs=pl.BlockSpec((tm, D_MODEL), row),
            scratch_shapes=[pltpu.VMEM((2, tm * ROW_CHUNKS, LANES), F32), pltpu.SemaphoreType.DMA((2,))],
        ),
        out_shape=jax.ShapeDtypeStruct((t_tokens, D_MODEL), F32),
        compiler_params=pltpu.CompilerParams(dimension_semantics=("arbitrary",), vmem_limit_bytes=VMEM_LIMIT),
        name="ple_final",
    )(pos, x1_rows, y_sorted, p2, g_ple.reshape(1, -1), w_gate_bf, b_gate.reshape(1, -1), w_proj_bf,
      g_final.reshape(1, -1))


def kernel(x, p, g_mix, w_in, w_dw, b_dw, conv_ln_g, conv_ln_b, beta_conv, beta_attn, w_out, g_ffn, w_group, b_group, w_erouter, b_erouter, w_gate, w_up, w_down, g_ple, w_ple_gate, b_ple_gate, w_ple_proj, g_final):
    batch, seq, _ = x.shape
    assert p.shape[0] == 1 and seq % MOBA_BLOCK == 0
    t_tokens = batch * seq
    x2 = x.reshape(t_tokens, D_MODEL)
    i = 0
    u, qp, kp, vp = _in_proj(x2, g_mix[i], w_in[i].astype(BF16), seq, tm=512)
    conv_n = _conv(u, w_dw[i], b_dw[i], conv_ln_g[i], conv_ln_b[i], beta_conv[i], batch, seq, ts=256)
    attn = _moba(qp, kp, vp, batch, seq)
    w_router = jnp.concatenate(
        [w_group[i], jnp.transpose(w_erouter[i], (1, 0, 2)).reshape(D_MODEL, N_EXPERTS),
         jnp.zeros((D_MODEL, LANES - N_GROUPS - N_EXPERTS), F32)], axis=1).astype(BF16)
    b_router = jnp.concatenate([b_group[i], b_erouter[i].reshape(-1),
                                jnp.zeros((LANES - N_GROUPS - N_EXPERTS,), F32)]).reshape(1, LANES)
    x1_rows, meta, counts = _out_proj(x2, conv_n, attn, beta_attn[i], w_out[i].astype(BF16), g_ffn[i], w_router,
                                      b_router, tm=512)
    moe_tm = 256
    starts, schedule = _moe_schedule(counts[0].astype(jnp.int32), t_tokens // moe_tm, moe_tm)
    pos = _positions(meta, starts, tm=2048)
    x1_sorted = _dispatch(pos, x1_rows, tm=1024)
    y_sorted = _moe(schedule, x1_sorted, g_ffn[i], w_router, b_router, w_gate[i].astype(BF16),
                    w_up[i].astype(BF16), w_down[i].astype(BF16), moe_tm)
    out = _ple_final(pos, x1_rows, y_sorted, p[i].reshape(t_tokens, PLE_DIM), g_ple[i],
                     w_ple_gate[i].astype(BF16), b_ple_gate[i], w_ple_proj[i].astype(BF16), g_final, tm=512)
    return out.reshape(batch, seq, D_MODEL)
```

```python
import jax
import jax.numpy as jnp
from jax import lax
from jax.experimental import pallas as pl
from jax.experimental.pallas import tpu as pltpu

F32 = jnp.float32
BF16 = jnp.bfloat16

D_MODEL = 1024
CONV_CH = 512
CONV_WIDTH = 31
ATTN_WIDTH = 512
HEAD_DIM = 64
N_HEADS = 8
ROT_DIM = 16
ROPE_THETA = 500000.0
MOBA_BLOCK = 256
MOBA_TOPK = 3
N_GROUPS = 4
EXPERTS_PER_GROUP = 8
N_EXPERTS = 32
D_EXPERT = 256
PLE_DIM = 256
N_BUCKETS = N_GROUPS * EXPERTS_PER_GROUP * EXPERTS_PER_GROUP
N_PAIR_BUCKETS = N_GROUPS * EXPERTS_PER_GROUP * (EXPERTS_PER_GROUP - 1) // 2
IN_COLS = 2 * CONV_CH + 3 * ATTN_WIDTH
EPS = 1e-6

LANES = 128
SUBLANES = 8
HEAD_PAD = LANES
QKV_PAD = N_HEADS * HEAD_PAD
ROW_CHUNKS = D_MODEL // LANES
assert ROW_CHUNKS == SUBLANES
MASK_NEG = -1e30
Q_SCALE = HEAD_DIM ** -0.5 * 1.4426950408889634
CONV_HALO = 32
CONV_ROWS = 32
DMA_GROUP = 8
ROUTER_EXPERT_COL = SUBLANES
VMEM_LIMIT = 56 * 1024 * 1024


def _rms(x, g):
    return x * lax.rsqrt(jnp.mean(x * x, axis=-1, keepdims=True) + EPS) * g


def _load_row_tiles(ref, n_rows):
    return jnp.concatenate([ref[pl.ds(c, n_rows, stride=ROW_CHUNKS), :] for c in range(ROW_CHUNKS)], axis=1)


def _store_row_tiles(ref, val, accumulate=False):
    n_rows = val.shape[0]
    for c in range(ROW_CHUNKS):
        idx = (pl.ds(c, n_rows, stride=ROW_CHUNKS), slice(None))
        piece = val[:, c * LANES:(c + 1) * LANES]
        ref[idx] = ref[idx] + piece if accumulate else piece


def _router_logits(h_bf16, wr_ref, br_ref):
    return jnp.dot(h_bf16, wr_ref[...], preferred_element_type=F32) + br_ref[...]


def _in_proj_kernel(x_ref, g_ref, w_ref, rc_ref, ra_ref, rb_ref, u_ref, q_ref, k_ref, v_ref):
    h = _rms(x_ref[...], g_ref[...]).astype(BF16)
    proj = jnp.dot(h, w_ref[...], preferred_element_type=F32)
    u_ref[...] = proj[:, :CONV_CH] * jax.nn.sigmoid(proj[:, CONV_CH:2 * CONV_CH])
    rc, ra, rb = rc_ref[...], ra_ref[...], rb_ref[...]
    low = lax.broadcasted_iota(jnp.int32, (x_ref.shape[0], LANES), 1) < HEAD_DIM

    def put(dst_ref, base, rope, scale, spare):
        for c in range(ATTN_WIDTH // LANES):
            t = proj[:, base + c * LANES: base + (c + 1) * LANES]
            if rope:
                t = t * rc + pltpu.roll(t, LANES - ROT_DIM // 2, 1) * ra + pltpu.roll(t, ROT_DIM // 2, 1) * rb
            if scale != 1.0:
                t = t * scale
            even = jnp.where(low, t, spare)
            odd = jnp.where(low, pltpu.roll(t, HEAD_DIM, 1), spare)
            dst_ref[:, (2 * c) * HEAD_PAD:(2 * c + 1) * HEAD_PAD] = even.astype(dst_ref.dtype)
            dst_ref[:, (2 * c + 1) * HEAD_PAD:(2 * c + 2) * HEAD_PAD] = odd.astype(dst_ref.dtype)

    ones_lane = jnp.where(lax.broadcasted_iota(jnp.int32, (x_ref.shape[0], LANES), 1) == HEAD_DIM, 1.0, 0.0)
    put(q_ref, 2 * CONV_CH, True, Q_SCALE, 0.0)
    put(k_ref, 2 * CONV_CH + ATTN_WIDTH, True, 1.0, 0.0)
    put(v_ref, 2 * CONV_CH + 2 * ATTN_WIDTH, False, 1.0, ones_lane)


def _rope_coeff_tables(seq):
    pos = jnp.arange(seq, dtype=F32)
    inv_freq = jnp.power(jnp.float32(ROPE_THETA), -jnp.arange(0, ROT_DIM, 2, dtype=F32) / ROT_DIM)
    ang = pos[:, None] * inv_freq[None, :]
    cos, sin = jnp.cos(ang), jnp.sin(ang)
    zeros = jnp.zeros((seq, HEAD_DIM - ROT_DIM), F32)
    c_head = jnp.concatenate([cos, cos, jnp.ones_like(zeros)], axis=1)
    a_head = jnp.concatenate([-sin, jnp.zeros_like(sin), zeros], axis=1)
    b_head = jnp.concatenate([jnp.zeros_like(sin), sin, zeros], axis=1)
    rep = LANES // HEAD_DIM
    return jnp.tile(c_head, (1, rep)), jnp.tile(a_head, (1, rep)), jnp.tile(b_head, (1, rep))


def _in_proj(x2, g_mix, w_in_bf, seq, tm):
    t_tokens = x2.shape[0]
    rc, ra, rb = _rope_coeff_tables(seq)
    n_seq_tiles = seq // tm
    row = lambda i: (i, 0)
    fixed = lambda i: (0, 0)
    tab = lambda i: (i % n_seq_tiles, 0)
    return pl.pallas_call(
        _in_proj_kernel,
        grid=(t_tokens // tm,),
        in_specs=[pl.BlockSpec((tm, D_MODEL), row), pl.BlockSpec((1, D_MODEL), fixed),
                  pl.BlockSpec((D_MODEL, IN_COLS), fixed),
                  pl.BlockSpec((tm, LANES), tab), pl.BlockSpec((tm, LANES), tab), pl.BlockSpec((tm, LANES), tab)],
        out_specs=[pl.BlockSpec((tm, CONV_CH), row), pl.BlockSpec((tm, QKV_PAD), row),
                   pl.BlockSpec((tm, QKV_PAD), row), pl.BlockSpec((tm, QKV_PAD), row)],
        out_shape=[jax.ShapeDtypeStruct((t_tokens, CONV_CH), F32),
                   jax.ShapeDtypeStruct((t_tokens, QKV_PAD), BF16),
                   jax.ShapeDtypeStruct((t_tokens, QKV_PAD), BF16),
                   jax.ShapeDtypeStruct((t_tokens, QKV_PAD), BF16)],
        compiler_params=pltpu.CompilerParams(dimension_semantics=("arbitrary",), vmem_limit_bytes=VMEM_LIMIT),
        name="in_proj",
    )(x2, g_mix.reshape(1, D_MODEL), w_in_bf, rc, ra, rb)


def _conv_kernel(u_ref, w_ref, b_ref, lg_ref, lb_ref, beta_ref, o_ref, pad_ref, shift_ref):
    ts = u_ref.shape[0]

    @pl.when(pl.program_id(1) == 0)
    def _():
        pad_ref[0:CONV_HALO, :] = jnp.zeros((CONV_HALO, CONV_CH), F32)

    pad_ref[CONV_HALO:CONV_HALO + ts, :] = u_ref[...]
    kept = ts + CONV_HALO - SUBLANES
    for sh in range(1, SUBLANES):
        shift_ref[sh - 1, 0:kept, :] = pad_ref[sh:sh + kept, :]
    w = w_ref[...]
    first = CONV_HALO - (CONV_WIDTH - 1)

    def window(start):
        sh, base = start % SUBLANES, start - start % SUBLANES
        if sh == 0:
            return pad_ref[base:base + CONV_ROWS, :]
        return shift_ref[sh - 1, base:base + CONV_ROWS, :]

    for r0 in range(0, ts, CONV_ROWS):
        acc = jnp.zeros((CONV_ROWS, CONV_CH), F32)
        for tap in range(CONV_WIDTH):
            acc = acc + window(r0 + first + tap) * w[tap:tap + 1, :]
        y = acc + b_ref[...]
        mu = jnp.mean(y, axis=-1, keepdims=True)
        d = y - mu
        var = jnp.mean(d * d, axis=-1, keepdims=True)
        y = d * lax.rsqrt(var + EPS) * lg_ref[...] + lb_ref[...]
        y = y * jax.nn.sigmoid(y)
        o_ref[r0:r0 + CONV_ROWS, :] = _rms(y, beta_ref[...]).astype(o_ref.dtype)
    pad_ref[0:CONV_HALO, :] = pad_ref[ts:ts + CONV_HALO, :]


def _conv(u2, w_dw, b_dw, ln_g, ln_b, beta, batch, seq, ts):
    n_s = seq // ts
    row = lambda b, s: (b * n_s + s, 0)
    fixed = lambda b, s: (0, 0)
    vec = pl.BlockSpec((1, CONV_CH), fixed)
    return pl.pallas_call(
        _conv_kernel,
        grid=(batch, n_s),
        in_specs=[pl.BlockSpec((ts, CONV_CH), row), pl.BlockSpec((CONV_WIDTH, CONV_CH), fixed), vec, vec, vec, vec],
        out_specs=pl.BlockSpec((ts, CONV_CH), row),
        out_shape=jax.ShapeDtypeStruct((batch * seq, CONV_CH), BF16),
        scratch_shapes=[pltpu.VMEM((ts + CONV_HALO, CONV_CH), F32),
                        pltpu.VMEM((SUBLANES - 1, ts + CONV_HALO, CONV_CH), F32)],
        compiler_params=pltpu.CompilerParams(dimension_semantics=("arbitrary", "arbitrary"),
                                             vmem_limit_bytes=VMEM_LIMIT),
        name="conformer_conv",
    )(u2, w_dw, b_dw.reshape(1, -1), ln_g.reshape(1, -1), ln_b.reshape(1, -1), beta.reshape(1, -1))


def _moba_kernel(q_ref, k_ref, v_ref, blk_ref, place_ref, o_ref):
    seq = q_ref.shape[0]
    nb = seq // MOBA_BLOCK
    blk_onehot = blk_ref[...]
    place = place_ref[...]
    sub = lax.broadcasted_iota(jnp.int32, (nb, seq), 0)
    q_blk = lax.broadcasted_iota(jnp.int32, (nb, seq), 1) // MOBA_BLOCK
    past = sub < q_blk
    r_io = lax.broadcasted_iota(jnp.int32, (MOBA_BLOCK, MOBA_BLOCK), 0)
    c_io = lax.broadcasted_iota(jnp.int32, (MOBA_BLOCK, MOBA_BLOCK), 1)
    causal = c_io <= r_io
    low = lax.broadcasted_iota(jnp.int32, (MOBA_BLOCK, LANES), 1) < HEAD_DIM
    contract_lanes = (((1,), (1,)), ((), ()))
    q_aug, k_aug = [], []
    for hh in range(2):
        lo, hi = hh * HEAD_PAD, (hh + 1) * HEAD_PAD
        k = k_ref[:, lo:hi]
        q = q_ref[:, lo:hi]
        kf = k.astype(F32)
        k_mean = jnp.concatenate(
            [jnp.mean(kf[j * MOBA_BLOCK:(j + 1) * MOBA_BLOCK], axis=0, keepdims=True) for j in range(nb)], axis=0)
        km_hi = k_mean.astype(BF16)
        km_lo = (k_mean - km_hi.astype(F32)).astype(BF16)
        gate = (lax.dot_general(km_hi, q, contract_lanes, preferred_element_type=F32)
                + lax.dot_general(km_lo, q, contract_lanes, preferred_element_type=F32))
        bias_t = jnp.where(sub == q_blk, 0.0, MASK_NEG)
        for j in range(nb - 1):
            gj = gate[j:j + 1, :]
            beats = ((gate > gj) | ((gate == gj) & (sub < j))) & past
            cnt = jnp.sum(beats.astype(F32), axis=0, keepdims=True)
            bias_t = jnp.where((sub == j) & past & (cnt < float(MOBA_TOPK)), 0.0, bias_t)
        bias = jnp.dot(bias_t.T, place, preferred_element_type=F32)
        q_aug.append(q + bias.astype(BF16))
        k_aug.append(k + blk_onehot)

    def scores(hh, i):
        return lax.dot_general(q_aug[hh][i * MOBA_BLOCK:(i + 1) * MOBA_BLOCK], k_aug[hh][:(i + 1) * MOBA_BLOCK],
                               contract_lanes, preferred_element_type=F32)

    def attend(s, hh, i):
        v_lo = hh * HEAD_PAD
        s_own = jnp.where(causal, s[:, i * MOBA_BLOCK:], MASK_NEG)
        m = jnp.max(s_own, axis=-1, keepdims=True)
        if i > 0:
            s_past = s[:, :i * MOBA_BLOCK]
            m = jnp.maximum(m, jnp.max(s_past, axis=-1, keepdims=True))
        acc = jnp.dot(jnp.exp2((s_own - m).astype(BF16)),
                      v_ref[i * MOBA_BLOCK:(i + 1) * MOBA_BLOCK, v_lo:v_lo + HEAD_PAD], preferred_element_type=F32)
        if i > 0:
            acc = acc + jnp.dot(jnp.exp2((s_past - m).astype(BF16)), v_ref[:i * MOBA_BLOCK, v_lo:v_lo + HEAD_PAD],
                                preferred_element_type=F32)
        return jnp.where(low, acc / acc[:, HEAD_DIM:HEAD_DIM + 1], 0.0)

    items = [(hh, i) for i in range(nb) for hh in range(2)]
    s_cur = scores(*items[0])
    even = None
    for n, (hh, i) in enumerate(items):
        s_next = scores(*items[n + 1]) if n + 1 < len(items) else None
        out = attend(s_cur, hh, i)
        if hh == 0:
            even = out
        else:
            o_ref[i * MOBA_BLOCK:(i + 1) * MOBA_BLOCK, :] = even + pltpu.roll(out, HEAD_DIM, 1)
        s_cur = s_next


def _moba(qp, kp, vp, batch, seq):
    nb = seq // MOBA_BLOCK
    lane = jnp.arange(LANES)[None, :]
    blk = (jnp.arange(seq) // MOBA_BLOCK)[:, None]
    blk_onehot = (lane == HEAD_DIM + blk).astype(BF16)
    place = (lane == HEAD_DIM + jnp.arange(nb)[:, None]).astype(F32)
    pair = lambda b, c: (b, c)
    fixed = lambda b, c: (0, 0)
    spec = pl.BlockSpec((seq, 2 * HEAD_PAD), pair)
    return pl.pallas_call(
        _moba_kernel,
        grid=(batch, N_HEADS // 2),
        in_specs=[spec, spec, spec, pl.BlockSpec((seq, LANES), fixed), pl.BlockSpec((nb, LANES), fixed)],
        out_specs=pl.BlockSpec((seq, LANES), pair),
        out_shape=jax.ShapeDtypeStruct((batch * seq, ATTN_WIDTH), F32),
        compiler_params=pltpu.CompilerParams(dimension_semantics=("arbitrary", "arbitrary"),
                                             vmem_limit_bytes=VMEM_LIMIT),
        name="moba_attention",
    )(qp, kp, vp, blk_onehot, place)


def _out_proj_kernel(x_ref, c_ref, a_ref, beta_ref, wc_ref, wa_ref, gf_ref, wr_ref, br_ref, tri_ref,
                     x1_ref, bucket_ref, rank_ref, cnt_ref, carry_ref):
    @pl.when(pl.program_id(0) == 0)
    def _():
        carry_ref[...] = jnp.zeros_like(carry_ref)

    an = _rms(a_ref[...], beta_ref[...]).astype(BF16)
    x1 = (x_ref[...] + jnp.dot(c_ref[...], wc_ref[...], preferred_element_type=F32)
          + jnp.dot(an, wa_ref[...], preferred_element_type=F32))
    _store_row_tiles(x1_ref, x1)
    logits = _router_logits(_rms(x1, gf_ref[...]).astype(BF16), wr_ref, br_ref)
    tm = logits.shape[0]
    lt = logits.T
    sub = lax.broadcasted_iota(jnp.int32, (SUBLANES, tm), 0)
    gl = jnp.where(sub < N_GROUPS, lt[0:SUBLANES, :], -jnp.inf)
    g_idx = jnp.min(jnp.where(gl == jnp.max(gl, axis=0, keepdims=True), sub, N_GROUPS), axis=0, keepdims=True)
    el = jnp.zeros((EXPERTS_PER_GROUP, tm), F32)
    for g in range(N_GROUPS):
        lo = ROUTER_EXPERT_COL + g * EXPERTS_PER_GROUP
        el = jnp.where(g_idx == g, lt[lo:lo + EXPERTS_PER_GROUP, :], el)
    i1 = jnp.min(jnp.where(el == jnp.max(el, axis=0, keepdims=True), sub, EXPERTS_PER_GROUP), axis=0, keepdims=True)
    el2 = jnp.where(sub == i1, -jnp.inf, el)
    i2 = jnp.min(jnp.where(el2 == jnp.max(el2, axis=0, keepdims=True), sub, EXPERTS_PER_GROUP), axis=0,
                 keepdims=True)
    bucket = (g_idx * (EXPERTS_PER_GROUP * EXPERTS_PER_GROUP) + jnp.minimum(i1, i2) * EXPERTS_PER_GROUP
              + jnp.maximum(i1, i2))
    onehot = (lax.broadcasted_iota(jnp.int32, (N_BUCKETS, tm), 0) == bucket)
    onehot_bf = onehot.astype(BF16)
    before = jnp.dot(onehot_bf, tri_ref[...], preferred_element_type=F32)
    counts = jnp.dot(onehot_bf, jnp.ones((tm, LANES), BF16), preferred_element_type=F32)
    carry = carry_ref[...]
    carry_wide = jnp.concatenate([carry] * (tm // LANES), axis=1)
    rank = jnp.sum(jnp.where(onehot, before + carry_wide, 0.0), axis=0, keepdims=True)
    carry_ref[...] = carry + counts
    cnt_ref[...] = carry + counts
    bucket_ref[0] = bucket
    rank_ref[0] = rank.astype(jnp.int32)


def _out_proj(x2, conv_n, attn, beta_attn, w_out_bf, g_ffn, w_router_bf, b_router, tm):
    t_tokens = x2.shape[0]
    row = lambda i: (i, 0)
    fixed = lambda i: (0, 0)
    tri = (jnp.arange(tm)[:, None] < jnp.arange(tm)[None, :]).astype(BF16)
    lane_row = pl.BlockSpec((1, 1, tm), lambda i: (i, 0, 0))
    return pl.pallas_call(
        _out_proj_kernel,
        grid=(t_tokens // tm,),
        in_specs=[pl.BlockSpec((tm, D_MODEL), row), pl.BlockSpec((tm, CONV_CH), row),
                  pl.BlockSpec((tm, ATTN_WIDTH), row), pl.BlockSpec((1, ATTN_WIDTH), fixed),
                  pl.BlockSpec((CONV_CH, D_MODEL), fixed), pl.BlockSpec((ATTN_WIDTH, D_MODEL), fixed),
                  pl.BlockSpec((1, D_MODEL), fixed), pl.BlockSpec((D_MODEL, LANES), fixed),
                  pl.BlockSpec((1, LANES), fixed), pl.BlockSpec((tm, tm), fixed)],
        out_specs=[pl.BlockSpec((tm * ROW_CHUNKS, LANES), row), lane_row, lane_row,
                   pl.BlockSpec((N_BUCKETS, LANES), fixed)],
        out_shape=[jax.ShapeDtypeStruct((t_tokens * ROW_CHUNKS, LANES), F32),
                   jax.ShapeDtypeStruct((t_tokens // tm, 1, tm), jnp.int32),
                   jax.ShapeDtypeStruct((t_tokens // tm, 1, tm), jnp.int32),
                   jax.ShapeDtypeStruct((N_BUCKETS, LANES), F32)],
        scratch_shapes=[pltpu.VMEM((N_BUCKETS, LANES), F32)],
        compiler_params=pltpu.CompilerParams(dimension_semantics=("arbitrary",), vmem_limit_bytes=VMEM_LIMIT),
        name="out_proj_router",
    )(x2, conv_n, attn, beta_attn.reshape(1, -1), w_out_bf[:CONV_CH], w_out_bf[CONV_CH:],
      g_ffn.reshape(1, -1), w_router_bf, b_router, tri)


def _positions_kernel(bucket_ref, rank_ref, starts_ref, pos_ref):
    bucket = bucket_ref[0]
    ids = lax.broadcasted_iota(jnp.int32, (N_BUCKETS, bucket.shape[1]), 0)
    pos_ref[0] = jnp.sum(jnp.where(ids == bucket, starts_ref[...], 0), axis=0, keepdims=True) + rank_ref[0]


def _positions(bucket, rank, starts, tm):
    t_tokens = bucket.size
    lane_row = pl.BlockSpec((1, 1, tm), lambda i: (i, 0, 0))
    shape = (t_tokens // tm, 1, tm)
    pos = pl.pallas_call(
        _positions_kernel,
        grid=(t_tokens // tm,),
        in_specs=[lane_row, lane_row, pl.BlockSpec((N_BUCKETS, 1), lambda i: (0, 0))],
        out_specs=lane_row,
        out_shape=jax.ShapeDtypeStruct(shape, jnp.int32),
        compiler_params=pltpu.CompilerParams(dimension_semantics=("arbitrary",), vmem_limit_bytes=VMEM_LIMIT),
        name="moe_positions",
    )(bucket.reshape(shape), rank.reshape(shape), starts.reshape(N_BUCKETS, 1))
    return pos.reshape(t_tokens)


def _row_tile(ref, r):
    return ref.at[pl.ds(pl.multiple_of(r * ROW_CHUNKS, ROW_CHUNKS), ROW_CHUNKS), :]


def _dispatch_kernel(pos_ref, x_ref, out_hbm, sem):
    tm = x_ref.shape[0] // ROW_CHUNKS
    base = pl.program_id(0) * tm

    def row_copy(r, dst):
        return pltpu.make_async_copy(_row_tile(x_ref, r), _row_tile(out_hbm, dst), sem)

    def issue(g, carry):
        for u in range(DMA_GROUP):
            r = g * DMA_GROUP + u
            row_copy(r, pos_ref[base + r]).start(priority=u % 2)
        return carry

    lax.fori_loop(0, tm // DMA_GROUP, issue, 0)

    def drain(r, carry):
        row_copy(0, 0).wait()
        return carry

    lax.fori_loop(0, tm, drain, 0, unroll=8)


def _dispatch(pos, x_rows, tm):
    n_rows = x_rows.shape[0]
    return pl.pallas_call(
        _dispatch_kernel,
        grid_spec=pltpu.PrefetchScalarGridSpec(
            num_scalar_prefetch=1,
            grid=(n_rows // (tm * ROW_CHUNKS),),
            in_specs=[pl.BlockSpec((tm * ROW_CHUNKS, LANES), lambda i, pos: (i, 0))],
            out_specs=pl.BlockSpec(memory_space=pl.ANY),
            scratch_shapes=[pltpu.SemaphoreType.DMA(())],
        ),
        out_shape=jax.ShapeDtypeStruct((n_rows, LANES), F32),
        compiler_params=pltpu.CompilerParams(dimension_semantics=("arbitrary",), vmem_limit_bytes=VMEM_LIMIT,
                                             has_side_effects=True),
        name="moe_dispatch",
    )(pos, x_rows)


def _moe_schedule(counts, n_tiles, tm):
    n_steps = n_tiles + N_PAIR_BUCKETS
    ends = jnp.cumsum(counts)
    starts = ends - counts
    first_tile = starts // tm
    steps_b = jnp.where(counts > 0, (ends - 1) // tm - first_tile + 1, 0)
    step_end = jnp.cumsum(steps_b)
    step_start = step_end - steps_b
    total = step_end[-1]
    s = jnp.minimum(jnp.arange(n_steps, dtype=jnp.int32), total - 1)[:, None]
    mine = (s >= step_start[None, :]) & (s < step_end[None, :])
    pick = lambda v: jnp.sum(jnp.where(mine, v[None, :], 0), axis=1).astype(jnp.int32)
    ids = jnp.arange(N_BUCKETS, dtype=jnp.int32)
    pair = EXPERTS_PER_GROUP * EXPERTS_PER_GROUP
    tile = pick(first_tile - step_start) + s[:, 0]
    return starts, (tile, pick(ids // pair), pick((ids % pair) // EXPERTS_PER_GROUP), pick(ids % EXPERTS_PER_GROUP),
                    pick(starts), pick(ends), total.astype(jnp.int32).reshape(1))


def _moe_kernel(tile_ref, grp_ref, elo_ref, ehi_ref, rlo_ref, rhi_ref, total_ref,
                x_ref, gf_ref, wr_ref, br_ref, wgl_ref, wul_ref, wdl_ref, wgh_ref, wuh_ref, wdh_ref, y_ref, acc_ref):
    s = pl.program_id(0)
    tm = x_ref.shape[0] // ROW_CHUNKS
    tile = tile_ref[s]
    total = total_ref[0]
    first = jnp.logical_or(s == 0, tile != tile_ref[jnp.maximum(s - 1, 0)])
    last = jnp.logical_or(s == total - 1, tile != tile_ref[jnp.minimum(s + 1, pl.num_programs(0) - 1)])

    @pl.when(s < total)
    def _():
        rows = tile * tm + lax.broadcasted_iota(jnp.int32, (tm, 1), 0)
        inside = (rows >= rlo_ref[s]) & (rows < rhi_ref[s])
        x = _load_row_tiles(x_ref, tm)
        hg = (x * gf_ref[...]).astype(BF16)
        inv_rms = lax.rsqrt(jnp.mean(x * x, axis=-1, keepdims=True) + EPS)
        logits = inv_rms * jnp.dot(hg, wr_ref[...], preferred_element_type=F32) + br_ref[...]
        lane = lax.broadcasted_iota(jnp.int32, (tm, LANES), 1)
        col = ROUTER_EXPERT_COL + grp_ref[s] * EXPERTS_PER_GROUP
        lane_val = lambda idx: jnp.sum(jnp.where(lane == idx, logits, 0.0), axis=-1, keepdims=True)
        gl = jnp.where(lane < N_GROUPS, logits, -jnp.inf)
        g_w = 1.0 / jnp.sum(jnp.exp(gl - jnp.max(gl, axis=-1, keepdims=True)), axis=-1, keepdims=True)
        v_lo, v_hi = lane_val(col + elo_ref[s]), lane_val(col + ehi_ref[s])
        v_max = jnp.maximum(v_lo, v_hi)
        p_lo, p_hi = jnp.exp(v_lo - v_max), jnp.exp(v_hi - v_max)
        scale = jnp.where(inside, inv_rms * g_w / (p_lo + p_hi), 0.0)

        def expert(wg_ref, wu_ref, wd_ref, w_row):
            a = inv_rms * jnp.dot(hg, wg_ref[0], preferred_element_type=F32)
            b = jnp.dot(hg, wu_ref[0], preferred_element_type=F32)
            hid = a * jax.nn.sigmoid(a) * b * w_row
            return jnp.dot(hid.astype(BF16), wd_ref[0], preferred_element_type=F32)

        y = expert(wgl_ref, wul_ref, wdl_ref, p_lo * scale) + expert(wgh_ref, wuh_ref, wdh_ref, p_hi * scale)

        @pl.when(first)
        def _():
            acc_ref[...] = y

        @pl.when(jnp.logical_not(first))
        def _():
            acc_ref[...] += y

        @pl.when(last)
        def _():
            _store_row_tiles(y_ref, acc_ref[...])


def _moe(schedule, x_sorted, g_ffn, w_router_bf, b_router, wg_bf, wu_bf, wd_bf, tm):
    n_rows = x_sorted.shape[0]
    n_steps = schedule[0].shape[0]
    row = lambda s, tile, *_: (tile[s], 0)
    fixed = lambda s, *_: (0, 0)
    w_lo = lambda s, tile, grp, elo, ehi, *_: (grp[s] * EXPERTS_PER_GROUP + elo[s], 0, 0)
    w_hi = lambda s, tile, grp, elo, ehi, *_: (grp[s] * EXPERTS_PER_GROUP + ehi[s], 0, 0)
    up_spec = lambda sel: pl.BlockSpec((1, D_MODEL, D_EXPERT), sel)
    down_spec = lambda sel: pl.BlockSpec((1, D_EXPERT, D_MODEL), sel)
    return pl.pallas_call(
        _moe_kernel,
        grid_spec=pltpu.PrefetchScalarGridSpec(
            num_scalar_prefetch=7,
            grid=(n_steps,),
            in_specs=[pl.BlockSpec((tm * ROW_CHUNKS, LANES), row), pl.BlockSpec((1, D_MODEL), fixed),
                      pl.BlockSpec((D_MODEL, LANES), fixed), pl.BlockSpec((1, LANES), fixed),
                      up_spec(w_lo), up_spec(w_lo), down_spec(w_lo), up_spec(w_hi), up_spec(w_hi), down_spec(w_hi)],
            out_specs=pl.BlockSpec((tm * ROW_CHUNKS, LANES), row),
            scratch_shapes=[pltpu.VMEM((tm, D_MODEL), F32)],
        ),
        out_shape=jax.ShapeDtypeStruct((n_rows, LANES), F32),
        compiler_params=pltpu.CompilerParams(dimension_semantics=("arbitrary",), vmem_limit_bytes=VMEM_LIMIT),
        name="hier_moe",
    )(*schedule, x_sorted, g_ffn.reshape(1, -1), w_router_bf, b_router, wg_bf, wu_bf, wd_bf, wg_bf, wu_bf, wd_bf)


def _ple_final_kernel(pos_ref, x1_ref, ys_hbm, p_ref, gp_ref, wg_ref, bg_ref, wp_ref, gfin_ref, o_ref,
                      ybuf, sems):
    i = pl.program_id(0)
    n = pl.num_programs(0)
    tm = o_ref.shape[0]

    def row_copy(src, r, slot):
        return pltpu.make_async_copy(_row_tile(ys_hbm, src), _row_tile(ybuf.at[slot], r), sems.at[slot])

    def gather(tile, slot):
        def issue(g, carry):
            for u in range(DMA_GROUP):
                r = g * DMA_GROUP + u
                row_copy(pos_ref[tile * tm + r], r, slot).start(priority=u % 2)
            return carry
        lax.fori_loop(0, tm // DMA_GROUP, issue, 0)

    @pl.when(i == 0)
    def _():
        gather(0, 0)

    @pl.when(i + 1 < n)
    def _():
        gather(i + 1, (i + 1) % 2)

    slot = i % 2

    def drain(r, carry):
        row_copy(0, r, slot).wait()
        return carry

    lax.fori_loop(0, tm, drain, 0, unroll=8)
    x2 = _load_row_tiles(x1_ref, tm) + _load_row_tiles(ybuf.at[slot], tm)
    hg = _rms(x2, gp_ref[...]).astype(BF16)
    gate = jax.nn.sigmoid(jnp.dot(hg, wg_ref[...], preferred_element_type=F32) + bg_ref[...])
    emb = jnp.dot(p_ref[...].astype(BF16), wp_ref[...], preferred_element_type=F32)
    x3 = x2 + emb * gate
    o_ref[...] = _rms(x3, gfin_ref[...])


def _ple_final(pos, x1_rows, y_sorted, p2, g_ple, w_gate_bf, b_gate, w_proj_bf, g_final, tm):
    t_tokens = p2.shape[0]
    row = lambda i, pos: (i, 0)
    fixed = lambda i, pos: (0, 0)
    vec = pl.BlockSpec((1, D_MODEL), fixed)
    return pl.pallas_call(
        _ple_final_kernel,
        grid_spec=pltpu.PrefetchScalarGridSpec(
            num_scalar_prefetch=1,
            grid=(t_tokens // tm,),
            in_specs=[pl.BlockSpec((tm * ROW_CHUNKS, LANES), row), pl.BlockSpec(memory_space=pl.ANY),
                      pl.BlockSpec((tm, PLE_DIM), row), vec, pl.BlockSpec((D_MODEL, D_MODEL), fixed), vec,
                      pl.BlockSpec((PLE_DIM, D_MODEL), fixed), vec],
            out_specs=pl.BlockSpec((tm, D_MODEL), row),
            scratch_shapes=[pltpu.VMEM((2, tm * ROW_CHUNKS, LANES), F32), pltpu.SemaphoreType.DMA((2,))],
        ),
        out_shape=jax.ShapeDtypeStruct((t_tokens, D_MODEL), F32),
        compiler_params=pltpu.CompilerParams(dimension_semantics=("arbitrary",), vmem_limit_bytes=VMEM_LIMIT),
        name="ple_final",
    )(pos, x1_rows, y_sorted, p2, g_ple.reshape(1, -1), w_gate_bf, b_gate.reshape(1, -1), w_proj_bf,
      g_final.reshape(1, -1))


def kernel(x, p, g_mix, w_in, w_dw, b_dw, conv_ln_g, conv_ln_b, beta_conv, beta_attn, w_out, g_ffn, w_group, b_group, w_erouter, b_erouter, w_gate, w_up, w_down, g_ple, w_ple_gate, b_ple_gate, w_ple_proj, g_final):
    batch, seq, _ = x.shape
    assert p.shape[0] == 1 and seq % MOBA_BLOCK == 0
    t_tokens = batch * seq
    x2 = x.reshape(t_tokens, D_MODEL)
    i = 0
    u, qp, kp, vp = _in_proj(x2, g_mix[i], w_in[i].astype(BF16), seq, tm=512)
    conv_n = _conv(u, w_dw[i], b_dw[i], conv_ln_g[i], conv_ln_b[i], beta_conv[i], batch, seq, ts=256)
    attn = _moba(qp, kp, vp, batch, seq)
    gap = ROUTER_EXPERT_COL - N_GROUPS
    tail = LANES - ROUTER_EXPERT_COL - N_EXPERTS
    w_router = jnp.concatenate(
        [w_group[i], jnp.zeros((D_MODEL, gap), F32),
         jnp.transpose(w_erouter[i], (1, 0, 2)).reshape(D_MODEL, N_EXPERTS),
         jnp.zeros((D_MODEL, tail), F32)], axis=1).astype(BF16)
    b_router = jnp.concatenate([b_group[i], jnp.zeros((gap,), F32), b_erouter[i].reshape(-1),
                                jnp.zeros((tail,), F32)]).reshape(1, LANES)
    x1_rows, bucket, rank, counts = _out_proj(x2, conv_n, attn, beta_attn[i], w_out[i].astype(BF16), g_ffn[i],
                                              w_router, b_router, tm=512)
    moe_tm = 256
    starts, schedule = _moe_schedule(counts[:, 0].astype(jnp.int32), t_tokens // moe_tm, moe_tm)
    pos = _positions(bucket, rank, starts, tm=2048)
    x1_sorted = _dispatch(pos, x1_rows, tm=1024)
    y_sorted = _moe(schedule, x1_sorted, g_ffn[i], w_router, b_router, w_gate[i].astype(BF16),
                    w_up[i].astype(BF16), w_down[i].astype(BF16), moe_tm)
    out = _ple_final(pos, x1_rows, y_sorted, p[i].reshape(t_tokens, PLE_DIM), g_ple[i],
                     w_ple_gate[i].astype(BF16), b_ple_gate[i], w_ple_proj[i].astype(BF16), g_final, tm=512)
    return out.reshape(batch, seq, D_MODEL)
```

```python
import jax
import jax.numpy as jnp
from jax import lax
from jax.experimental import pallas as pl
from jax.experimental.pallas import tpu as pltpu

F32 = jnp.float32
BF16 = jnp.bfloat16

D_MODEL = 1024
CONV_CH = 512
CONV_WIDTH = 31
ATTN_WIDTH = 512
HEAD_DIM = 64
N_HEADS = 8
ROT_DIM = 16
ROPE_THETA = 500000.0
MOBA_BLOCK = 256
MOBA_TOPK = 3
N_GROUPS = 4
EXPERTS_PER_GROUP = 8
N_EXPERTS = 32
D_EXPERT = 256
PLE_DIM = 256
N_BUCKETS = N_GROUPS * EXPERTS_PER_GROUP * EXPERTS_PER_GROUP
N_PAIR_BUCKETS = N_GROUPS * EXPERTS_PER_GROUP * (EXPERTS_PER_GROUP - 1) // 2
IN_COLS = 2 * CONV_CH + 3 * ATTN_WIDTH
EPS = 1e-6

LANES = 128
SUBLANES = 8
HEAD_PAD = LANES
QKV_PAD = N_HEADS * HEAD_PAD
ROW_CHUNKS = D_MODEL // LANES
assert ROW_CHUNKS == SUBLANES
MASK_NEG = -1e30
Q_SCALE = HEAD_DIM ** -0.5 * 1.4426950408889634
CONV_HALO = 32
CONV_ROWS = 32
DMA_GROUP = 8
ROUTER_EXPERT_COL = SUBLANES
VMEM_LIMIT = 56 * 1024 * 1024


def _rms(x, g):
    return x * lax.rsqrt(jnp.mean(x * x, axis=-1, keepdims=True) + EPS) * g


def _load_row_tiles(ref, n_rows):
    return jnp.concatenate([ref[pl.ds(c, n_rows, stride=ROW_CHUNKS), :] for c in range(ROW_CHUNKS)], axis=1)


def _store_row_tiles(ref, val, accumulate=False):
    n_rows = val.shape[0]
    for c in range(ROW_CHUNKS):
        idx = (pl.ds(c, n_rows, stride=ROW_CHUNKS), slice(None))
        piece = val[:, c * LANES:(c + 1) * LANES]
        ref[idx] = ref[idx] + piece if accumulate else piece


def _router_logits(h_bf16, wr_ref, br_ref):
    return jnp.dot(h_bf16, wr_ref[...], preferred_element_type=F32) + br_ref[...]


def _in_proj_kernel(x_ref, g_ref, w_ref, rc_ref, ra_ref, rb_ref, u_ref, q_ref, k_ref, v_ref):
    h = _rms(x_ref[...], g_ref[...]).astype(BF16)
    proj = jnp.dot(h, w_ref[...], preferred_element_type=F32)
    u_ref[...] = proj[:, :CONV_CH] * jax.nn.sigmoid(proj[:, CONV_CH:2 * CONV_CH])
    rc, ra, rb = rc_ref[...], ra_ref[...], rb_ref[...]
    low = lax.broadcasted_iota(jnp.int32, (x_ref.shape[0], LANES), 1) < HEAD_DIM

    def put(dst_ref, base, rope, scale, spare):
        for c in range(ATTN_WIDTH // LANES):
            t = proj[:, base + c * LANES: base + (c + 1) * LANES]
            if rope:
                t = t * rc + pltpu.roll(t, LANES - ROT_DIM // 2, 1) * ra + pltpu.roll(t, ROT_DIM // 2, 1) * rb
            if scale != 1.0:
                t = t * scale
            even = jnp.where(low, t, spare)
            odd = jnp.where(low, pltpu.roll(t, HEAD_DIM, 1), spare)
            dst_ref[:, (2 * c) * HEAD_PAD:(2 * c + 1) * HEAD_PAD] = even.astype(dst_ref.dtype)
            dst_ref[:, (2 * c + 1) * HEAD_PAD:(2 * c + 2) * HEAD_PAD] = odd.astype(dst_ref.dtype)

    ones_lane = jnp.where(lax.broadcasted_iota(jnp.int32, (x_ref.shape[0], LANES), 1) == HEAD_DIM, 1.0, 0.0)
    put(q_ref, 2 * CONV_CH, True, Q_SCALE, 0.0)
    put(k_ref, 2 * CONV_CH + ATTN_WIDTH, True, 1.0, 0.0)
    put(v_ref, 2 * CONV_CH + 2 * ATTN_WIDTH, False, 1.0, ones_lane)


def _rope_coeff_tables(seq):
    pos = jnp.arange(seq, dtype=F32)
    inv_freq = jnp.power(jnp.float32(ROPE_THETA), -jnp.arange(0, ROT_DIM, 2, dtype=F32) / ROT_DIM)
    ang = pos[:, None] * inv_freq[None, :]
    cos, sin = jnp.cos(ang), jnp.sin(ang)
    zeros = jnp.zeros((seq, HEAD_DIM - ROT_DIM), F32)
    c_head = jnp.concatenate([cos, cos, jnp.ones_like(zeros)], axis=1)
    a_head = jnp.concatenate([-sin, jnp.zeros_like(sin), zeros], axis=1)
    b_head = jnp.concatenate([jnp.zeros_like(sin), sin, zeros], axis=1)
    rep = LANES // HEAD_DIM
    return jnp.tile(c_head, (1, rep)), jnp.tile(a_head, (1, rep)), jnp.tile(b_head, (1, rep))


def _in_proj(x2, g_mix, w_in_bf, seq, tm):
    t_tokens = x2.shape[0]
    rc, ra, rb = _rope_coeff_tables(seq)
    n_seq_tiles = seq // tm
    row = lambda i: (i, 0)
    fixed = lambda i: (0, 0)
    tab = lambda i: (i % n_seq_tiles, 0)
    return pl.pallas_call(
        _in_proj_kernel,
        grid=(t_tokens // tm,),
        in_specs=[pl.BlockSpec((tm, D_MODEL), row), pl.BlockSpec((1, D_MODEL), fixed),
                  pl.BlockSpec((D_MODEL, IN_COLS), fixed),
                  pl.BlockSpec((tm, LANES), tab), pl.BlockSpec((tm, LANES), tab), pl.BlockSpec((tm, LANES), tab)],
        out_specs=[pl.BlockSpec((tm, CONV_CH), row), pl.BlockSpec((tm, QKV_PAD), row),
                   pl.BlockSpec((tm, QKV_PAD), row), pl.BlockSpec((tm, QKV_PAD), row)],
        out_shape=[jax.ShapeDtypeStruct((t_tokens, CONV_CH), F32),
                   jax.ShapeDtypeStruct((t_tokens, QKV_PAD), BF16),
                   jax.ShapeDtypeStruct((t_tokens, QKV_PAD), BF16),
                   jax.ShapeDtypeStruct((t_tokens, QKV_PAD), BF16)],
        compiler_params=pltpu.CompilerParams(dimension_semantics=("arbitrary",), vmem_limit_bytes=VMEM_LIMIT),
        name="in_proj",
    )(x2, g_mix.reshape(1, D_MODEL), w_in_bf, rc, ra, rb)


def _conv_begin(u_ref, pad_ref, shift_ref, first_tile):
    ts = u_ref.shape[0]

    @pl.when(first_tile)
    def _():
        pad_ref[0:CONV_HALO, :] = jnp.zeros((CONV_HALO, CONV_CH), F32)

    pad_ref[CONV_HALO:CONV_HALO + ts, :] = u_ref[...]
    kept = ts + CONV_HALO - SUBLANES
    for sh in range(1, SUBLANES):
        shift_ref[sh - 1, 0:kept, :] = pad_ref[sh:sh + kept, :]


def _conv_rows(r0, w, pad_ref, shift_ref, b_ref, lg_ref, lb_ref, beta_ref, o_ref):
    first = CONV_HALO - (CONV_WIDTH - 1)

    def window(start):
        sh, base = start % SUBLANES, start - start % SUBLANES
        if sh == 0:
            return pad_ref[base:base + CONV_ROWS, :]
        return shift_ref[sh - 1, base:base + CONV_ROWS, :]

    acc = jnp.zeros((CONV_ROWS, CONV_CH), F32)
    for tap in range(CONV_WIDTH):
        acc = acc + window(r0 + first + tap) * w[tap:tap + 1, :]
    y = acc + b_ref[...]
    mu = jnp.mean(y, axis=-1, keepdims=True)
    d = y - mu
    var = jnp.mean(d * d, axis=-1, keepdims=True)
    y = d * lax.rsqrt(var + EPS) * lg_ref[...] + lb_ref[...]
    y = y * jax.nn.sigmoid(y)
    o_ref[r0:r0 + CONV_ROWS, :] = _rms(y, beta_ref[...]).astype(o_ref.dtype)


def _conv_end(pad_ref, ts):
    pad_ref[0:CONV_HALO, :] = pad_ref[ts:ts + CONV_HALO, :]


def _conv_kernel(u_ref, w_ref, b_ref, lg_ref, lb_ref, beta_ref, o_ref, pad_ref, shift_ref):
    ts = u_ref.shape[0]
    _conv_begin(u_ref, pad_ref, shift_ref, pl.program_id(1) == 0)
    w = w_ref[...]
    for r0 in range(0, ts, CONV_ROWS):
        _conv_rows(r0, w, pad_ref, shift_ref, b_ref, lg_ref, lb_ref, beta_ref, o_ref)
    _conv_end(pad_ref, ts)


def _conv(u2, w_dw, b_dw, ln_g, ln_b, beta, batch, seq, ts):
    n_s = seq // ts
    row = lambda b, s: (b * n_s + s, 0)
    fixed = lambda b, s: (0, 0)
    vec = pl.BlockSpec((1, CONV_CH), fixed)
    return pl.pallas_call(
        _conv_kernel,
        grid=(batch, n_s),
        in_specs=[pl.BlockSpec((ts, CONV_CH), row), pl.BlockSpec((CONV_WIDTH, CONV_CH), fixed), vec, vec, vec, vec],
        out_specs=pl.BlockSpec((ts, CONV_CH), row),
        out_shape=jax.ShapeDtypeStruct((batch * seq, CONV_CH), BF16),
        scratch_shapes=[pltpu.VMEM((ts + CONV_HALO, CONV_CH), F32),
                        pltpu.VMEM((SUBLANES - 1, ts + CONV_HALO, CONV_CH), F32)],
        compiler_params=pltpu.CompilerParams(dimension_semantics=("arbitrary", "arbitrary"),
                                             vmem_limit_bytes=VMEM_LIMIT),
        name="conformer_conv",
    )(u2, w_dw, b_dw.reshape(1, -1), ln_g.reshape(1, -1), ln_b.reshape(1, -1), beta.reshape(1, -1))


def _moba_kernel(q_ref, k_ref, v_ref, blk_ref, place_ref, o_ref):
    seq = q_ref.shape[0]
    nb = seq // MOBA_BLOCK
    blk_onehot = blk_ref[...]
    place = place_ref[...]
    sub = lax.broadcasted_iota(jnp.int32, (nb, seq), 0)
    q_blk = lax.broadcasted_iota(jnp.int32, (nb, seq), 1) // MOBA_BLOCK
    past = sub < q_blk
    r_io = lax.broadcasted_iota(jnp.int32, (MOBA_BLOCK, MOBA_BLOCK), 0)
    c_io = lax.broadcasted_iota(jnp.int32, (MOBA_BLOCK, MOBA_BLOCK), 1)
    causal = c_io <= r_io
    low = lax.broadcasted_iota(jnp.int32, (MOBA_BLOCK, LANES), 1) < HEAD_DIM
    contract_lanes = (((1,), (1,)), ((), ()))
    q_aug, k_aug = [], []
    for hh in range(2):
        lo, hi = hh * HEAD_PAD, (hh + 1) * HEAD_PAD
        k = k_ref[:, lo:hi]
        q = q_ref[:, lo:hi]
        kf = k.astype(F32)
        k_mean = jnp.concatenate(
            [jnp.mean(kf[j * MOBA_BLOCK:(j + 1) * MOBA_BLOCK], axis=0, keepdims=True) for j in range(nb)], axis=0)
        km_hi = k_mean.astype(BF16)
        km_lo = (k_mean - km_hi.astype(F32)).astype(BF16)
        gate = (lax.dot_general(km_hi, q, contract_lanes, preferred_element_type=F32)
                + lax.dot_general(km_lo, q, contract_lanes, preferred_element_type=F32))
        bias_t = jnp.where(sub == q_blk, 0.0, MASK_NEG)
        for j in range(nb - 1):
            gj = gate[j:j + 1, :]
            beats = ((gate > gj) | ((gate == gj) & (sub < j))) & past
            cnt = jnp.sum(beats.astype(F32), axis=0, keepdims=True)
            bias_t = jnp.where((sub == j) & past & (cnt < float(MOBA_TOPK)), 0.0, bias_t)
        bias = jnp.dot(bias_t.T.astype(BF16), place, preferred_element_type=F32)
        q_aug.append(q + bias.astype(BF16))
        k_aug.append(k + blk_onehot)

    def scores(hh, i):
        return lax.dot_general(q_aug[hh][i * MOBA_BLOCK:(i + 1) * MOBA_BLOCK], k_aug[hh][:(i + 1) * MOBA_BLOCK],
                               contract_lanes, preferred_element_type=F32)

    def attend(s, hh, i):
        v_lo = hh * HEAD_PAD
        s_own = jnp.where(causal, s[:, i * MOBA_BLOCK:], MASK_NEG)
        m = jnp.max(s_own, axis=-1, keepdims=True)
        if i > 0:
            s_past = s[:, :i * MOBA_BLOCK]
            m = jnp.maximum(m, jnp.max(s_past, axis=-1, keepdims=True))
        acc = jnp.dot(jnp.exp2((s_own - m).astype(BF16)),
                      v_ref[i * MOBA_BLOCK:(i + 1) * MOBA_BLOCK, v_lo:v_lo + HEAD_PAD], preferred_element_type=F32)
        if i > 0:
            acc = acc + jnp.dot(jnp.exp2((s_past - m).astype(BF16)), v_ref[:i * MOBA_BLOCK, v_lo:v_lo + HEAD_PAD],
                                preferred_element_type=F32)
        return jnp.where(low, acc / acc[:, HEAD_DIM:HEAD_DIM + 1], 0.0)

    items = [(hh, i) for i in range(nb) for hh in range(2)]
    s_cur = scores(*items[0])
    even = None
    for n, (hh, i) in enumerate(items):
        s_next = scores(*items[n + 1]) if n + 1 < len(items) else None
        out = attend(s_cur, hh, i)
        if hh == 0:
            even = out
        else:
            o_ref[i * MOBA_BLOCK:(i + 1) * MOBA_BLOCK, :] = even + pltpu.roll(out, HEAD_DIM, 1)
        s_cur = s_next


def _moba(qp, kp, vp, batch, seq):
    nb = seq // MOBA_BLOCK
    lane = jnp.arange(LANES)[None, :]
    blk = (jnp.arange(seq) // MOBA_BLOCK)[:, None]
    blk_onehot = (lane == HEAD_DIM + blk).astype(BF16)
    place = (lane == HEAD_DIM + jnp.arange(nb)[:, None]).astype(BF16)
    pair = lambda b, c: (b, c)
    fixed = lambda b, c: (0, 0)
    spec = pl.BlockSpec((seq, 2 * HEAD_PAD), pair)
    return pl.pallas_call(
        _moba_kernel,
        grid=(batch, N_HEADS // 2),
        in_specs=[spec, spec, spec, pl.BlockSpec((seq, LANES), fixed), pl.BlockSpec((nb, LANES), fixed)],
        out_specs=pl.BlockSpec((seq, LANES), pair),
        out_shape=jax.ShapeDtypeStruct((batch * seq, ATTN_WIDTH), F32),
        compiler_params=pltpu.CompilerParams(dimension_semantics=("arbitrary", "arbitrary"),
                                             vmem_limit_bytes=VMEM_LIMIT),
        name="moba_attention",
    )(qp, kp, vp, blk_onehot, place)


def _out_proj_kernel(x_ref, c_ref, a_ref, beta_ref, wc_ref, wa_ref, gf_ref, wr_ref, br_ref, tri_ref,
                     x1_ref, bucket_ref, rank_ref, cnt_ref, carry_ref):
    @pl.when(pl.program_id(0) == 0)
    def _():
        carry_ref[...] = jnp.zeros_like(carry_ref)

    an = _rms(a_ref[...], beta_ref[...]).astype(BF16)
    x1 = (x_ref[...] + jnp.dot(c_ref[...], wc_ref[...], preferred_element_type=F32)
          + jnp.dot(an, wa_ref[...], preferred_element_type=F32))
    _store_row_tiles(x1_ref, x1)
    logits = _router_logits(_rms(x1, gf_ref[...]).astype(BF16), wr_ref, br_ref)
    tm = logits.shape[0]
    lt = logits.T
    sub = lax.broadcasted_iota(jnp.int32, (SUBLANES, tm), 0)
    gl = jnp.where(sub < N_GROUPS, lt[0:SUBLANES, :], -jnp.inf)
    g_idx = jnp.min(jnp.where(gl == jnp.max(gl, axis=0, keepdims=True), sub, N_GROUPS), axis=0, keepdims=True)
    el = jnp.zeros((EXPERTS_PER_GROUP, tm), F32)
    for g in range(N_GROUPS):
        lo = ROUTER_EXPERT_COL + g * EXPERTS_PER_GROUP
        el = jnp.where(g_idx == g, lt[lo:lo + EXPERTS_PER_GROUP, :], el)
    i1 = jnp.min(jnp.where(el == jnp.max(el, axis=0, keepdims=True), sub, EXPERTS_PER_GROUP), axis=0, keepdims=True)
    el2 = jnp.where(sub == i1, -jnp.inf, el)
    i2 = jnp.min(jnp.where(el2 == jnp.max(el2, axis=0, keepdims=True), sub, EXPERTS_PER_GROUP), axis=0,
                 keepdims=True)
    bucket = (g_idx * (EXPERTS_PER_GROUP * EXPERTS_PER_GROUP) + jnp.minimum(i1, i2) * EXPERTS_PER_GROUP
              + jnp.maximum(i1, i2))
    onehot = (lax.broadcasted_iota(jnp.int32, (N_BUCKETS, tm), 0) == bucket)
    onehot_bf = onehot.astype(BF16)
    before = jnp.dot(onehot_bf, tri_ref[...], preferred_element_type=F32)
    counts = jnp.dot(onehot_bf, jnp.ones((tm, LANES), BF16), preferred_element_type=F32)
    carry = carry_ref[...]
    carry_wide = jnp.concatenate([carry] * (tm // LANES), axis=1)
    rank = jnp.sum(jnp.where(onehot, before + carry_wide, 0.0), axis=0, keepdims=True)
    carry_ref[...] = carry + counts
    cnt_ref[...] = carry + counts
    bucket_ref[0] = bucket
    rank_ref[0] = rank.astype(jnp.int32)


def _out_proj(x2, conv_n, attn, beta_attn, w_out_bf, g_ffn, w_router_bf, b_router, tm):
    t_tokens = x2.shape[0]
    row = lambda i: (i, 0)
    fixed = lambda i: (0, 0)
    tri = (jnp.arange(tm)[:, None] < jnp.arange(tm)[None, :]).astype(BF16)
    lane_row = pl.BlockSpec((1, 1, tm), lambda i: (i, 0, 0))
    return pl.pallas_call(
        _out_proj_kernel,
        grid=(t_tokens // tm,),
        in_specs=[pl.BlockSpec((tm, D_MODEL), row), pl.BlockSpec((tm, CONV_CH), row),
                  pl.BlockSpec((tm, ATTN_WIDTH), row), pl.BlockSpec((1, ATTN_WIDTH), fixed),
                  pl.BlockSpec((CONV_CH, D_MODEL), fixed), pl.BlockSpec((ATTN_WIDTH, D_MODEL), fixed),
                  pl.BlockSpec((1, D_MODEL), fixed), pl.BlockSpec((D_MODEL, LANES), fixed),
                  pl.BlockSpec((1, LANES), fixed), pl.BlockSpec((tm, tm), fixed)],
        out_specs=[pl.BlockSpec((tm * ROW_CHUNKS, LANES), row), lane_row, lane_row,
                   pl.BlockSpec((N_BUCKETS, LANES), fixed)],
        out_shape=[jax.ShapeDtypeStruct((t_tokens * ROW_CHUNKS, LANES), F32),
                   jax.ShapeDtypeStruct((t_tokens // tm, 1, tm), jnp.int32),
                   jax.ShapeDtypeStruct((t_tokens // tm, 1, tm), jnp.int32),
                   jax.ShapeDtypeStruct((N_BUCKETS, LANES), F32)],
        scratch_shapes=[pltpu.VMEM((N_BUCKETS, LANES), F32)],
        compiler_params=pltpu.CompilerParams(dimension_semantics=("arbitrary",), vmem_limit_bytes=VMEM_LIMIT),
        name="out_proj_router",
    )(x2, conv_n, attn, beta_attn.reshape(1, -1), w_out_bf[:CONV_CH], w_out_bf[CONV_CH:],
      g_ffn.reshape(1, -1), w_router_bf, b_router, tri)


def _positions_kernel(bucket_ref, rank_ref, starts_ref, pos_ref):
    bucket = bucket_ref[0]
    ids = lax.broadcasted_iota(jnp.int32, (N_BUCKETS, bucket.shape[1]), 0)
    pos_ref[0] = jnp.sum(jnp.where(ids == bucket, starts_ref[...], 0), axis=0, keepdims=True) + rank_ref[0]


def _positions(bucket, rank, starts, tm):
    t_tokens = bucket.size
    lane_row = pl.BlockSpec((1, 1, tm), lambda i: (i, 0, 0))
    shape = (t_tokens // tm, 1, tm)
    pos = pl.pallas_call(
        _positions_kernel,
        grid=(t_tokens // tm,),
        in_specs=[lane_row, lane_row, pl.BlockSpec((N_BUCKETS, 1), lambda i: (0, 0))],
        out_specs=lane_row,
        out_shape=jax.ShapeDtypeStruct(shape, jnp.int32),
        compiler_params=pltpu.CompilerParams(dimension_semantics=("arbitrary",), vmem_limit_bytes=VMEM_LIMIT),
        name="moe_positions",
    )(bucket.reshape(shape), rank.reshape(shape), starts.reshape(N_BUCKETS, 1))
    return pos.reshape(t_tokens)


def _row_tile(ref, r):
    return ref.at[pl.ds(pl.multiple_of(r * ROW_CHUNKS, ROW_CHUNKS), ROW_CHUNKS), :]


def _dispatch_kernel(pos_ref, x_ref, out_hbm, sem):
    tm = x_ref.shape[0] // ROW_CHUNKS
    base = pl.program_id(0) * tm

    def row_copy(r, dst):
        return pltpu.make_async_copy(_row_tile(x_ref, r), _row_tile(out_hbm, dst), sem)

    def issue(g, carry):
        for u in range(DMA_GROUP):
            r = g * DMA_GROUP + u
            row_copy(r, pos_ref[base + r]).start(priority=u % 2)
        return carry

    lax.fori_loop(0, tm // DMA_GROUP, issue, 0)

    def drain(r, carry):
        row_copy(0, 0).wait()
        return carry

    lax.fori_loop(0, tm, drain, 0, unroll=8)


def _dispatch(pos, x_rows, tm):
    n_rows = x_rows.shape[0]
    return pl.pallas_call(
        _dispatch_kernel,
        grid_spec=pltpu.PrefetchScalarGridSpec(
            num_scalar_prefetch=1,
            grid=(n_rows // (tm * ROW_CHUNKS),),
            in_specs=[pl.BlockSpec((tm * ROW_CHUNKS, LANES), lambda i, pos: (i, 0))],
            out_specs=pl.BlockSpec(memory_space=pl.ANY),
            scratch_shapes=[pltpu.SemaphoreType.DMA(())],
        ),
        out_shape=jax.ShapeDtypeStruct((n_rows, LANES), F32),
        compiler_params=pltpu.CompilerParams(dimension_semantics=("arbitrary",), vmem_limit_bytes=VMEM_LIMIT,
                                             has_side_effects=True),
        name="moe_dispatch",
    )(pos, x_rows)


def _moe_schedule(counts, n_tiles, tm):
    n_steps = n_tiles + N_PAIR_BUCKETS
    ends = jnp.cumsum(counts)
    starts = ends - counts
    first_tile = starts // tm
    steps_b = jnp.where(counts > 0, (ends - 1) // tm - first_tile + 1, 0)
    step_end = jnp.cumsum(steps_b)
    step_start = step_end - steps_b
    total = step_end[-1]
    s = jnp.minimum(jnp.arange(n_steps, dtype=jnp.int32), total - 1)[:, None]
    mine = (s >= step_start[None, :]) & (s < step_end[None, :])
    pick = lambda v: jnp.sum(jnp.where(mine, v[None, :], 0), axis=1).astype(jnp.int32)
    ids = jnp.arange(N_BUCKETS, dtype=jnp.int32)
    pair = EXPERTS_PER_GROUP * EXPERTS_PER_GROUP
    tile = pick(first_tile - step_start) + s[:, 0]
    return starts, (tile, pick(ids // pair), pick((ids % pair) // EXPERTS_PER_GROUP), pick(ids % EXPERTS_PER_GROUP),
                    pick(starts), pick(ends), total.astype(jnp.int32).reshape(1))


def _moe_kernel(tile_ref, grp_ref, elo_ref, ehi_ref, rlo_ref, rhi_ref, total_ref,
                x_ref, gf_ref, wr_ref, br_ref, wgl_ref, wul_ref, wdl_ref, wgh_ref, wuh_ref, wdh_ref, y_ref, acc_ref):
    s = pl.program_id(0)
    tm = x_ref.shape[0] // ROW_CHUNKS
    tile = tile_ref[s]
    total = total_ref[0]
    first = jnp.logical_or(s == 0, tile != tile_ref[jnp.maximum(s - 1, 0)])
    last = jnp.logical_or(s == total - 1, tile != tile_ref[jnp.minimum(s + 1, pl.num_programs(0) - 1)])

    @pl.when(s < total)
    def _():
        rows = tile * tm + lax.broadcasted_iota(jnp.int32, (tm, 1), 0)
        inside = (rows >= rlo_ref[s]) & (rows < rhi_ref[s])
        x = _load_row_tiles(x_ref, tm)
        hg = (x * gf_ref[...]).astype(BF16)
        inv_rms = lax.rsqrt(jnp.mean(x * x, axis=-1, keepdims=True) + EPS)
        logits = inv_rms * jnp.dot(hg, wr_ref[...], preferred_element_type=F32) + br_ref[...]
        lane = lax.broadcasted_iota(jnp.int32, (tm, LANES), 1)
        col = ROUTER_EXPERT_COL + grp_ref[s] * EXPERTS_PER_GROUP
        lane_val = lambda idx: jnp.sum(jnp.where(lane == idx, logits, 0.0), axis=-1, keepdims=True)
        gl = jnp.where(lane < N_GROUPS, logits, -jnp.inf)
        g_w = 1.0 / jnp.sum(jnp.exp(gl - jnp.max(gl, axis=-1, keepdims=True)), axis=-1, keepdims=True)
        v_lo, v_hi = lane_val(col + elo_ref[s]), lane_val(col + ehi_ref[s])
        v_max = jnp.maximum(v_lo, v_hi)
        p_lo, p_hi = jnp.exp(v_lo - v_max), jnp.exp(v_hi - v_max)
        scale = jnp.where(inside, inv_rms * g_w / (p_lo + p_hi), 0.0)

        def expert(wg_ref, wu_ref, wd_ref, w_row):
            a = inv_rms * jnp.dot(hg, wg_ref[0], preferred_element_type=F32)
            b = jnp.dot(hg, wu_ref[0], preferred_element_type=F32)
            hid = a * jax.nn.sigmoid(a) * b * w_row
            return jnp.dot(hid.astype(BF16), wd_ref[0], preferred_element_type=F32)

        y = expert(wgl_ref, wul_ref, wdl_ref, p_lo * scale) + expert(wgh_ref, wuh_ref, wdh_ref, p_hi * scale)

        @pl.when(first)
        def _():
            acc_ref[...] = y

        @pl.when(jnp.logical_not(first))
        def _():
            acc_ref[...] += y

        @pl.when(last)
        def _():
            _store_row_tiles(y_ref, acc_ref[...])


def _moe(schedule, x_sorted, g_ffn, w_router_bf, b_router, wg_bf, wu_bf, wd_bf, tm):
    n_rows = x_sorted.shape[0]
    n_steps = schedule[0].shape[0]
    row = lambda s, tile, *_: (tile[s], 0)
    fixed = lambda s, *_: (0, 0)
    w_lo = lambda s, tile, grp, elo, ehi, *_: (grp[s] * EXPERTS_PER_GROUP + elo[s], 0, 0)
    w_hi = lambda s, tile, grp, elo, ehi, *_: (grp[s] * EXPERTS_PER_GROUP + ehi[s], 0, 0)
    up_spec = lambda sel: pl.BlockSpec((1, D_MODEL, D_EXPERT), sel)
    down_spec = lambda sel: pl.BlockSpec((1, D_EXPERT, D_MODEL), sel)
    return pl.pallas_call(
        _moe_kernel,
        grid_spec=pltpu.PrefetchScalarGridSpec(
            num_scalar_prefetch=7,
            grid=(n_steps,),
            in_specs=[pl.BlockSpec((tm * ROW_CHUNKS, LANES), row), pl.BlockSpec((1, D_MODEL), fixed),
                      pl.BlockSpec((D_MODEL, LANES), fixed), pl.BlockSpec((1, LANES), fixed),
                      up_spec(w_lo), up_spec(w_lo), down_spec(w_lo), up_spec(w_hi), up_spec(w_hi), down_spec(w_hi)],
            out_specs=pl.BlockSpec((tm * ROW_CHUNKS, LANES), row),
            scratch_shapes=[pltpu.VMEM((tm, D_MODEL), F32)],
        ),
        out_shape=jax.ShapeDtypeStruct((n_rows, LANES), F32),
        compiler_params=pltpu.CompilerParams(dimension_semantics=("arbitrary",), vmem_limit_bytes=VMEM_LIMIT),
        name="hier_moe",
    )(*schedule, x_sorted, g_ffn.reshape(1, -1), w_router_bf, b_router, wg_bf, wu_bf, wd_bf, wg_bf, wu_bf, wd_bf)


def _ple_final_kernel(pos_ref, x1_ref, ys_hbm, p_ref, gp_ref, wg_ref, bg_ref, wp_ref, gfin_ref, o_ref,
                      ybuf, sems):
    i = pl.program_id(0)
    n = pl.num_programs(0)
    tm = o_ref.shape[0]

    def row_copy(src, r, slot):
        return pltpu.make_async_copy(_row_tile(ys_hbm, src), _row_tile(ybuf.at[slot], r), sems.at[slot])

    def gather(tile, slot):
        def issue(g, carry):
            for u in range(DMA_GROUP):
                r = g * DMA_GROUP + u
                row_copy(pos_ref[tile * tm + r], r, slot).start(priority=u % 2)
            return carry
        lax.fori_loop(0, tm // DMA_GROUP, issue, 0)

    @pl.when(i == 0)
    def _():
        gather(0, 0)

    slot = i % 2

    def drain(r, carry):
        row_copy(0, r, slot).wait()
        return carry

    def step(prefetch_next):
        lax.fori_loop(0, tm, drain, 0, unroll=8)
        x2 = _load_row_tiles(x1_ref, tm) + _load_row_tiles(ybuf.at[slot], tm)
        if prefetch_next:
            for r in range(tm):
                row_copy(pos_ref[(i + 1) * tm + r], r, 1 - slot).start(priority=r % 2)
        hg = _rms(x2, gp_ref[...]).astype(BF16)
        gate = jax.nn.sigmoid(jnp.dot(hg, wg_ref[...], preferred_element_type=F32) + bg_ref[...])
        emb = jnp.dot(p_ref[...].astype(BF16), wp_ref[...], preferred_element_type=F32)
        x3 = x2 + emb * gate
        o_ref[...] = _rms(x3, gfin_ref[...])

    @pl.when(i + 1 < n)
    def _():
        step(True)

    @pl.when(i + 1 == n)
    def _():
        step(False)


def _ple_final(pos, x1_rows, y_sorted, p2, g_ple, w_gate_bf, b_gate, w_proj_bf, g_final, tm):
    t_tokens = p2.shape[0]
    row = lambda i, pos: (i, 0)
    fixed = lambda i, pos: (0, 0)
    vec = pl.BlockSpec((1, D_MODEL), fixed)
    return pl.pallas_call(
        _ple_final_kernel,
        grid_spec=pltpu.PrefetchScalarGridSpec(
            num_scalar_prefetch=1,
            grid=(t_tokens // tm,),
            in_specs=[pl.BlockSpec((tm * ROW_CHUNKS, LANES), row), pl.BlockSpec(memory_space=pl.ANY),
                      pl.BlockSpec((tm, PLE_DIM), row), vec, pl.BlockSpec((D_MODEL, D_MODEL), fixed), vec,
                      pl.BlockSpec((PLE_DIM, D_MODEL), fixed), vec],
            out_specs=pl.BlockSpec((tm, D_MODEL), row),
            scratch_shapes=[pltpu.VMEM((2, tm * ROW_CHUNKS, LANES), F32), pltpu.SemaphoreType.DMA((2,))],
        ),
        out_shape=jax.ShapeDtypeStruct((t_tokens, D_MODEL), F32),
        compiler_params=pltpu.CompilerParams(dimension_semantics=("arbitrary",), vmem_limit_bytes=VMEM_LIMIT),
        name="ple_final",
    )(pos, x1_rows, y_sorted, p2, g_ple.reshape(1, -1), w_gate_bf, b_gate.reshape(1, -1), w_proj_bf,
      g_final.reshape(1, -1))


def kernel(x, p, g_mix, w_in, w_dw, b_dw, conv_ln_g, conv_ln_b, beta_conv, beta_attn, w_out, g_ffn, w_group, b_group, w_erouter, b_erouter, w_gate, w_up, w_down, g_ple, w_ple_gate, b_ple_gate, w_ple_proj, g_final):
    batch, seq, _ = x.shape
    assert p.shape[0] == 1 and seq % MOBA_BLOCK == 0
    t_tokens = batch * seq
    x2 = x.reshape(t_tokens, D_MODEL)
    i = 0
    u, qp, kp, vp = _in_proj(x2, g_mix[i], w_in[i].astype(BF16), seq, tm=512)
    conv_n = _conv(u, w_dw[i], b_dw[i], conv_ln_g[i], conv_ln_b[i], beta_conv[i], batch, seq, ts=256)
    attn = _moba(qp, kp, vp, batch, seq)
    gap = ROUTER_EXPERT_COL - N_GROUPS
    tail = LANES - ROUTER_EXPERT_COL - N_EXPERTS
    w_router = jnp.concatenate(
        [w_group[i], jnp.zeros((D_MODEL, gap), F32),
         jnp.transpose(w_erouter[i], (1, 0, 2)).reshape(D_MODEL, N_EXPERTS),
         jnp.zeros((D_MODEL, tail), F32)], axis=1).astype(BF16)
    b_router = jnp.concatenate([b_group[i], jnp.zeros((gap,), F32), b_erouter[i].reshape(-1),
                                jnp.zeros((tail,), F32)]).reshape(1, LANES)
    x1_rows, bucket, rank, counts = _out_proj(x2, conv_n, attn, beta_attn[i], w_out[i].astype(BF16), g_ffn[i],
                                              w_router, b_router, tm=512)
    moe_tm = 256
    starts, schedule = _moe_schedule(counts[:, 0].astype(jnp.int32), t_tokens // moe_tm, moe_tm)
    pos = _positions(bucket, rank, starts, tm=2048)
    x1_sorted = _dispatch(pos, x1_rows, tm=1024)
    y_sorted = _moe(schedule, x1_sorted, g_ffn[i], w_router, b_router, w_gate[i].astype(BF16),
                    w_up[i].astype(BF16), w_down[i].astype(BF16), moe_tm)
    out = _ple_final(pos, x1_rows, y_sorted, p[i].reshape(t_tokens, PLE_DIM), g_ple[i],
                     w_ple_gate[i].astype(BF16), b_ple_gate[i], w_ple_proj[i].astype(BF16), g_final, tm=512)
    return out.reshape(batch, seq, D_MODEL)
```

```python
import jax
import jax.numpy as jnp
from jax import lax
from jax.experimental import pallas as pl
from jax.experimental.pallas import tpu as pltpu

F32 = jnp.float32
BF16 = jnp.bfloat16

D_MODEL = 1024
CONV_CH = 512
CONV_WIDTH = 31
ATTN_WIDTH = 512
HEAD_DIM = 64
N_HEADS = 8
ROT_DIM = 16
ROPE_THETA = 500000.0
MOBA_BLOCK = 256
MOBA_TOPK = 3
N_GROUPS = 4
EXPERTS_PER_GROUP = 8
N_EXPERTS = 32
D_EXPERT = 256
PLE_DIM = 256
N_BUCKETS = N_GROUPS * EXPERTS_PER_GROUP * EXPERTS_PER_GROUP
N_PAIR_BUCKETS = N_GROUPS * EXPERTS_PER_GROUP * (EXPERTS_PER_GROUP - 1) // 2
IN_COLS = 2 * CONV_CH + 3 * ATTN_WIDTH
EPS = 1e-6

LANES = 128
SUBLANES = 8
HEAD_PAD = LANES
QKV_PAD = N_HEADS * HEAD_PAD
ROW_CHUNKS = D_MODEL // LANES
assert ROW_CHUNKS == SUBLANES
MASK_NEG = -1e30
Q_SCALE = HEAD_DIM ** -0.5 * 1.4426950408889634
CONV_HALO = 32
CONV_ROWS = 32
DMA_GROUP = 8
PAD_BITS = 8
ROUTER_EXPERT_COL = SUBLANES
VMEM_LIMIT = 56 * 1024 * 1024


def _rms(x, g):
    return x * lax.rsqrt(jnp.mean(x * x, axis=-1, keepdims=True) + EPS) * g


def _load_row_tiles(ref, n_rows):
    return jnp.concatenate([ref[pl.ds(c, n_rows, stride=ROW_CHUNKS), :] for c in range(ROW_CHUNKS)], axis=1)


def _store_row_tiles(ref, val, accumulate=False):
    n_rows = val.shape[0]
    for c in range(ROW_CHUNKS):
        idx = (pl.ds(c, n_rows, stride=ROW_CHUNKS), slice(None))
        piece = val[:, c * LANES:(c + 1) * LANES]
        ref[idx] = ref[idx] + piece if accumulate else piece


def _router_logits(h_bf16, wr_ref, br_ref):
    return jnp.dot(h_bf16, wr_ref[...], preferred_element_type=F32) + br_ref[...]


def _in_proj_kernel(x_ref, g_ref, w_ref, rc_ref, ra_ref, rb_ref, u_ref, q_ref, k_ref, v_ref):
    h = _rms(x_ref[...], g_ref[...]).astype(BF16)
    proj = jnp.dot(h, w_ref[...], preferred_element_type=F32)
    u_ref[...] = proj[:, :CONV_CH] * jax.nn.sigmoid(proj[:, CONV_CH:2 * CONV_CH])
    rc, ra, rb = rc_ref[...], ra_ref[...], rb_ref[...]
    low = lax.broadcasted_iota(jnp.int32, (x_ref.shape[0], LANES), 1) < HEAD_DIM

    def put(dst_ref, base, rope, scale, spare):
        for c in range(ATTN_WIDTH // LANES):
            t = proj[:, base + c * LANES: base + (c + 1) * LANES]
            if rope:
                t = t * rc + pltpu.roll(t, LANES - ROT_DIM // 2, 1) * ra + pltpu.roll(t, ROT_DIM // 2, 1) * rb
            if scale != 1.0:
                t = t * scale
            even = jnp.where(low, t, spare)
            odd = jnp.where(low, pltpu.roll(t, HEAD_DIM, 1), spare)
            dst_ref[:, (2 * c) * HEAD_PAD:(2 * c + 1) * HEAD_PAD] = even.astype(dst_ref.dtype)
            dst_ref[:, (2 * c + 1) * HEAD_PAD:(2 * c + 2) * HEAD_PAD] = odd.astype(dst_ref.dtype)

    ones_lane = jnp.where(lax.broadcasted_iota(jnp.int32, (x_ref.shape[0], LANES), 1) == HEAD_DIM, 1.0, 0.0)
    put(q_ref, 2 * CONV_CH, True, Q_SCALE, 0.0)
    put(k_ref, 2 * CONV_CH + ATTN_WIDTH, True, 1.0, 0.0)
    put(v_ref, 2 * CONV_CH + 2 * ATTN_WIDTH, False, 1.0, ones_lane)


def _rope_coeff_tables(seq):
    pos = jnp.arange(seq, dtype=F32)
    inv_freq = jnp.power(jnp.float32(ROPE_THETA), -jnp.arange(0, ROT_DIM, 2, dtype=F32) / ROT_DIM)
    ang = pos[:, None] * inv_freq[None, :]
    cos, sin = jnp.cos(ang), jnp.sin(ang)
    zeros = jnp.zeros((seq, HEAD_DIM - ROT_DIM), F32)
    c_head = jnp.concatenate([cos, cos, jnp.ones_like(zeros)], axis=1)
    a_head = jnp.concatenate([-sin, jnp.zeros_like(sin), zeros], axis=1)
    b_head = jnp.concatenate([jnp.zeros_like(sin), sin, zeros], axis=1)
    rep = LANES // HEAD_DIM
    return jnp.tile(c_head, (1, rep)), jnp.tile(a_head, (1, rep)), jnp.tile(b_head, (1, rep))


def _in_proj(x2, g_mix, w_in_bf, seq, tm):
    t_tokens = x2.shape[0]
    rc, ra, rb = _rope_coeff_tables(seq)
    n_seq_tiles = seq // tm
    row = lambda i: (i, 0)
    fixed = lambda i: (0, 0)
    tab = lambda i: (i % n_seq_tiles, 0)
    return pl.pallas_call(
        _in_proj_kernel,
        grid=(t_tokens // tm,),
        in_specs=[pl.BlockSpec((tm, D_MODEL), row), pl.BlockSpec((1, D_MODEL), fixed),
                  pl.BlockSpec((D_MODEL, IN_COLS), fixed),
                  pl.BlockSpec((tm, LANES), tab), pl.BlockSpec((tm, LANES), tab), pl.BlockSpec((tm, LANES), tab)],
        out_specs=[pl.BlockSpec((tm, CONV_CH), row), pl.BlockSpec((tm, QKV_PAD), row),
                   pl.BlockSpec((tm, QKV_PAD), row), pl.BlockSpec((tm, QKV_PAD), row)],
        out_shape=[jax.ShapeDtypeStruct((t_tokens, CONV_CH), F32),
                   jax.ShapeDtypeStruct((t_tokens, QKV_PAD), BF16),
                   jax.ShapeDtypeStruct((t_tokens, QKV_PAD), BF16),
                   jax.ShapeDtypeStruct((t_tokens, QKV_PAD), BF16)],
        compiler_params=pltpu.CompilerParams(dimension_semantics=("arbitrary",), vmem_limit_bytes=VMEM_LIMIT),
        name="in_proj",
    )(x2, g_mix.reshape(1, D_MODEL), w_in_bf, rc, ra, rb)


def _conv_begin(u_ref, pad_ref, shift_ref, first_tile):
    ts = u_ref.shape[0]

    @pl.when(first_tile)
    def _():
        pad_ref[0:CONV_HALO, :] = jnp.zeros((CONV_HALO, CONV_CH), F32)

    pad_ref[CONV_HALO:CONV_HALO + ts, :] = u_ref[...]
    kept = ts + CONV_HALO - SUBLANES
    for sh in range(1, SUBLANES):
        shift_ref[sh - 1, 0:kept, :] = pad_ref[sh:sh + kept, :]


def _conv_rows(r0, w, pad_ref, shift_ref, b_ref, lg_ref, lb_ref, beta_ref, o_ref):
    first = CONV_HALO - (CONV_WIDTH - 1)

    def window(start):
        sh, base = start % SUBLANES, start - start % SUBLANES
        if sh == 0:
            return pad_ref[base:base + CONV_ROWS, :]
        return shift_ref[sh - 1, base:base + CONV_ROWS, :]

    acc = jnp.zeros((CONV_ROWS, CONV_CH), F32)
    for tap in range(CONV_WIDTH):
        acc = acc + window(r0 + first + tap) * w[tap:tap + 1, :]
    y = acc + b_ref[...]
    mu = jnp.mean(y, axis=-1, keepdims=True)
    d = y - mu
    var = jnp.mean(d * d, axis=-1, keepdims=True)
    y = d * lax.rsqrt(var + EPS) * lg_ref[...] + lb_ref[...]
    y = y * jax.nn.sigmoid(y)
    o_ref[r0:r0 + CONV_ROWS, :] = _rms(y, beta_ref[...]).astype(o_ref.dtype)


def _conv_end(pad_ref, ts):
    pad_ref[0:CONV_HALO, :] = pad_ref[ts:ts + CONV_HALO, :]


def _conv_kernel(u_ref, w_ref, b_ref, lg_ref, lb_ref, beta_ref, o_ref, pad_ref, shift_ref):
    ts = u_ref.shape[0]
    _conv_begin(u_ref, pad_ref, shift_ref, pl.program_id(1) == 0)
    w = w_ref[...]
    for r0 in range(0, ts, CONV_ROWS):
        _conv_rows(r0, w, pad_ref, shift_ref, b_ref, lg_ref, lb_ref, beta_ref, o_ref)
    _conv_end(pad_ref, ts)


def _conv(u2, w_dw, b_dw, ln_g, ln_b, beta, batch, seq, ts):
    n_s = seq // ts
    row = lambda b, s: (b * n_s + s, 0)
    fixed = lambda b, s: (0, 0)
    vec = pl.BlockSpec((1, CONV_CH), fixed)
    return pl.pallas_call(
        _conv_kernel,
        grid=(batch, n_s),
        in_specs=[pl.BlockSpec((ts, CONV_CH), row), pl.BlockSpec((CONV_WIDTH, CONV_CH), fixed), vec, vec, vec, vec],
        out_specs=pl.BlockSpec((ts, CONV_CH), row),
        out_shape=jax.ShapeDtypeStruct((batch * seq, CONV_CH), BF16),
        scratch_shapes=[pltpu.VMEM((ts + CONV_HALO, CONV_CH), F32),
                        pltpu.VMEM((SUBLANES - 1, ts + CONV_HALO, CONV_CH), F32)],
        compiler_params=pltpu.CompilerParams(dimension_semantics=("arbitrary", "arbitrary"),
                                             vmem_limit_bytes=VMEM_LIMIT),
        name="conformer_conv",
    )(u2, w_dw, b_dw.reshape(1, -1), ln_g.reshape(1, -1), ln_b.reshape(1, -1), beta.reshape(1, -1))


def _moba_kernel(q_ref, k_ref, v_ref, blk_ref, place_ref, o_ref):
    seq = q_ref.shape[0]
    nb = seq // MOBA_BLOCK
    blk_onehot = blk_ref[...]
    place = place_ref[...]
    sub = lax.broadcasted_iota(jnp.int32, (nb, seq), 0)
    q_blk = lax.broadcasted_iota(jnp.int32, (nb, seq), 1) // MOBA_BLOCK
    past = sub < q_blk
    r_io = lax.broadcasted_iota(jnp.int32, (MOBA_BLOCK, MOBA_BLOCK), 0)
    c_io = lax.broadcasted_iota(jnp.int32, (MOBA_BLOCK, MOBA_BLOCK), 1)
    causal = c_io <= r_io
    low = lax.broadcasted_iota(jnp.int32, (MOBA_BLOCK, LANES), 1) < HEAD_DIM
    contract_lanes = (((1,), (1,)), ((), ()))
    q_aug, k_aug = [], []
    for hh in range(2):
        lo, hi = hh * HEAD_PAD, (hh + 1) * HEAD_PAD
        k = k_ref[:, lo:hi]
        q = q_ref[:, lo:hi]
        kf = k.astype(F32)
        k_mean = jnp.concatenate(
            [jnp.mean(kf[j * MOBA_BLOCK:(j + 1) * MOBA_BLOCK], axis=0, keepdims=True) for j in range(nb)], axis=0)
        km_hi = k_mean.astype(BF16)
        km_lo = (k_mean - km_hi.astype(F32)).astype(BF16)
        gate = (lax.dot_general(km_hi, q, contract_lanes, preferred_element_type=F32)
                + lax.dot_general(km_lo, q, contract_lanes, preferred_element_type=F32))
        bias_t = jnp.where(sub == q_blk, 0.0, MASK_NEG)
        for j in range(nb - 1):
            gj = gate[j:j + 1, :]
            beats = ((gate > gj) | ((gate == gj) & (sub < j))) & past
            cnt = jnp.sum(beats.astype(F32), axis=0, keepdims=True)
            bias_t = jnp.where((sub == j) & past & (cnt < float(MOBA_TOPK)), 0.0, bias_t)
        bias = jnp.dot(bias_t.T.astype(BF16), place, preferred_element_type=F32)
        q_aug.append(q + bias.astype(BF16))
        k_aug.append(k + blk_onehot)

    def scores(hh, i):
        return lax.dot_general(q_aug[hh][i * MOBA_BLOCK:(i + 1) * MOBA_BLOCK], k_aug[hh][:(i + 1) * MOBA_BLOCK],
                               contract_lanes, preferred_element_type=F32)

    def attend(s, hh, i):
        v_lo = hh * HEAD_PAD
        s_own = jnp.where(causal, s[:, i * MOBA_BLOCK:], MASK_NEG)
        m = jnp.max(s_own, axis=-1, keepdims=True)
        if i > 0:
            s_past = s[:, :i * MOBA_BLOCK]
            m = jnp.maximum(m, jnp.max(s_past, axis=-1, keepdims=True))
        acc = jnp.dot(jnp.exp2((s_own - m).astype(BF16)),
                      v_ref[i * MOBA_BLOCK:(i + 1) * MOBA_BLOCK, v_lo:v_lo + HEAD_PAD], preferred_element_type=F32)
        if i > 0:
            acc = acc + jnp.dot(jnp.exp2((s_past - m).astype(BF16)), v_ref[:i * MOBA_BLOCK, v_lo:v_lo + HEAD_PAD],
                                preferred_element_type=F32)
        return jnp.where(low, acc / acc[:, HEAD_DIM:HEAD_DIM + 1], 0.0)

    items = [(hh, i) for i in range(nb) for hh in range(2)]
    s_cur = scores(*items[0])
    even = None
    for n, (hh, i) in enumerate(items):
        s_next = scores(*items[n + 1]) if n + 1 < len(items) else None
        out = attend(s_cur, hh, i)
        if hh == 0:
            even = out
        else:
            o_ref[i * MOBA_BLOCK:(i + 1) * MOBA_BLOCK, :] = even + pltpu.roll(out, HEAD_DIM, 1)
        s_cur = s_next


def _moba(qp, kp, vp, batch, seq):
    nb = seq // MOBA_BLOCK
    lane = jnp.arange(LANES)[None, :]
    blk = (jnp.arange(seq) // MOBA_BLOCK)[:, None]
    blk_onehot = (lane == HEAD_DIM + blk).astype(BF16)
    place = (lane == HEAD_DIM + jnp.arange(nb)[:, None]).astype(BF16)
    pair = lambda b, c: (b, c)
    fixed = lambda b, c: (0, 0)
    spec = pl.BlockSpec((seq, 2 * HEAD_PAD), pair)
    return pl.pallas_call(
        _moba_kernel,
        grid=(batch, N_HEADS // 2),
        in_specs=[spec, spec, spec, pl.BlockSpec((seq, LANES), fixed), pl.BlockSpec((nb, LANES), fixed)],
        out_specs=pl.BlockSpec((seq, LANES), pair),
        out_shape=jax.ShapeDtypeStruct((batch * seq, ATTN_WIDTH), F32),
        compiler_params=pltpu.CompilerParams(dimension_semantics=("arbitrary", "arbitrary"),
                                             vmem_limit_bytes=VMEM_LIMIT),
        name="moba_attention",
    )(qp, kp, vp, blk_onehot, place)


def _out_proj_kernel(x_ref, c_ref, a_ref, beta_ref, wc_ref, wa_ref, gf_ref, wr_ref, br_ref, tri_ref,
                     x1_ref, bucket_ref, rank_ref, cnt_ref, carry_ref):
    @pl.when(pl.program_id(0) == 0)
    def _():
        carry_ref[...] = jnp.zeros_like(carry_ref)

    an = _rms(a_ref[...], beta_ref[...]).astype(BF16)
    x1 = (x_ref[...] + jnp.dot(c_ref[...], wc_ref[...], preferred_element_type=F32)
          + jnp.dot(an, wa_ref[...], preferred_element_type=F32))
    _store_row_tiles(x1_ref, x1)
    logits = _router_logits(_rms(x1, gf_ref[...]).astype(BF16), wr_ref, br_ref)
    tm = logits.shape[0]
    lt = logits.T
    sub = lax.broadcasted_iota(jnp.int32, (SUBLANES, tm), 0)
    gl = jnp.where(sub < N_GROUPS, lt[0:SUBLANES, :], -jnp.inf)
    g_idx = jnp.min(jnp.where(gl == jnp.max(gl, axis=0, keepdims=True), sub, N_GROUPS), axis=0, keepdims=True)
    el = jnp.zeros((EXPERTS_PER_GROUP, tm), F32)
    for g in range(N_GROUPS):
        lo = ROUTER_EXPERT_COL + g * EXPERTS_PER_GROUP
        el = jnp.where(g_idx == g, lt[lo:lo + EXPERTS_PER_GROUP, :], el)
    i1 = jnp.min(jnp.where(el == jnp.max(el, axis=0, keepdims=True), sub, EXPERTS_PER_GROUP), axis=0, keepdims=True)
    el2 = jnp.where(sub == i1, -jnp.inf, el)
    i2 = jnp.min(jnp.where(el2 == jnp.max(el2, axis=0, keepdims=True), sub, EXPERTS_PER_GROUP), axis=0,
                 keepdims=True)
    bucket = (g_idx * (EXPERTS_PER_GROUP * EXPERTS_PER_GROUP) + jnp.minimum(i1, i2) * EXPERTS_PER_GROUP
              + jnp.maximum(i1, i2))
    onehot = (lax.broadcasted_iota(jnp.int32, (N_BUCKETS, tm), 0) == bucket)
    onehot_bf = onehot.astype(BF16)
    before = jnp.dot(onehot_bf, tri_ref[...], preferred_element_type=F32)
    counts = jnp.dot(onehot_bf, jnp.ones((tm, LANES), BF16), preferred_element_type=F32)
    carry = carry_ref[...]
    carry_wide = jnp.concatenate([carry] * (tm // LANES), axis=1)
    rank = jnp.sum(jnp.where(onehot, before + carry_wide, 0.0), axis=0, keepdims=True)
    carry_ref[...] = carry + counts
    cnt_ref[...] = carry + counts
    bucket_ref[0] = bucket
    rank_ref[0] = rank.astype(jnp.int32)


def _out_proj(x2, conv_n, attn, beta_attn, w_out_bf, g_ffn, w_router_bf, b_router, tm):
    t_tokens = x2.shape[0]
    row = lambda i: (i, 0)
    fixed = lambda i: (0, 0)
    tri = (jnp.arange(tm)[:, None] < jnp.arange(tm)[None, :]).astype(BF16)
    lane_row = pl.BlockSpec((1, 1, tm), lambda i: (i, 0, 0))
    return pl.pallas_call(
        _out_proj_kernel,
        grid=(t_tokens // tm,),
        in_specs=[pl.BlockSpec((tm, D_MODEL), row), pl.BlockSpec((tm, CONV_CH), row),
                  pl.BlockSpec((tm, ATTN_WIDTH), row), pl.BlockSpec((1, ATTN_WIDTH), fixed),
                  pl.BlockSpec((CONV_CH, D_MODEL), fixed), pl.BlockSpec((ATTN_WIDTH, D_MODEL), fixed),
                  pl.BlockSpec((1, D_MODEL), fixed), pl.BlockSpec((D_MODEL, LANES), fixed),
                  pl.BlockSpec((1, LANES), fixed), pl.BlockSpec((tm, tm), fixed)],
        out_specs=[pl.BlockSpec((tm * ROW_CHUNKS, LANES), row), lane_row, lane_row,
                   pl.BlockSpec((N_BUCKETS, LANES), fixed)],
        out_shape=[jax.ShapeDtypeStruct((t_tokens * ROW_CHUNKS, LANES), F32),
                   jax.ShapeDtypeStruct((t_tokens // tm, 1, tm), jnp.int32),
                   jax.ShapeDtypeStruct((t_tokens // tm, 1, tm), jnp.int32),
                   jax.ShapeDtypeStruct((N_BUCKETS, LANES), F32)],
        scratch_shapes=[pltpu.VMEM((N_BUCKETS, LANES), F32)],
        compiler_params=pltpu.CompilerParams(dimension_semantics=("arbitrary",), vmem_limit_bytes=VMEM_LIMIT),
        name="out_proj_router",
    )(x2, conv_n, attn, beta_attn.reshape(1, -1), w_out_bf[:CONV_CH], w_out_bf[CONV_CH:],
      g_ffn.reshape(1, -1), w_router_bf, b_router, tri)


def _positions_kernel(bucket_ref, rank_ref, starts_ref, pos_ref):
    bucket = bucket_ref[0]
    ids = lax.broadcasted_iota(jnp.int32, (N_BUCKETS, bucket.shape[1]), 0)
    pos_ref[0] = jnp.sum(jnp.where(ids == bucket, starts_ref[...], 0), axis=0, keepdims=True) + rank_ref[0]


def _positions(bucket, rank, starts, tm):
    t_tokens = bucket.size
    lane_row = pl.BlockSpec((1, 1, tm), lambda i: (i, 0, 0))
    shape = (t_tokens // tm, 1, tm)
    pos = pl.pallas_call(
        _positions_kernel,
        grid=(t_tokens // tm,),
        in_specs=[lane_row, lane_row, pl.BlockSpec((N_BUCKETS, 1), lambda i: (0, 0))],
        out_specs=lane_row,
        out_shape=jax.ShapeDtypeStruct(shape, jnp.int32),
        compiler_params=pltpu.CompilerParams(dimension_semantics=("arbitrary",), vmem_limit_bytes=VMEM_LIMIT),
        name="moe_positions",
    )(bucket.reshape(shape), rank.reshape(shape), starts.reshape(N_BUCKETS, 1))
    return pos.reshape(t_tokens)


def _row_tile(ref, r):
    return ref.at[pl.ds(pl.multiple_of(r * ROW_CHUNKS, ROW_CHUNKS), ROW_CHUNKS), :]


def _dispatch_kernel(pos_ref, pad_start_ref, pad_len_ref, total_ref, x_ref, out_hbm, zero_ref, sem, pad_sem):
    tm = x_ref.shape[0] // ROW_CHUNKS
    base = pl.program_id(0) * tm
    first = pl.program_id(0) == 0
    half_tile = zero_ref.shape[0] // ROW_CHUNKS
    n_tiles = out_hbm.shape[0] // (2 * zero_ref.shape[0])

    def zero_copy(row, rows, wait):
        cp = pltpu.make_async_copy(
            zero_ref.at[pl.ds(0, rows * ROW_CHUNKS), :],
            out_hbm.at[pl.ds(pl.multiple_of(row * ROW_CHUNKS, ROW_CHUNKS), rows * ROW_CHUNKS), :], pad_sem)
        cp.wait() if wait else cp.start()

    def fill(wait):
        def pad(b, carry):
            off, n = pad_start_ref[b], pad_len_ref[b]
            for bit in range(PAD_BITS):
                take = ((n >> bit) & 1) == 1
                pl.when(take)(lambda off=off, bit=bit: zero_copy(off, 1 << bit, wait))
                off = off + jnp.where(take, 1 << bit, 0)
            return carry

        def unused_tile(t, carry):
            zero_copy(t * 2 * half_tile, half_tile, wait)
            zero_copy(t * 2 * half_tile + half_tile, half_tile, wait)
            return carry

        lax.fori_loop(0, N_BUCKETS, pad, 0)
        lax.fori_loop(total_ref[0], n_tiles, unused_tile, 0)

    @pl.when(first)
    def _():
        zero_ref[...] = jnp.zeros_like(zero_ref)
        fill(wait=False)

    def row_copy(r, dst):
        return pltpu.make_async_copy(_row_tile(x_ref, r), _row_tile(out_hbm, dst), sem)

    def issue(g, carry):
        for u in range(DMA_GROUP):
            r = g * DMA_GROUP + u
            row_copy(r, pos_ref[base + r]).start(priority=u % 2)
        return carry

    lax.fori_loop(0, tm // DMA_GROUP, issue, 0)

    def drain(r, carry):
        row_copy(0, 0).wait()
        return carry

    lax.fori_loop(0, tm, drain, 0, unroll=8)

    @pl.when(first)
    def _():
        fill(wait=True)


def _dispatch(pos, pad_start, pad_len, total, x_rows, sorted_rows, tm):
    n_rows = x_rows.shape[0]
    return pl.pallas_call(
        _dispatch_kernel,
        grid_spec=pltpu.PrefetchScalarGridSpec(
            num_scalar_prefetch=4,
            grid=(n_rows // (tm * ROW_CHUNKS),),
            in_specs=[pl.BlockSpec((tm * ROW_CHUNKS, LANES), lambda i, *_: (i, 0))],
            out_specs=pl.BlockSpec(memory_space=pl.ANY),
            scratch_shapes=[pltpu.VMEM(((1 << (PAD_BITS - 1)) * ROW_CHUNKS, LANES), F32),
                            pltpu.SemaphoreType.DMA(()), pltpu.SemaphoreType.DMA(())],
        ),
        out_shape=jax.ShapeDtypeStruct((sorted_rows * ROW_CHUNKS, LANES), F32),
        compiler_params=pltpu.CompilerParams(dimension_semantics=("arbitrary",), vmem_limit_bytes=VMEM_LIMIT,
                                             has_side_effects=True),
        name="moe_dispatch",
    )(pos, pad_start, pad_len, total, x_rows)


def _moe_schedule(counts, n_steps, tm):
    tiles_b = (counts + tm - 1) // tm
    step_end = jnp.cumsum(tiles_b)
    step_start = step_end - tiles_b
    starts = step_start * tm
    total = step_end[-1]
    s = jnp.minimum(jnp.arange(n_steps, dtype=jnp.int32), total - 1)
    mine = (s[:, None] >= step_start[None, :]) & (s[:, None] < step_end[None, :])
    pick = lambda v: jnp.sum(jnp.where(mine, v[None, :], 0), axis=1).astype(jnp.int32)
    ids = jnp.arange(N_BUCKETS, dtype=jnp.int32)
    pair = EXPERTS_PER_GROUP * EXPERTS_PER_GROUP
    valid = jnp.clip(pick(counts) - (s - pick(step_start)) * tm, 0, tm).astype(jnp.int32)
    schedule = (s, pick(ids // pair), pick((ids % pair) // EXPERTS_PER_GROUP), pick(ids % EXPERTS_PER_GROUP),
                valid, total.astype(jnp.int32).reshape(1))
    return starts, (starts + counts).astype(jnp.int32), (tiles_b * tm - counts).astype(jnp.int32), schedule


def _moe_kernel(tile_ref, grp_ref, elo_ref, ehi_ref, valid_ref, total_ref,
                x_ref, gf_ref, wr_ref, br_ref, wgl_ref, wul_ref, wdl_ref, wgh_ref, wuh_ref, wdh_ref, y_ref):
    s = pl.program_id(0)
    tm = x_ref.shape[0] // ROW_CHUNKS

    @pl.when(s < total_ref[0])
    def _():
        inside = lax.broadcasted_iota(jnp.int32, (tm, 1), 0) < valid_ref[s]
        x = _load_row_tiles(x_ref, tm)
        hg = (x * gf_ref[...]).astype(BF16)
        inv_rms = lax.rsqrt(jnp.mean(x * x, axis=-1, keepdims=True) + EPS)
        logits = inv_rms * jnp.dot(hg, wr_ref[...], preferred_element_type=F32) + br_ref[...]
        lane = lax.broadcasted_iota(jnp.int32, (tm, LANES), 1)
        col = ROUTER_EXPERT_COL + grp_ref[s] * EXPERTS_PER_GROUP
        lane_val = lambda idx: jnp.sum(jnp.where(lane == idx, logits, 0.0), axis=-1, keepdims=True)
        gl = jnp.where(lane < N_GROUPS, logits, -jnp.inf)
        g_w = 1.0 / jnp.sum(jnp.exp(gl - jnp.max(gl, axis=-1, keepdims=True)), axis=-1, keepdims=True)
        v_lo, v_hi = lane_val(col + elo_ref[s]), lane_val(col + ehi_ref[s])
        v_max = jnp.maximum(v_lo, v_hi)
        p_lo, p_hi = jnp.exp(v_lo - v_max), jnp.exp(v_hi - v_max)
        scale = jnp.where(inside, inv_rms * g_w / (p_lo + p_hi), 0.0)

        def expert(wg_ref, wu_ref, wd_ref, w_row):
            a = inv_rms * jnp.dot(hg, wg_ref[0], preferred_element_type=F32)
            b = jnp.dot(hg, wu_ref[0], preferred_element_type=F32)
            hid = a * jax.nn.sigmoid(a) * b * w_row
            return jnp.dot(hid.astype(BF16), wd_ref[0], preferred_element_type=F32)

        y = expert(wgl_ref, wul_ref, wdl_ref, p_lo * scale) + expert(wgh_ref, wuh_ref, wdh_ref, p_hi * scale)
        _store_row_tiles(y_ref, y)

    @pl.when(s >= total_ref[0])
    def _():
        y_ref[...] = jnp.zeros_like(y_ref)


def _moe(schedule, x_sorted, g_ffn, w_router_bf, b_router, wg_bf, wu_bf, wd_bf, tm):
    n_rows = x_sorted.shape[0]
    n_steps = schedule[0].shape[0]
    row = lambda s, tile, *_: (tile[s], 0)
    fixed = lambda s, *_: (0, 0)
    w_lo = lambda s, tile, grp, elo, ehi, *_: (grp[s] * EXPERTS_PER_GROUP + elo[s], 0, 0)
    w_hi = lambda s, tile, grp, elo, ehi, *_: (grp[s] * EXPERTS_PER_GROUP + ehi[s], 0, 0)
    up_spec = lambda sel: pl.BlockSpec((1, D_MODEL, D_EXPERT), sel)
    down_spec = lambda sel: pl.BlockSpec((1, D_EXPERT, D_MODEL), sel)
    return pl.pallas_call(
        _moe_kernel,
        grid_spec=pltpu.PrefetchScalarGridSpec(
            num_scalar_prefetch=6,
            grid=(n_steps,),
            in_specs=[pl.BlockSpec((tm * ROW_CHUNKS, LANES), row), pl.BlockSpec((1, D_MODEL), fixed),
                      pl.BlockSpec((D_MODEL, LANES), fixed), pl.BlockSpec((1, LANES), fixed),
                      up_spec(w_lo), up_spec(w_lo), down_spec(w_lo), up_spec(w_hi), up_spec(w_hi), down_spec(w_hi)],
            out_specs=pl.BlockSpec((tm * ROW_CHUNKS, LANES), lambda s, *_: (s, 0)),
        ),
        out_shape=jax.ShapeDtypeStruct((n_rows, LANES), F32),
        compiler_params=pltpu.CompilerParams(dimension_semantics=("arbitrary",), vmem_limit_bytes=VMEM_LIMIT),
        name="hier_moe",
    )(*schedule, x_sorted, g_ffn.reshape(1, -1), w_router_bf, b_router, wg_bf, wu_bf, wd_bf, wg_bf, wu_bf, wd_bf)


def _ple_final_kernel(pos_ref, x1_ref, ys_hbm, p_ref, gp_ref, wg_ref, bg_ref, wp_ref, gfin_ref, o_ref,
                      ybuf, sems):
    i = pl.program_id(0)
    n = pl.num_programs(0)
    tm = o_ref.shape[0]

    def row_copy(src, r, slot):
        return pltpu.make_async_copy(_row_tile(ys_hbm, src), _row_tile(ybuf.at[slot], r), sems.at[slot])

    def gather(tile, slot):
        def issue(g, carry):
            for u in range(DMA_GROUP):
                r = g * DMA_GROUP + u
                row_copy(pos_ref[tile * tm + r], r, slot).start(priority=u % 2)
            return carry
        lax.fori_loop(0, tm // DMA_GROUP, issue, 0)

    @pl.when(i == 0)
    def _():
        gather(0, 0)

    slot = i % 2

    def drain(r, carry):
        row_copy(0, r, slot).wait()
        return carry

    def step(prefetch_next):
        lax.fori_loop(0, tm, drain, 0, unroll=8)
        x2 = _load_row_tiles(x1_ref, tm) + _load_row_tiles(ybuf.at[slot], tm)
        if prefetch_next:
            for r in range(tm):
                row_copy(pos_ref[(i + 1) * tm + r], r, 1 - slot).start(priority=r % 2)
        hg = _rms(x2, gp_ref[...]).astype(BF16)
        gate = jax.nn.sigmoid(jnp.dot(hg, wg_ref[...], preferred_element_type=F32) + bg_ref[...])
        emb = jnp.dot(p_ref[...].astype(BF16), wp_ref[...], preferred_element_type=F32)
        x3 = x2 + emb * gate
        o_ref[...] = _rms(x3, gfin_ref[...])

    @pl.when(i + 1 < n)
    def _():
        step(True)

    @pl.when(i + 1 == n)
    def _():
        step(False)


def _ple_final(pos, x1_rows, y_sorted, p2, g_ple, w_gate_bf, b_gate, w_proj_bf, g_final, tm):
    t_tokens = p2.shape[0]
    row = lambda i, pos: (i, 0)
    fixed = lambda i, pos: (0, 0)
    vec = pl.BlockSpec((1, D_MODEL), fixed)
    return pl.pallas_call(
        _ple_final_kernel,
        grid_spec=pltpu.PrefetchScalarGridSpec(
            num_scalar_prefetch=1,
            grid=(t_tokens // tm,),
            in_specs=[pl.BlockSpec((tm * ROW_CHUNKS, LANES), row), pl.BlockSpec(memory_space=pl.ANY),
                      pl.BlockSpec((tm, PLE_DIM), row), vec, pl.BlockSpec((D_MODEL, D_MODEL), fixed), vec,
                      pl.BlockSpec((PLE_DIM, D_MODEL), fixed), vec],
            out_specs=pl.BlockSpec((tm, D_MODEL), row),
            scratch_shapes=[pltpu.VMEM((2, tm * ROW_CHUNKS, LANES), F32), pltpu.SemaphoreType.DMA((2,))],
        ),
        out_shape=jax.ShapeDtypeStruct((t_tokens, D_MODEL), F32),
        compiler_params=pltpu.CompilerParams(dimension_semantics=("arbitrary",), vmem_limit_bytes=VMEM_LIMIT),
        name="ple_final",
    )(pos, x1_rows, y_sorted, p2, g_ple.reshape(1, -1), w_gate_bf, b_gate.reshape(1, -1), w_proj_bf,
      g_final.reshape(1, -1))


def kernel(x, p, g_mix, w_in, w_dw, b_dw, conv_ln_g, conv_ln_b, beta_conv, beta_attn, w_out, g_ffn, w_group, b_group, w_erouter, b_erouter, w_gate, w_up, w_down, g_ple, w_ple_gate, b_ple_gate, w_ple_proj, g_final):
    batch, seq, _ = x.shape
    assert p.shape[0] == 1 and seq % MOBA_BLOCK == 0
    t_tokens = batch * seq
    x2 = x.reshape(t_tokens, D_MODEL)
    i = 0
    u, qp, kp, vp = _in_proj(x2, g_mix[i], w_in[i].astype(BF16), seq, tm=512)
    conv_n = _conv(u, w_dw[i], b_dw[i], conv_ln_g[i], conv_ln_b[i], beta_conv[i], batch, seq, ts=256)
    attn = _moba(qp, kp, vp, batch, seq)
    gap = ROUTER_EXPERT_COL - N_GROUPS
    tail = LANES - ROUTER_EXPERT_COL - N_EXPERTS
    w_router = jnp.concatenate(
        [w_group[i], jnp.zeros((D_MODEL, gap), F32),
         jnp.transpose(w_erouter[i], (1, 0, 2)).reshape(D_MODEL, N_EXPERTS),
         jnp.zeros((D_MODEL, tail), F32)], axis=1).astype(BF16)
    b_router = jnp.concatenate([b_group[i], jnp.zeros((gap,), F32), b_erouter[i].reshape(-1),
                                jnp.zeros((tail,), F32)]).reshape(1, LANES)
    x1_rows, bucket, rank, counts = _out_proj(x2, conv_n, attn, beta_attn[i], w_out[i].astype(BF16), g_ffn[i],
                                              w_router, b_router, tm=512)
    moe_tm = 1 << PAD_BITS
    moe_steps = t_tokens // moe_tm + N_PAIR_BUCKETS
    starts, pad_start, pad_len, schedule = _moe_schedule(counts[:, 0].astype(jnp.int32), moe_steps, moe_tm)
    pos = _positions(bucket, rank, starts, tm=2048)
    x1_sorted = _dispatch(pos, pad_start, pad_len, schedule[-1], x1_rows, moe_steps * moe_tm, tm=1024)
    y_sorted = _moe(schedule, x1_sorted, g_ffn[i], w_router, b_router, w_gate[i].astype(BF16),
                    w_up[i].astype(BF16), w_down[i].astype(BF16), moe_tm)
    out = _ple_final(pos, x1_rows, y_sorted, p[i].reshape(t_tokens, PLE_DIM), g_ple[i],
                     w_ple_gate[i].astype(BF16), b_ple_gate[i], w_ple_proj[i].astype(BF16), g_final, tm=512)
    return out.reshape(batch, seq, D_MODEL)
```

```python
import jax
import jax.numpy as jnp
from jax import lax
from jax.experimental import pallas as pl
from jax.experimental.pallas import tpu as pltpu

F32 = jnp.float32
BF16 = jnp.bfloat16

D_MODEL = 1024
CONV_CH = 512
CONV_WIDTH = 31
ATTN_WIDTH = 512
HEAD_DIM = 64
N_HEADS = 8
ROT_DIM = 16
ROPE_THETA = 500000.0
MOBA_BLOCK = 256
MOBA_TOPK = 3
N_GROUPS = 4
EXPERTS_PER_GROUP = 8
N_EXPERTS = 32
D_EXPERT = 256
PLE_DIM = 256
N_BUCKETS = N_GROUPS * EXPERTS_PER_GROUP * EXPERTS_PER_GROUP
N_PAIR_BUCKETS = N_GROUPS * EXPERTS_PER_GROUP * (EXPERTS_PER_GROUP - 1) // 2
IN_COLS = 2 * CONV_CH + 3 * ATTN_WIDTH
EPS = 1e-6

LANES = 128
SUBLANES = 8
HEAD_PAD = LANES
QKV_PAD = N_HEADS * HEAD_PAD
ROW_CHUNKS = D_MODEL // LANES
assert ROW_CHUNKS == SUBLANES
MASK_NEG = -1e30
Q_SCALE = HEAD_DIM ** -0.5 * 1.4426950408889634
CONV_HALO = 32
CONV_ROWS = 32
DMA_GROUP = 8
PAD_BITS = 8
ROUTER_EXPERT_COL = SUBLANES
VMEM_LIMIT = 56 * 1024 * 1024


def _rms(x, g):
    return x * lax.rsqrt(jnp.mean(x * x, axis=-1, keepdims=True) + EPS) * g


def _load_row_tiles(ref, n_rows):
    return jnp.concatenate([ref[pl.ds(c, n_rows, stride=ROW_CHUNKS), :] for c in range(ROW_CHUNKS)], axis=1)


def _store_row_tiles(ref, val, accumulate=False):
    n_rows = val.shape[0]
    for c in range(ROW_CHUNKS):
        idx = (pl.ds(c, n_rows, stride=ROW_CHUNKS), slice(None))
        piece = val[:, c * LANES:(c + 1) * LANES]
        ref[idx] = ref[idx] + piece if accumulate else piece


def _router_logits(h_bf16, wr_ref, br_ref):
    return jnp.dot(h_bf16, wr_ref[...], preferred_element_type=F32) + br_ref[...]


def _in_proj_kernel(x_ref, g_ref, w_ref, rc_ref, ra_ref, rb_ref, u_ref, q_ref, k_ref, v_ref):
    h = _rms(x_ref[...], g_ref[...]).astype(BF16)
    proj = jnp.dot(h, w_ref[...], preferred_element_type=F32)
    u_ref[...] = proj[:, :CONV_CH] * jax.nn.sigmoid(proj[:, CONV_CH:2 * CONV_CH])
    rc, ra, rb = rc_ref[...], ra_ref[...], rb_ref[...]
    low = lax.broadcasted_iota(jnp.int32, (x_ref.shape[0], LANES), 1) < HEAD_DIM

    def put(dst_ref, base, rope, scale, spare):
        for c in range(ATTN_WIDTH // LANES):
            t = proj[:, base + c * LANES: base + (c + 1) * LANES]
            if rope:
                t = t * rc + pltpu.roll(t, LANES - ROT_DIM // 2, 1) * ra + pltpu.roll(t, ROT_DIM // 2, 1) * rb
            if scale != 1.0:
                t = t * scale
            even = jnp.where(low, t, spare)
            odd = jnp.where(low, pltpu.roll(t, HEAD_DIM, 1), spare)
            dst_ref[:, (2 * c) * HEAD_PAD:(2 * c + 1) * HEAD_PAD] = even.astype(dst_ref.dtype)
            dst_ref[:, (2 * c + 1) * HEAD_PAD:(2 * c + 2) * HEAD_PAD] = odd.astype(dst_ref.dtype)

    ones_lane = jnp.where(lax.broadcasted_iota(jnp.int32, (x_ref.shape[0], LANES), 1) == HEAD_DIM, 1.0, 0.0)
    put(q_ref, 2 * CONV_CH, True, Q_SCALE, 0.0)
    put(k_ref, 2 * CONV_CH + ATTN_WIDTH, True, 1.0, 0.0)
    put(v_ref, 2 * CONV_CH + 2 * ATTN_WIDTH, False, 1.0, ones_lane)


def _rope_coeff_tables(seq):
    pos = jnp.arange(seq, dtype=F32)
    inv_freq = jnp.power(jnp.float32(ROPE_THETA), -jnp.arange(0, ROT_DIM, 2, dtype=F32) / ROT_DIM)
    ang = pos[:, None] * inv_freq[None, :]
    cos, sin = jnp.cos(ang), jnp.sin(ang)
    zeros = jnp.zeros((seq, HEAD_DIM - ROT_DIM), F32)
    c_head = jnp.concatenate([cos, cos, jnp.ones_like(zeros)], axis=1)
    a_head = jnp.concatenate([-sin, jnp.zeros_like(sin), zeros], axis=1)
    b_head = jnp.concatenate([jnp.zeros_like(sin), sin, zeros], axis=1)
    rep = LANES // HEAD_DIM
    return jnp.tile(c_head, (1, rep)), jnp.tile(a_head, (1, rep)), jnp.tile(b_head, (1, rep))


def _in_proj(x2, g_mix, w_in_bf, seq, tm):
    t_tokens = x2.shape[0]
    rc, ra, rb = _rope_coeff_tables(seq)
    n_seq_tiles = seq // tm
    row = lambda i: (i, 0)
    fixed = lambda i: (0, 0)
    tab = lambda i: (i % n_seq_tiles, 0)
    return pl.pallas_call(
        _in_proj_kernel,
        grid=(t_tokens // tm,),
        in_specs=[pl.BlockSpec((tm, D_MODEL), row), pl.BlockSpec((1, D_MODEL), fixed),
                  pl.BlockSpec((D_MODEL, IN_COLS), fixed),
                  pl.BlockSpec((tm, LANES), tab), pl.BlockSpec((tm, LANES), tab), pl.BlockSpec((tm, LANES), tab)],
        out_specs=[pl.BlockSpec((tm, CONV_CH), row), pl.BlockSpec((tm, QKV_PAD), row),
                   pl.BlockSpec((tm, QKV_PAD), row), pl.BlockSpec((tm, QKV_PAD), row)],
        out_shape=[jax.ShapeDtypeStruct((t_tokens, CONV_CH), F32),
                   jax.ShapeDtypeStruct((t_tokens, QKV_PAD), BF16),
                   jax.ShapeDtypeStruct((t_tokens, QKV_PAD), BF16),
                   jax.ShapeDtypeStruct((t_tokens, QKV_PAD), BF16)],
        compiler_params=pltpu.CompilerParams(dimension_semantics=("arbitrary",), vmem_limit_bytes=VMEM_LIMIT),
        name="in_proj",
    )(x2, g_mix.reshape(1, D_MODEL), w_in_bf, rc, ra, rb)


def _conv_begin(u_ref, pad_ref, shift_ref, first_tile):
    ts = u_ref.shape[0]

    @pl.when(first_tile)
    def _():
        pad_ref[0:CONV_HALO, :] = jnp.zeros((CONV_HALO, CONV_CH), F32)

    pad_ref[CONV_HALO:CONV_HALO + ts, :] = u_ref[...]
    kept = ts + CONV_HALO - SUBLANES
    for sh in range(1, SUBLANES):
        shift_ref[sh - 1, 0:kept, :] = pad_ref[sh:sh + kept, :]


def _conv_rows(r0, w, pad_ref, shift_ref, b_ref, lg_ref, lb_ref, beta_ref, o_ref):
    first = CONV_HALO - (CONV_WIDTH - 1)

    def window(start):
        sh, base = start % SUBLANES, start - start % SUBLANES
        if sh == 0:
            return pad_ref[base:base + CONV_ROWS, :]
        return shift_ref[sh - 1, base:base + CONV_ROWS, :]

    acc = jnp.zeros((CONV_ROWS, CONV_CH), F32)
    for tap in range(CONV_WIDTH):
        acc = acc + window(r0 + first + tap) * w[tap:tap + 1, :]
    y = acc + b_ref[...]
    mu = jnp.mean(y, axis=-1, keepdims=True)
    d = y - mu
    var = jnp.mean(d * d, axis=-1, keepdims=True)
    y = d * lax.rsqrt(var + EPS) * lg_ref[...] + lb_ref[...]
    y = y * jax.nn.sigmoid(y)
    o_ref[r0:r0 + CONV_ROWS, :] = _rms(y, beta_ref[...]).astype(o_ref.dtype)


def _conv_end(pad_ref, ts):
    pad_ref[0:CONV_HALO, :] = pad_ref[ts:ts + CONV_HALO, :]


def _conv_kernel(u_ref, w_ref, b_ref, lg_ref, lb_ref, beta_ref, o_ref, pad_ref, shift_ref):
    ts = u_ref.shape[0]
    _conv_begin(u_ref, pad_ref, shift_ref, pl.program_id(1) == 0)
    w = w_ref[...]
    for r0 in range(0, ts, CONV_ROWS):
        _conv_rows(r0, w, pad_ref, shift_ref, b_ref, lg_ref, lb_ref, beta_ref, o_ref)
    _conv_end(pad_ref, ts)


def _conv(u2, w_dw, b_dw, ln_g, ln_b, beta, batch, seq, ts):
    n_s = seq // ts
    row = lambda b, s: (b * n_s + s, 0)
    fixed = lambda b, s: (0, 0)
    vec = pl.BlockSpec((1, CONV_CH), fixed)
    return pl.pallas_call(
        _conv_kernel,
        grid=(batch, n_s),
        in_specs=[pl.BlockSpec((ts, CONV_CH), row), pl.BlockSpec((CONV_WIDTH, CONV_CH), fixed), vec, vec, vec, vec],
        out_specs=pl.BlockSpec((ts, CONV_CH), row),
        out_shape=jax.ShapeDtypeStruct((batch * seq, CONV_CH), BF16),
        scratch_shapes=[pltpu.VMEM((ts + CONV_HALO, CONV_CH), F32),
                        pltpu.VMEM((SUBLANES - 1, ts + CONV_HALO, CONV_CH), F32)],
        compiler_params=pltpu.CompilerParams(dimension_semantics=("arbitrary", "arbitrary"),
                                             vmem_limit_bytes=VMEM_LIMIT),
        name="conformer_conv",
    )(u2, w_dw, b_dw.reshape(1, -1), ln_g.reshape(1, -1), ln_b.reshape(1, -1), beta.reshape(1, -1))


def _moba_kernel(q_ref, k_ref, v_ref, blk_ref, place_ref, o_ref):
    seq = q_ref.shape[0]
    nb = seq // MOBA_BLOCK
    blk_onehot = blk_ref[...]
    place = place_ref[...]
    sub = lax.broadcasted_iota(jnp.int32, (nb, seq), 0)
    q_blk = lax.broadcasted_iota(jnp.int32, (nb, seq), 1) // MOBA_BLOCK
    past = sub < q_blk
    r_io = lax.broadcasted_iota(jnp.int32, (MOBA_BLOCK, MOBA_BLOCK), 0)
    c_io = lax.broadcasted_iota(jnp.int32, (MOBA_BLOCK, MOBA_BLOCK), 1)
    causal = c_io <= r_io
    low = lax.broadcasted_iota(jnp.int32, (MOBA_BLOCK, LANES), 1) < HEAD_DIM
    contract_lanes = (((1,), (1,)), ((), ()))
    q_aug, k_aug = [], []
    for hh in range(2):
        lo, hi = hh * HEAD_PAD, (hh + 1) * HEAD_PAD
        k = k_ref[:, lo:hi]
        q = q_ref[:, lo:hi]
        kf = k.astype(F32)
        k_mean = jnp.concatenate(
            [jnp.mean(kf[j * MOBA_BLOCK:(j + 1) * MOBA_BLOCK], axis=0, keepdims=True) for j in range(nb)], axis=0)
        km_hi = k_mean.astype(BF16)
        km_lo = (k_mean - km_hi.astype(F32)).astype(BF16)
        gate = (lax.dot_general(km_hi, q, contract_lanes, preferred_element_type=F32)
                + lax.dot_general(km_lo, q, contract_lanes, preferred_element_type=F32))
        bias_t = jnp.where(sub == q_blk, 0.0, MASK_NEG)
        for j in range(nb - 1):
            gj = gate[j:j + 1, :]
            beats = ((gate > gj) | ((gate == gj) & (sub < j))) & past
            cnt = jnp.sum(beats.astype(F32), axis=0, keepdims=True)
            bias_t = jnp.where((sub == j) & past & (cnt < float(MOBA_TOPK)), 0.0, bias_t)
        bias = jnp.dot(bias_t.T.astype(BF16), place, preferred_element_type=F32)
        q_aug.append(q + bias.astype(BF16))
        k_aug.append(k + blk_onehot)

    def scores(hh, i):
        return lax.dot_general(q_aug[hh][i * MOBA_BLOCK:(i + 1) * MOBA_BLOCK], k_aug[hh][:(i + 1) * MOBA_BLOCK],
                               contract_lanes, preferred_element_type=F32)

    def attend(s, hh, i):
        v_lo = hh * HEAD_PAD
        s_own = jnp.where(causal, s[:, i * MOBA_BLOCK:], MASK_NEG)
        m = jnp.max(s_own, axis=-1, keepdims=True)
        if i > 0:
            s_past = s[:, :i * MOBA_BLOCK]
            m = jnp.maximum(m, jnp.max(s_past, axis=-1, keepdims=True))
        acc = jnp.dot(jnp.exp2((s_own - m).astype(BF16)),
                      v_ref[i * MOBA_BLOCK:(i + 1) * MOBA_BLOCK, v_lo:v_lo + HEAD_PAD], preferred_element_type=F32)
        if i > 0:
            acc = acc + jnp.dot(jnp.exp2((s_past - m).astype(BF16)), v_ref[:i * MOBA_BLOCK, v_lo:v_lo + HEAD_PAD],
                                preferred_element_type=F32)
        return jnp.where(low, acc / acc[:, HEAD_DIM:HEAD_DIM + 1], 0.0)

    items = [(hh, i) for i in range(nb) for hh in range(2)]
    s_cur = scores(*items[0])
    even = None
    for n, (hh, i) in enumerate(items):
        s_next = scores(*items[n + 1]) if n + 1 < len(items) else None
        out = attend(s_cur, hh, i)
        if hh == 0:
            even = out
        else:
            o_ref[i * MOBA_BLOCK:(i + 1) * MOBA_BLOCK, :] = even + pltpu.roll(out, HEAD_DIM, 1)
        s_cur = s_next


def _moba(qp, kp, vp, batch, seq):
    nb = seq // MOBA_BLOCK
    lane = jnp.arange(LANES)[None, :]
    blk = (jnp.arange(seq) // MOBA_BLOCK)[:, None]
    blk_onehot = (lane == HEAD_DIM + blk).astype(BF16)
    place = (lane == HEAD_DIM + jnp.arange(nb)[:, None]).astype(BF16)
    pair = lambda b, c: (b, c)
    fixed = lambda b, c: (0, 0)
    spec = pl.BlockSpec((seq, 2 * HEAD_PAD), pair)
    return pl.pallas_call(
        _moba_kernel,
        grid=(batch, N_HEADS // 2),
        in_specs=[spec, spec, spec, pl.BlockSpec((seq, LANES), fixed), pl.BlockSpec((nb, LANES), fixed)],
        out_specs=pl.BlockSpec((seq, LANES), pair),
        out_shape=jax.ShapeDtypeStruct((batch * seq, ATTN_WIDTH), F32),
        compiler_params=pltpu.CompilerParams(dimension_semantics=("arbitrary", "arbitrary"),
                                             vmem_limit_bytes=VMEM_LIMIT),
        name="moba_attention",
    )(qp, kp, vp, blk_onehot, place)


def _out_proj_kernel(x_ref, c_ref, a_ref, beta_ref, wc_ref, wa_ref, gf_ref, wr_ref, br_ref, tri_ref,
                     x1_ref, bucket_ref, rank_ref, cnt_ref, carry_ref):
    @pl.when(pl.program_id(0) == 0)
    def _():
        carry_ref[...] = jnp.zeros_like(carry_ref)

    an = _rms(a_ref[...], beta_ref[...]).astype(BF16)
    x1 = (x_ref[...] + jnp.dot(c_ref[...], wc_ref[...], preferred_element_type=F32)
          + jnp.dot(an, wa_ref[...], preferred_element_type=F32))
    _store_row_tiles(x1_ref, x1)
    logits = _router_logits(_rms(x1, gf_ref[...]).astype(BF16), wr_ref, br_ref)
    tm = logits.shape[0]
    lt = logits.T
    sub = lax.broadcasted_iota(jnp.int32, (SUBLANES, tm), 0)
    gl = jnp.where(sub < N_GROUPS, lt[0:SUBLANES, :], -jnp.inf)
    g_idx = jnp.min(jnp.where(gl == jnp.max(gl, axis=0, keepdims=True), sub, N_GROUPS), axis=0, keepdims=True)
    el = jnp.zeros((EXPERTS_PER_GROUP, tm), F32)
    for g in range(N_GROUPS):
        lo = ROUTER_EXPERT_COL + g * EXPERTS_PER_GROUP
        el = jnp.where(g_idx == g, lt[lo:lo + EXPERTS_PER_GROUP, :], el)
    i1 = jnp.min(jnp.where(el == jnp.max(el, axis=0, keepdims=True), sub, EXPERTS_PER_GROUP), axis=0, keepdims=True)
    el2 = jnp.where(sub == i1, -jnp.inf, el)
    i2 = jnp.min(jnp.where(el2 == jnp.max(el2, axis=0, keepdims=True), sub, EXPERTS_PER_GROUP), axis=0,
                 keepdims=True)
    bucket = (g_idx * (EXPERTS_PER_GROUP * EXPERTS_PER_GROUP) + jnp.minimum(i1, i2) * EXPERTS_PER_GROUP
              + jnp.maximum(i1, i2))
    onehot = (lax.broadcasted_iota(jnp.int32, (N_BUCKETS, tm), 0) == bucket)
    onehot_bf = onehot.astype(BF16)
    before = jnp.dot(onehot_bf, tri_ref[...], preferred_element_type=F32)
    counts = jnp.dot(onehot_bf, jnp.ones((tm, LANES), BF16), preferred_element_type=F32)
    carry = carry_ref[...]
    carry_wide = jnp.concatenate([carry] * (tm // LANES), axis=1)
    rank = jnp.sum(jnp.where(onehot, before + carry_wide, 0.0), axis=0, keepdims=True)
    carry_ref[...] = carry + counts
    cnt_ref[...] = carry + counts
    bucket_ref[0] = bucket
    rank_ref[0] = rank.astype(jnp.int32)


def _out_proj(x2, conv_n, attn, beta_attn, w_out_bf, g_ffn, w_router_bf, b_router, tm):
    t_tokens = x2.shape[0]
    row = lambda i: (i, 0)
    fixed = lambda i: (0, 0)
    tri = (jnp.arange(tm)[:, None] < jnp.arange(tm)[None, :]).astype(BF16)
    lane_row = pl.BlockSpec((1, 1, tm), lambda i: (i, 0, 0))
    return pl.pallas_call(
        _out_proj_kernel,
        grid=(t_tokens // tm,),
        in_specs=[pl.BlockSpec((tm, D_MODEL), row), pl.BlockSpec((tm, CONV_CH), row),
                  pl.BlockSpec((tm, ATTN_WIDTH), row), pl.BlockSpec((1, ATTN_WIDTH), fixed),
                  pl.BlockSpec((CONV_CH, D_MODEL), fixed), pl.BlockSpec((ATTN_WIDTH, D_MODEL), fixed),
                  pl.BlockSpec((1, D_MODEL), fixed), pl.BlockSpec((D_MODEL, LANES), fixed),
                  pl.BlockSpec((1, LANES), fixed), pl.BlockSpec((tm, tm), fixed)],
        out_specs=[pl.BlockSpec((tm * ROW_CHUNKS, LANES), row), lane_row, lane_row,
                   pl.BlockSpec((N_BUCKETS, LANES), fixed)],
        out_shape=[jax.ShapeDtypeStruct((t_tokens * ROW_CHUNKS, LANES), F32),
                   jax.ShapeDtypeStruct((t_tokens // tm, 1, tm), jnp.int32),
                   jax.ShapeDtypeStruct((t_tokens // tm, 1, tm), jnp.int32),
                   jax.ShapeDtypeStruct((N_BUCKETS, LANES), F32)],
        scratch_shapes=[pltpu.VMEM((N_BUCKETS, LANES), F32)],
        compiler_params=pltpu.CompilerParams(dimension_semantics=("arbitrary",), vmem_limit_bytes=VMEM_LIMIT),
        name="out_proj_router",
    )(x2, conv_n, attn, beta_attn.reshape(1, -1), w_out_bf[:CONV_CH], w_out_bf[CONV_CH:],
      g_ffn.reshape(1, -1), w_router_bf, b_router, tri)


def _positions_kernel(bucket_ref, rank_ref, starts_ref, pos_ref):
    bucket = bucket_ref[0]
    ids = lax.broadcasted_iota(jnp.int32, (N_BUCKETS, bucket.shape[1]), 0)
    pos_ref[0] = jnp.sum(jnp.where(ids == bucket, starts_ref[...], 0), axis=0, keepdims=True) + rank_ref[0]


def _positions(bucket, rank, starts, tm):
    t_tokens = bucket.size
    lane_row = pl.BlockSpec((1, 1, tm), lambda i: (i, 0, 0))
    shape = (t_tokens // tm, 1, tm)
    pos = pl.pallas_call(
        _positions_kernel,
        grid=(t_tokens // tm,),
        in_specs=[lane_row, lane_row, pl.BlockSpec((N_BUCKETS, 1), lambda i: (0, 0))],
        out_specs=lane_row,
        out_shape=jax.ShapeDtypeStruct(shape, jnp.int32),
        compiler_params=pltpu.CompilerParams(dimension_semantics=("arbitrary",), vmem_limit_bytes=VMEM_LIMIT),
        name="moe_positions",
    )(bucket.reshape(shape), rank.reshape(shape), starts.reshape(N_BUCKETS, 1))
    return pos.reshape(t_tokens)


def _row_tile(ref, r):
    return ref.at[pl.ds(pl.multiple_of(r * ROW_CHUNKS, ROW_CHUNKS), ROW_CHUNKS), :]


def _dispatch_kernel(pos_ref, pad_start_ref, pad_len_ref, total_ref, x_ref, out_hbm, zero_ref, sem, pad_sem):
    tm = x_ref.shape[0] // ROW_CHUNKS
    step, n_steps = pl.program_id(0), pl.num_programs(0)
    base = step * tm
    half_tile = zero_ref.shape[0] // ROW_CHUNKS
    n_tiles = out_hbm.shape[0] // (2 * zero_ref.shape[0])

    def zero_copy(dst_hbm, row, rows, wait):
        cp = pltpu.make_async_copy(
            zero_ref.at[pl.ds(0, rows * ROW_CHUNKS), :],
            dst_hbm.at[pl.ds(pl.multiple_of(row * ROW_CHUNKS, ROW_CHUNKS), rows * ROW_CHUNKS), :], pad_sem)
        cp.wait() if wait else cp.start()

    def fill(wait):
        def pad(k, carry):
            b = k * n_steps + step
            off, n = pad_start_ref[b], pad_len_ref[b]
            for bit in range(PAD_BITS):
                take = ((n >> bit) & 1) == 1
                pl.when(take)(lambda off=off, bit=bit: zero_copy(out_hbm, off, 1 << bit, wait))
                off = off + jnp.where(take, 1 << bit, 0)
            return carry

        def unused_tile(k, carry):
            row = (total_ref[0] + step + k * n_steps) * 2 * half_tile
            zero_copy(out_hbm, row, half_tile, wait)
            zero_copy(out_hbm, row + half_tile, half_tile, wait)
            return carry

        lax.fori_loop(0, N_BUCKETS // n_steps, pad, 0)
        n_unused = n_tiles - total_ref[0] - step
        lax.fori_loop(0, (jnp.maximum(n_unused, 0) + n_steps - 1) // n_steps, unused_tile, 0)

    @pl.when(step == 0)
    def _():
        zero_ref[...] = jnp.zeros_like(zero_ref)

    fill(wait=False)

    def row_copy(r, dst):
        return pltpu.make_async_copy(_row_tile(x_ref, r), _row_tile(out_hbm, dst), sem)

    def issue(g, carry):
        for u in range(DMA_GROUP):
            r = g * DMA_GROUP + u
            row_copy(r, pos_ref[base + r]).start(priority=u % 2)
        return carry

    lax.fori_loop(0, tm // DMA_GROUP, issue, 0)

    def drain(r, carry):
        row_copy(0, 0).wait()
        return carry

    lax.fori_loop(0, tm, drain, 0, unroll=8)
    fill(wait=True)


def _dispatch(pos, pad_start, pad_len, total, x_rows, sorted_rows, tm):
    n_rows = x_rows.shape[0]
    n_steps = n_rows // (tm * ROW_CHUNKS)
    assert N_BUCKETS % n_steps == 0
    return pl.pallas_call(
        _dispatch_kernel,
        grid_spec=pltpu.PrefetchScalarGridSpec(
            num_scalar_prefetch=4,
            grid=(n_steps,),
            in_specs=[pl.BlockSpec((tm * ROW_CHUNKS, LANES), lambda i, *_: (i, 0))],
            out_specs=pl.BlockSpec(memory_space=pl.ANY),
            scratch_shapes=[pltpu.VMEM(((1 << (PAD_BITS - 1)) * ROW_CHUNKS, LANES), F32),
                            pltpu.SemaphoreType.DMA(()), pltpu.SemaphoreType.DMA(())],
        ),
        out_shape=jax.ShapeDtypeStruct((sorted_rows * ROW_CHUNKS, LANES), F32),
        compiler_params=pltpu.CompilerParams(dimension_semantics=("arbitrary",), vmem_limit_bytes=VMEM_LIMIT,
                                             has_side_effects=True),
        name="moe_dispatch",
    )(pos, pad_start, pad_len, total, x_rows)


def _moe_schedule(counts, n_steps, tm):
    tiles_b = (counts + tm - 1) // tm
    step_end = jnp.cumsum(tiles_b)
    step_start = step_end - tiles_b
    starts = step_start * tm
    total = step_end[-1]
    s = jnp.minimum(jnp.arange(n_steps, dtype=jnp.int32), total - 1)
    mine = (s[:, None] >= step_start[None, :]) & (s[:, None] < step_end[None, :])
    pick = lambda v: jnp.sum(jnp.where(mine, v[None, :], 0), axis=1).astype(jnp.int32)
    ids = jnp.arange(N_BUCKETS, dtype=jnp.int32)
    pair = EXPERTS_PER_GROUP * EXPERTS_PER_GROUP
    valid = jnp.clip(pick(counts) - (s - pick(step_start)) * tm, 0, tm).astype(jnp.int32)
    schedule = (s, pick(ids // pair), pick((ids % pair) // EXPERTS_PER_GROUP), pick(ids % EXPERTS_PER_GROUP),
                valid, total.astype(jnp.int32).reshape(1))
    return starts, (starts + counts).astype(jnp.int32), (tiles_b * tm - counts).astype(jnp.int32), schedule


def _moe_kernel(tile_ref, grp_ref, elo_ref, ehi_ref, valid_ref, total_ref,
                x_ref, gf_ref, wr_ref, br_ref, wgl_ref, wul_ref, wdl_ref, wgh_ref, wuh_ref, wdh_ref, y_ref):
    s = pl.program_id(0)
    tm = x_ref.shape[0] // ROW_CHUNKS

    @pl.when(s < total_ref[0])
    def _():
        inside = lax.broadcasted_iota(jnp.int32, (tm, 1), 0) < valid_ref[s]
        x = _load_row_tiles(x_ref, tm)
        hg = (x * gf_ref[...]).astype(BF16)
        inv_rms = lax.rsqrt(jnp.mean(x * x, axis=-1, keepdims=True) + EPS)
        logits = inv_rms * jnp.dot(hg, wr_ref[...], preferred_element_type=F32) + br_ref[...]
        lane = lax.broadcasted_iota(jnp.int32, (tm, LANES), 1)
        col = ROUTER_EXPERT_COL + grp_ref[s] * EXPERTS_PER_GROUP
        lane_val = lambda idx: jnp.sum(jnp.where(lane == idx, logits, 0.0), axis=-1, keepdims=True)
        gl = jnp.where(lane < N_GROUPS, logits, -jnp.inf)
        g_w = 1.0 / jnp.sum(jnp.exp(gl - jnp.max(gl, axis=-1, keepdims=True)), axis=-1, keepdims=True)
        v_lo, v_hi = lane_val(col + elo_ref[s]), lane_val(col + ehi_ref[s])
        v_max = jnp.maximum(v_lo, v_hi)
        p_lo, p_hi = jnp.exp(v_lo - v_max), jnp.exp(v_hi - v_max)
        scale = jnp.where(inside, inv_rms * g_w / (p_lo + p_hi), 0.0)

        def expert(wg_ref, wu_ref, wd_ref, w_row):
            a = inv_rms * jnp.dot(hg, wg_ref[0], preferred_element_type=F32)
            b = jnp.dot(hg, wu_ref[0], preferred_element_type=F32)
            hid = a * jax.nn.sigmoid(a) * b * w_row
            return jnp.dot(hid.astype(BF16), wd_ref[0], preferred_element_type=F32)

        y = expert(wgl_ref, wul_ref, wdl_ref, p_lo * scale) + expert(wgh_ref, wuh_ref, wdh_ref, p_hi * scale)
        _store_row_tiles(y_ref, y)

    @pl.when(s >= total_ref[0])
    def _():
        y_ref[...] = jnp.zeros_like(y_ref)


def _moe(schedule, x_sorted, g_ffn, w_router_bf, b_router, wg_bf, wu_bf, wd_bf, tm):
    n_rows = x_sorted.shape[0]
    n_steps = schedule[0].shape[0]
    row = lambda s, tile, *_: (tile[s], 0)
    fixed = lambda s, *_: (0, 0)
    w_lo = lambda s, tile, grp, elo, ehi, *_: (grp[s] * EXPERTS_PER_GROUP + elo[s], 0, 0)
    w_hi = lambda s, tile, grp, elo, ehi, *_: (grp[s] * EXPERTS_PER_GROUP + ehi[s], 0, 0)
    up_spec = lambda sel: pl.BlockSpec((1, D_MODEL, D_EXPERT), sel)
    down_spec = lambda sel: pl.BlockSpec((1, D_EXPERT, D_MODEL), sel)
    return pl.pallas_call(
        _moe_kernel,
        grid_spec=pltpu.PrefetchScalarGridSpec(
            num_scalar_prefetch=6,
            grid=(n_steps,),
            in_specs=[pl.BlockSpec((tm * ROW_CHUNKS, LANES), row), pl.BlockSpec((1, D_MODEL), fixed),
                      pl.BlockSpec((D_MODEL, LANES), fixed), pl.BlockSpec((1, LANES), fixed),
                      up_spec(w_lo), up_spec(w_lo), down_spec(w_lo), up_spec(w_hi), up_spec(w_hi), down_spec(w_hi)],
            out_specs=pl.BlockSpec((tm * ROW_CHUNKS, LANES), lambda s, *_: (s, 0)),
        ),
        out_shape=jax.ShapeDtypeStruct((n_rows, LANES), F32),
        compiler_params=pltpu.CompilerParams(dimension_semantics=("arbitrary",), vmem_limit_bytes=VMEM_LIMIT),
        name="hier_moe",
    )(*schedule, x_sorted, g_ffn.reshape(1, -1), w_router_bf, b_router, wg_bf, wu_bf, wd_bf, wg_bf, wu_bf, wd_bf)


def _ple_final_kernel(pos_ref, x1_ref, ys_hbm, p_ref, gp_ref, wg_ref, bg_ref, wp_ref, gfin_ref, o_ref,
                      ybuf, sems):
    i = pl.program_id(0)
    n = pl.num_programs(0)
    tm = o_ref.shape[0]

    def row_copy(src, r, slot):
        return pltpu.make_async_copy(_row_tile(ys_hbm, src), _row_tile(ybuf.at[slot], r), sems.at[slot])

    def gather(tile, slot):
        def issue(g, carry):
            for u in range(DMA_GROUP):
                r = g * DMA_GROUP + u
                row_copy(pos_ref[tile * tm + r], r, slot).start(priority=u % 2)
            return carry
        lax.fori_loop(0, tm // DMA_GROUP, issue, 0)

    @pl.when(i == 0)
    def _():
        gather(0, 0)

    slot = i % 2

    def drain(r, carry):
        row_copy(0, r, slot).wait()
        return carry

    def step(prefetch_next):
        lax.fori_loop(0, tm, drain, 0, unroll=8)
        x2 = _load_row_tiles(x1_ref, tm) + _load_row_tiles(ybuf.at[slot], tm)
        if prefetch_next:
            for r in range(tm):
                row_copy(pos_ref[(i + 1) * tm + r], r, 1 - slot).start(priority=r % 2)
        hg = _rms(x2, gp_ref[...]).astype(BF16)
        gate = jax.nn.sigmoid(jnp.dot(hg, wg_ref[...], preferred_element_type=F32) + bg_ref[...])
        emb = jnp.dot(p_ref[...].astype(BF16), wp_ref[...], preferred_element_type=F32)
        x3 = x2 + emb * gate
        o_ref[...] = _rms(x3, gfin_ref[...])

    @pl.when(i + 1 < n)
    def _():
        step(True)

    @pl.when(i + 1 == n)
    def _():
        step(False)


def _ple_final(pos, x1_rows, y_sorted, p2, g_ple, w_gate_bf, b_gate, w_proj_bf, g_final, tm):
    t_tokens = p2.shape[0]
    row = lambda i, pos: (i, 0)
    fixed = lambda i, pos: (0, 0)
    vec = pl.BlockSpec((1, D_MODEL), fixed)
    return pl.pallas_call(
        _ple_final_kernel,
        grid_spec=pltpu.PrefetchScalarGridSpec(
            num_scalar_prefetch=1,
            grid=(t_tokens // tm,),
            in_specs=[pl.BlockSpec((tm * ROW_CHUNKS, LANES), row), pl.BlockSpec(memory_space=pl.ANY),
                      pl.BlockSpec((tm, PLE_DIM), row), vec, pl.BlockSpec((D_MODEL, D_MODEL), fixed), vec,
                      pl.BlockSpec((PLE_DIM, D_MODEL), fixed), vec],
            out_specs=pl.BlockSpec((tm, D_MODEL), row),
            scratch_shapes=[pltpu.VMEM((2, tm * ROW_CHUNKS, LANES), F32), pltpu.SemaphoreType.DMA((2,))],
        ),
        out_shape=jax.ShapeDtypeStruct((t_tokens, D_MODEL), F32),
        compiler_params=pltpu.CompilerParams(dimension_semantics=("arbitrary",), vmem_limit_bytes=VMEM_LIMIT),
        name="ple_final",
    )(pos, x1_rows, y_sorted, p2, g_ple.reshape(1, -1), w_gate_bf, b_gate.reshape(1, -1), w_proj_bf,
      g_final.reshape(1, -1))


def kernel(x, p, g_mix, w_in, w_dw, b_dw, conv_ln_g, conv_ln_b, beta_conv, beta_attn, w_out, g_ffn, w_group, b_group, w_erouter, b_erouter, w_gate, w_up, w_down, g_ple, w_ple_gate, b_ple_gate, w_ple_proj, g_final):
    batch, seq, _ = x.shape
    assert p.shape[0] == 1 and seq % MOBA_BLOCK == 0
    t_tokens = batch * seq
    x2 = x.reshape(t_tokens, D_MODEL)
    i = 0
    u, qp, kp, vp = _in_proj(x2, g_mix[i], w_in[i].astype(BF16), seq, tm=512)
    conv_n = _conv(u, w_dw[i], b_dw[i], conv_ln_g[i], conv_ln_b[i], beta_conv[i], batch, seq, ts=256)
    attn = _moba(qp, kp, vp, batch, seq)
    gap = ROUTER_EXPERT_COL - N_GROUPS
    tail = LANES - ROUTER_EXPERT_COL - N_EXPERTS
    w_router = jnp.concatenate(
        [w_group[i], jnp.zeros((D_MODEL, gap), F32),
         jnp.transpose(w_erouter[i], (1, 0, 2)).reshape(D_MODEL, N_EXPERTS),
         jnp.zeros((D_MODEL, tail), F32)], axis=1).astype(BF16)
    b_router = jnp.concatenate([b_group[i], jnp.zeros((gap,), F32), b_erouter[i].reshape(-1),
                                jnp.zeros((tail,), F32)]).reshape(1, LANES)
    x1_rows, bucket, rank, counts = _out_proj(x2, conv_n, attn, beta_attn[i], w_out[i].astype(BF16), g_ffn[i],
                                              w_router, b_router, tm=512)
    moe_tm = 1 << PAD_BITS
    moe_steps = t_tokens // moe_tm + N_PAIR_BUCKETS
    starts, pad_start, pad_len, schedule = _moe_schedule(counts[:, 0].astype(jnp.int32), moe_steps, moe_tm)
    pos = _positions(bucket, rank, starts, tm=2048)
    x1_sorted = _dispatch(pos, pad_start, pad_len, schedule[-1], x1_rows, moe_steps * moe_tm, tm=1024)
    y_sorted = _moe(schedule, x1_sorted, g_ffn[i], w_router, b_router, w_gate[i].astype(BF16),
                    w_up[i].astype(BF16), w_down[i].astype(BF16), moe_tm)
    out = _ple_final(pos, x1_rows, y_sorted, p[i].reshape(t_tokens, PLE_DIM), g_ple[i],
                     w_ple_gate[i].astype(BF16), b_ple_gate[i], w_ple_proj[i].astype(BF16), g_final, tm=512)
    return out.reshape(batch, seq, D_MODEL)
```

```python
import jax
import jax.numpy as jnp
from jax import lax
from jax.experimental import pallas as pl
from jax.experimental.pallas import tpu as pltpu

F32 = jnp.float32
BF16 = jnp.bfloat16

D_MODEL = 1024
CONV_CH = 512
CONV_WIDTH = 31
ATTN_WIDTH = 512
HEAD_DIM = 64
N_HEADS = 8
ROT_DIM = 16
ROPE_THETA = 500000.0
MOBA_BLOCK = 256
MOBA_TOPK = 3
N_GROUPS = 4
EXPERTS_PER_GROUP = 8
N_EXPERTS = 32
D_EXPERT = 256
PLE_DIM = 256
N_BUCKETS = N_GROUPS * EXPERTS_PER_GROUP * EXPERTS_PER_GROUP
N_PAIR_BUCKETS = N_GROUPS * EXPERTS_PER_GROUP * (EXPERTS_PER_GROUP - 1) // 2
IN_COLS = 2 * CONV_CH + 3 * ATTN_WIDTH
EPS = 1e-6

LANES = 128
SUBLANES = 8
HEAD_PAD = LANES
QKV_PAD = N_HEADS * HEAD_PAD
ROW_CHUNKS = D_MODEL // LANES
assert ROW_CHUNKS == SUBLANES
MASK_NEG = -1e30
Q_SCALE = HEAD_DIM ** -0.5 * 1.4426950408889634
CONV_HALO = 32
CONV_ROWS = 32
DMA_GROUP = 8
PAD_BITS = 8
ROUTER_EXPERT_COL = SUBLANES
VMEM_LIMIT = 56 * 1024 * 1024


def _rms(x, g):
    return x * lax.rsqrt(jnp.mean(x * x, axis=-1, keepdims=True) + EPS) * g


def _load_row_tiles(ref, n_rows):
    return jnp.concatenate([ref[pl.ds(c, n_rows, stride=ROW_CHUNKS), :] for c in range(ROW_CHUNKS)], axis=1)


def _store_row_tiles(ref, val, accumulate=False):
    n_rows = val.shape[0]
    for c in range(ROW_CHUNKS):
        idx = (pl.ds(c, n_rows, stride=ROW_CHUNKS), slice(None))
        piece = val[:, c * LANES:(c + 1) * LANES]
        ref[idx] = ref[idx] + piece if accumulate else piece


def _router_logits(h_bf16, wr_ref, br_ref):
    return jnp.dot(h_bf16, wr_ref[...], preferred_element_type=F32) + br_ref[...]


def _in_proj_kernel(x_ref, g_ref, w_ref, rc_ref, ra_ref, rb_ref, wg32_ref, wu32_ref, wd32_ref,
                    u_ref, q_ref, k_ref, v_ref, wg16_ref, wu16_ref, wd16_ref):
    wg16_ref[...] = wg32_ref[...].astype(BF16)
    wu16_ref[...] = wu32_ref[...].astype(BF16)
    wd16_ref[...] = wd32_ref[...].astype(BF16)
    h = _rms(x_ref[...], g_ref[...]).astype(BF16)
    proj = jnp.dot(h, w_ref[...], preferred_element_type=F32)
    u_ref[...] = proj[:, :CONV_CH] * jax.nn.sigmoid(proj[:, CONV_CH:2 * CONV_CH])
    rc, ra, rb = rc_ref[...], ra_ref[...], rb_ref[...]
    low = lax.broadcasted_iota(jnp.int32, (x_ref.shape[0], LANES), 1) < HEAD_DIM

    def put(dst_ref, base, rope, scale, spare):
        for c in range(ATTN_WIDTH // LANES):
            t = proj[:, base + c * LANES: base + (c + 1) * LANES]
            if rope:
                t = t * rc + pltpu.roll(t, LANES - ROT_DIM // 2, 1) * ra + pltpu.roll(t, ROT_DIM // 2, 1) * rb
            if scale != 1.0:
                t = t * scale
            even = jnp.where(low, t, spare)
            odd = jnp.where(low, pltpu.roll(t, HEAD_DIM, 1), spare)
            dst_ref[:, (2 * c) * HEAD_PAD:(2 * c + 1) * HEAD_PAD] = even.astype(dst_ref.dtype)
            dst_ref[:, (2 * c + 1) * HEAD_PAD:(2 * c + 2) * HEAD_PAD] = odd.astype(dst_ref.dtype)

    ones_lane = jnp.where(lax.broadcasted_iota(jnp.int32, (x_ref.shape[0], LANES), 1) == HEAD_DIM, 1.0, 0.0)
    put(q_ref, 2 * CONV_CH, True, Q_SCALE, 0.0)
    put(k_ref, 2 * CONV_CH + ATTN_WIDTH, True, 1.0, 0.0)
    put(v_ref, 2 * CONV_CH + 2 * ATTN_WIDTH, False, 1.0, ones_lane)


def _rope_coeff_tables(seq):
    pos = jnp.arange(seq, dtype=F32)
    inv_freq = jnp.power(jnp.float32(ROPE_THETA), -jnp.arange(0, ROT_DIM, 2, dtype=F32) / ROT_DIM)
    ang = pos[:, None] * inv_freq[None, :]
    cos, sin = jnp.cos(ang), jnp.sin(ang)
    zeros = jnp.zeros((seq, HEAD_DIM - ROT_DIM), F32)
    c_head = jnp.concatenate([cos, cos, jnp.ones_like(zeros)], axis=1)
    a_head = jnp.concatenate([-sin, jnp.zeros_like(sin), zeros], axis=1)
    b_head = jnp.concatenate([jnp.zeros_like(sin), sin, zeros], axis=1)
    rep = LANES // HEAD_DIM
    return jnp.tile(c_head, (1, rep)), jnp.tile(a_head, (1, rep)), jnp.tile(b_head, (1, rep))


def _expert_slice_spec(shape, n_steps):
    n_e, rows, cols = shape
    if n_steps >= n_e:
        per = n_steps // n_e
        assert n_steps == per * n_e and rows % (per * 2 * SUBLANES) == 0
        return pl.BlockSpec((1, rows // per, cols), lambda i: (i // per, i % per, 0))
    assert n_e % n_steps == 0
    return pl.BlockSpec((n_e // n_steps, rows, cols), lambda i: (i, 0, 0))


def _in_proj(x2, g_mix, w_in_bf, w_gate, w_up, w_down, seq, tm):
    t_tokens = x2.shape[0]
    n_steps = t_tokens // tm
    rc, ra, rb = _rope_coeff_tables(seq)
    n_seq_tiles = seq // tm
    row = lambda i: (i, 0)
    fixed = lambda i: (0, 0)
    tab = lambda i: (i % n_seq_tiles, 0)
    w_specs = [_expert_slice_spec(w.shape, n_steps) for w in (w_gate, w_up, w_down)]
    return pl.pallas_call(
        _in_proj_kernel,
        grid=(n_steps,),
        in_specs=[pl.BlockSpec((tm, D_MODEL), row), pl.BlockSpec((1, D_MODEL), fixed),
                  pl.BlockSpec((D_MODEL, IN_COLS), fixed),
                  pl.BlockSpec((tm, LANES), tab), pl.BlockSpec((tm, LANES), tab), pl.BlockSpec((tm, LANES), tab)]
        + w_specs,
        out_specs=[pl.BlockSpec((tm, CONV_CH), row), pl.BlockSpec((tm, QKV_PAD), row),
                   pl.BlockSpec((tm, QKV_PAD), row), pl.BlockSpec((tm, QKV_PAD), row)] + w_specs,
        out_shape=[jax.ShapeDtypeStruct((t_tokens, CONV_CH), F32),
                   jax.ShapeDtypeStruct((t_tokens, QKV_PAD), BF16),
                   jax.ShapeDtypeStruct((t_tokens, QKV_PAD), BF16),
                   jax.ShapeDtypeStruct((t_tokens, QKV_PAD), BF16)]
        + [jax.ShapeDtypeStruct(w.shape, BF16) for w in (w_gate, w_up, w_down)],
        compiler_params=pltpu.CompilerParams(dimension_semantics=("arbitrary",), vmem_limit_bytes=VMEM_LIMIT),
        name="in_proj",
    )(x2, g_mix.reshape(1, D_MODEL), w_in_bf, rc, ra, rb, w_gate, w_up, w_down)


def _conv_begin(u_ref, pad_ref, shift_ref, first_tile):
    ts = u_ref.shape[0]

    @pl.when(first_tile)
    def _():
        pad_ref[0:CONV_HALO, :] = jnp.zeros((CONV_HALO, CONV_CH), F32)

    pad_ref[CONV_HALO:CONV_HALO + ts, :] = u_ref[...]
    kept = ts + CONV_HALO - SUBLANES
    for sh in range(1, SUBLANES):
        shift_ref[sh - 1, 0:kept, :] = pad_ref[sh:sh + kept, :]


def _conv_rows(r0, w, pad_ref, shift_ref, b_ref, lg_ref, lb_ref, beta_ref, o_ref):
    first = CONV_HALO - (CONV_WIDTH - 1)

    def window(start):
        sh, base = start % SUBLANES, start - start % SUBLANES
        if sh == 0:
            return pad_ref[base:base + CONV_ROWS, :]
        return shift_ref[sh - 1, base:base + CONV_ROWS, :]

    acc = jnp.zeros((CONV_ROWS, CONV_CH), F32)
    for tap in range(CONV_WIDTH):
        acc = acc + window(r0 + first + tap) * w[tap:tap + 1, :]
    y = acc + b_ref[...]
    mu = jnp.mean(y, axis=-1, keepdims=True)
    d = y - mu
    var = jnp.mean(d * d, axis=-1, keepdims=True)
    y = d * lax.rsqrt(var + EPS) * lg_ref[...] + lb_ref[...]
    y = y * jax.nn.sigmoid(y)
    o_ref[r0:r0 + CONV_ROWS, :] = _rms(y, beta_ref[...]).astype(o_ref.dtype)


def _conv_end(pad_ref, ts):
    pad_ref[0:CONV_HALO, :] = pad_ref[ts:ts + CONV_HALO, :]


def _conv_kernel(u_ref, w_ref, b_ref, lg_ref, lb_ref, beta_ref, o_ref, pad_ref, shift_ref):
    ts = u_ref.shape[0]
    _conv_begin(u_ref, pad_ref, shift_ref, pl.program_id(1) == 0)
    w = w_ref[...]
    for r0 in range(0, ts, CONV_ROWS):
        _conv_rows(r0, w, pad_ref, shift_ref, b_ref, lg_ref, lb_ref, beta_ref, o_ref)
    _conv_end(pad_ref, ts)


def _conv(u2, w_dw, b_dw, ln_g, ln_b, beta, batch, seq, ts):
    n_s = seq // ts
    row = lambda b, s: (b * n_s + s, 0)
    fixed = lambda b, s: (0, 0)
    vec = pl.BlockSpec((1, CONV_CH), fixed)
    return pl.pallas_call(
        _conv_kernel,
        grid=(batch, n_s),
        in_specs=[pl.BlockSpec((ts, CONV_CH), row), pl.BlockSpec((CONV_WIDTH, CONV_CH), fixed), vec, vec, vec, vec],
        out_specs=pl.BlockSpec((ts, CONV_CH), row),
        out_shape=jax.ShapeDtypeStruct((batch * seq, CONV_CH), BF16),
        scratch_shapes=[pltpu.VMEM((ts + CONV_HALO, CONV_CH), F32),
                        pltpu.VMEM((SUBLANES - 1, ts + CONV_HALO, CONV_CH), F32)],
        compiler_params=pltpu.CompilerParams(dimension_semantics=("arbitrary", "arbitrary"),
                                             vmem_limit_bytes=VMEM_LIMIT),
        name="conformer_conv",
    )(u2, w_dw, b_dw.reshape(1, -1), ln_g.reshape(1, -1), ln_b.reshape(1, -1), beta.reshape(1, -1))


def _moba_kernel(q_ref, k_ref, v_ref, blk_ref, place_ref, o_ref):
    seq = q_ref.shape[0]
    nb = seq // MOBA_BLOCK
    blk_onehot = blk_ref[...]
    place = place_ref[...]
    sub = lax.broadcasted_iota(jnp.int32, (nb, seq), 0)
    q_blk = lax.broadcasted_iota(jnp.int32, (nb, seq), 1) // MOBA_BLOCK
    past = sub < q_blk
    r_io = lax.broadcasted_iota(jnp.int32, (MOBA_BLOCK, MOBA_BLOCK), 0)
    c_io = lax.broadcasted_iota(jnp.int32, (MOBA_BLOCK, MOBA_BLOCK), 1)
    causal = c_io <= r_io
    low = lax.broadcasted_iota(jnp.int32, (MOBA_BLOCK, LANES), 1) < HEAD_DIM
    contract_lanes = (((1,), (1,)), ((), ()))
    q_aug, k_aug = [], []
    for hh in range(2):
        lo, hi = hh * HEAD_PAD, (hh + 1) * HEAD_PAD
        k = k_ref[:, lo:hi]
        q = q_ref[:, lo:hi]
        kf = k.astype(F32)
        k_mean = jnp.concatenate(
            [jnp.mean(kf[j * MOBA_BLOCK:(j + 1) * MOBA_BLOCK], axis=0, keepdims=True) for j in range(nb)], axis=0)
        km_hi = k_mean.astype(BF16)
        km_lo = (k_mean - km_hi.astype(F32)).astype(BF16)
        gate = (lax.dot_general(km_hi, q, contract_lanes, preferred_element_type=F32)
                + lax.dot_general(km_lo, q, contract_lanes, preferred_element_type=F32))
        bias_t = jnp.where(sub == q_blk, 0.0, MASK_NEG)
        for j in range(nb - 1):
            gj = gate[j:j + 1, :]
            beats = ((gate > gj) | ((gate == gj) & (sub < j))) & past
            cnt = jnp.sum(beats.astype(F32), axis=0, keepdims=True)
            bias_t = jnp.where((sub == j) & past & (cnt < float(MOBA_TOPK)), 0.0, bias_t)
        bias = jnp.dot(bias_t.T.astype(BF16), place, preferred_element_type=F32)
        q_aug.append(q + bias.astype(BF16))
        k_aug.append(k + blk_onehot)

    def scores(hh, i):
        return lax.dot_general(q_aug[hh][i * MOBA_BLOCK:(i + 1) * MOBA_BLOCK], k_aug[hh][:(i + 1) * MOBA_BLOCK],
                               contract_lanes, preferred_element_type=F32)

    def attend(s, hh, i):
        v_lo = hh * HEAD_PAD
        s_own = jnp.where(causal, s[:, i * MOBA_BLOCK:], MASK_NEG)
        m = jnp.max(s_own, axis=-1, keepdims=True)
        if i > 0:
            s_past = s[:, :i * MOBA_BLOCK]
            m = jnp.maximum(m, jnp.max(s_past, axis=-1, keepdims=True))
        acc = jnp.dot(jnp.exp2((s_own - m).astype(BF16)),
                      v_ref[i * MOBA_BLOCK:(i + 1) * MOBA_BLOCK, v_lo:v_lo + HEAD_PAD], preferred_element_type=F32)
        if i > 0:
            acc = acc + jnp.dot(jnp.exp2((s_past - m).astype(BF16)), v_ref[:i * MOBA_BLOCK, v_lo:v_lo + HEAD_PAD],
                                preferred_element_type=F32)
        return jnp.where(low, acc / acc[:, HEAD_DIM:HEAD_DIM + 1], 0.0)

    items = [(hh, i) for i in range(nb) for hh in range(2)]
    s_cur = scores(*items[0])
    even = None
    for n, (hh, i) in enumerate(items):
        s_next = scores(*items[n + 1]) if n + 1 < len(items) else None
        out = attend(s_cur, hh, i)
        if hh == 0:
            even = out
        else:
            o_ref[i * MOBA_BLOCK:(i + 1) * MOBA_BLOCK, :] = even + pltpu.roll(out, HEAD_DIM, 1)
        s_cur = s_next


def _moba(qp, kp, vp, batch, seq):
    nb = seq // MOBA_BLOCK
    lane = jnp.arange(LANES)[None, :]
    blk = (jnp.arange(seq) // MOBA_BLOCK)[:, None]
    blk_onehot = (lane == HEAD_DIM + blk).astype(BF16)
    place = (lane == HEAD_DIM + jnp.arange(nb)[:, None]).astype(BF16)
    pair = lambda b, c: (b, c)
    fixed = lambda b, c: (0, 0)
    spec = pl.BlockSpec((seq, 2 * HEAD_PAD), pair)
    return pl.pallas_call(
        _moba_kernel,
        grid=(batch, N_HEADS // 2),
        in_specs=[spec, spec, spec, pl.BlockSpec((seq, LANES), fixed), pl.BlockSpec((nb, LANES), fixed)],
        out_specs=pl.BlockSpec((seq, LANES), pair),
        out_shape=jax.ShapeDtypeStruct((batch * seq, ATTN_WIDTH), F32),
        compiler_params=pltpu.CompilerParams(dimension_semantics=("arbitrary", "arbitrary"),
                                             vmem_limit_bytes=VMEM_LIMIT),
        name="moba_attention",
    )(qp, kp, vp, blk_onehot, place)


def _out_proj_kernel(x_ref, c_ref, a_ref, beta_ref, wc_ref, wa_ref, gf_ref, wr_ref, br_ref, tri_ref,
                     x1_ref, bucket_ref, rank_ref, cnt_ref, carry_ref):
    @pl.when(pl.program_id(0) == 0)
    def _():
        carry_ref[...] = jnp.zeros_like(carry_ref)

    an = _rms(a_ref[...], beta_ref[...]).astype(BF16)
    x1 = (x_ref[...] + jnp.dot(c_ref[...], wc_ref[...], preferred_element_type=F32)
          + jnp.dot(an, wa_ref[...], preferred_element_type=F32))
    _store_row_tiles(x1_ref, x1)
    logits = _router_logits(_rms(x1, gf_ref[...]).astype(BF16), wr_ref, br_ref)
    tm = logits.shape[0]
    lt = logits.T
    sub = lax.broadcasted_iota(jnp.int32, (SUBLANES, tm), 0)
    gl = jnp.where(sub < N_GROUPS, lt[0:SUBLANES, :], -jnp.inf)
    g_idx = jnp.min(jnp.where(gl == jnp.max(gl, axis=0, keepdims=True), sub, N_GROUPS), axis=0, keepdims=True)
    el = jnp.zeros((EXPERTS_PER_GROUP, tm), F32)
    for g in range(N_GROUPS):
        lo = ROUTER_EXPERT_COL + g * EXPERTS_PER_GROUP
        el = jnp.where(g_idx == g, lt[lo:lo + EXPERTS_PER_GROUP, :], el)
    i1 = jnp.min(jnp.where(el == jnp.max(el, axis=0, keepdims=True), sub, EXPERTS_PER_GROUP), axis=0, keepdims=True)
    el2 = jnp.where(sub == i1, -jnp.inf, el)
    i2 = jnp.min(jnp.where(el2 == jnp.max(el2, axis=0, keepdims=True), sub, EXPERTS_PER_GROUP), axis=0,
                 keepdims=True)
    bucket = (g_idx * (EXPERTS_PER_GROUP * EXPERTS_PER_GROUP) + jnp.minimum(i1, i2) * EXPERTS_PER_GROUP
              + jnp.maximum(i1, i2))
    onehot = (lax.broadcasted_iota(jnp.int32, (N_BUCKETS, tm), 0) == bucket)
    onehot_bf = onehot.astype(BF16)
    before = jnp.dot(onehot_bf, tri_ref[...], preferred_element_type=F32)
    counts = jnp.dot(onehot_bf, jnp.ones((tm, LANES), BF16), preferred_element_type=F32)
    carry = carry_ref[...]
    carry_wide = jnp.concatenate([carry] * (tm // LANES), axis=1)
    rank = jnp.sum(jnp.where(onehot, before + carry_wide, 0.0), axis=0, keepdims=True)
    carry_ref[...] = carry + counts
    cnt_ref[...] = carry + counts
    bucket_ref[0] = bucket
    rank_ref[0] = rank.astype(jnp.int32)


def _out_proj(x2, conv_n, attn, beta_attn, w_out_bf, g_ffn, w_router_bf, b_router, tm):
    t_tokens = x2.shape[0]
    row = lambda i: (i, 0)
    fixed = lambda i: (0, 0)
    tri = (jnp.arange(tm)[:, None] < jnp.arange(tm)[None, :]).astype(BF16)
    lane_row = pl.BlockSpec((1, 1, tm), lambda i: (i, 0, 0))
    return pl.pallas_call(
        _out_proj_kernel,
        grid=(t_tokens // tm,),
        in_specs=[pl.BlockSpec((tm, D_MODEL), row), pl.BlockSpec((tm, CONV_CH), row),
                  pl.BlockSpec((tm, ATTN_WIDTH), row), pl.BlockSpec((1, ATTN_WIDTH), fixed),
                  pl.BlockSpec((CONV_CH, D_MODEL), fixed), pl.BlockSpec((ATTN_WIDTH, D_MODEL), fixed),
                  pl.BlockSpec((1, D_MODEL), fixed), pl.BlockSpec((D_MODEL, LANES), fixed),
                  pl.BlockSpec((1, LANES), fixed), pl.BlockSpec((tm, tm), fixed)],
        out_specs=[pl.BlockSpec((tm * ROW_CHUNKS, LANES), row), lane_row, lane_row,
                   pl.BlockSpec((N_BUCKETS, LANES), fixed)],
        out_shape=[jax.ShapeDtypeStruct((t_tokens * ROW_CHUNKS, LANES), F32),
                   jax.ShapeDtypeStruct((t_tokens // tm, 1, tm), jnp.int32),
                   jax.ShapeDtypeStruct((t_tokens // tm, 1, tm), jnp.int32),
                   jax.ShapeDtypeStruct((N_BUCKETS, LANES), F32)],
        scratch_shapes=[pltpu.VMEM((N_BUCKETS, LANES), F32)],
        compiler_params=pltpu.CompilerParams(dimension_semantics=("arbitrary",), vmem_limit_bytes=VMEM_LIMIT),
        name="out_proj_router",
    )(x2, conv_n, attn, beta_attn.reshape(1, -1), w_out_bf[:CONV_CH], w_out_bf[CONV_CH:],
      g_ffn.reshape(1, -1), w_router_bf, b_router, tri)


def _positions_kernel(bucket_ref, rank_ref, starts_ref, pos_ref):
    bucket = bucket_ref[0]
    ids = lax.broadcasted_iota(jnp.int32, (N_BUCKETS, bucket.shape[1]), 0)
    pos_ref[0] = jnp.sum(jnp.where(ids == bucket, starts_ref[...], 0), axis=0, keepdims=True) + rank_ref[0]


def _positions(bucket, rank, starts, tm):
    t_tokens = bucket.size
    lane_row = pl.BlockSpec((1, 1, tm), lambda i: (i, 0, 0))
    shape = (t_tokens // tm, 1, tm)
    pos = pl.pallas_call(
        _positions_kernel,
        grid=(t_tokens // tm,),
        in_specs=[lane_row, lane_row, pl.BlockSpec((N_BUCKETS, 1), lambda i: (0, 0))],
        out_specs=lane_row,
        out_shape=jax.ShapeDtypeStruct(shape, jnp.int32),
        compiler_params=pltpu.CompilerParams(dimension_semantics=("arbitrary",), vmem_limit_bytes=VMEM_LIMIT),
        name="moe_positions",
    )(bucket.reshape(shape), rank.reshape(shape), starts.reshape(N_BUCKETS, 1))
    return pos.reshape(t_tokens)


def _row_tile(ref, r):
    return ref.at[pl.ds(pl.multiple_of(r * ROW_CHUNKS, ROW_CHUNKS), ROW_CHUNKS), :]


def _dispatch_kernel(pos_ref, pad_start_ref, pad_len_ref, total_ref, x_ref, out_hbm, zero_ref, sem, pad_sem):
    tm = x_ref.shape[0] // ROW_CHUNKS
    step, n_steps = pl.program_id(0), pl.num_programs(0)
    base = step * tm
    half_tile = zero_ref.shape[0] // ROW_CHUNKS
    n_tiles = out_hbm.shape[0] // (2 * zero_ref.shape[0])

    def zero_copy(dst_hbm, row, rows, wait):
        cp = pltpu.make_async_copy(
            zero_ref.at[pl.ds(0, rows * ROW_CHUNKS), :],
            dst_hbm.at[pl.ds(pl.multiple_of(row * ROW_CHUNKS, ROW_CHUNKS), rows * ROW_CHUNKS), :], pad_sem)
        cp.wait() if wait else cp.start()

    def fill(wait):
        def pad(k, carry):
            b = k * n_steps + step
            off, n = pad_start_ref[b], pad_len_ref[b]
            for bit in range(PAD_BITS):
                take = ((n >> bit) & 1) == 1
                pl.when(take)(lambda off=off, bit=bit: zero_copy(out_hbm, off, 1 << bit, wait))
                off = off + jnp.where(take, 1 << bit, 0)
            return carry

        def unused_tile(k, carry):
            row = (total_ref[0] + step + k * n_steps) * 2 * half_tile
            zero_copy(out_hbm, row, half_tile, wait)
            zero_copy(out_hbm, row + half_tile, half_tile, wait)
            return carry

        lax.fori_loop(0, N_BUCKETS // n_steps, pad, 0)
        n_unused = n_tiles - total_ref[0] - step
        lax.fori_loop(0, (jnp.maximum(n_unused, 0) + n_steps - 1) // n_steps, unused_tile, 0)

    @pl.when(step == 0)
    def _():
        zero_ref[...] = jnp.zeros_like(zero_ref)

    fill(wait=False)

    def row_copy(r, dst):
        return pltpu.make_async_copy(_row_tile(x_ref, r), _row_tile(out_hbm, dst), sem)

    def issue(g, carry):
        for u in range(DMA_GROUP):
            r = g * DMA_GROUP + u
            row_copy(r, pos_ref[base + r]).start(priority=u % 2)
        return carry

    lax.fori_loop(0, tm // DMA_GROUP, issue, 0)

    def drain(r, carry):
        row_copy(0, 0).wait()
        return carry

    lax.fori_loop(0, tm, drain, 0, unroll=8)
    fill(wait=True)


def _dispatch(pos, pad_start, pad_len, total, x_rows, sorted_rows, tm):
    n_rows = x_rows.shape[0]
    n_steps = n_rows // (tm * ROW_CHUNKS)
    assert N_BUCKETS % n_steps == 0
    return pl.pallas_call(
        _dispatch_kernel,
        grid_spec=pltpu.PrefetchScalarGridSpec(
            num_scalar_prefetch=4,
            grid=(n_steps,),
            in_specs=[pl.BlockSpec((tm * ROW_CHUNKS, LANES), lambda i, *_: (i, 0))],
            out_specs=pl.BlockSpec(memory_space=pl.ANY),
            scratch_shapes=[pltpu.VMEM(((1 << (PAD_BITS - 1)) * ROW_CHUNKS, LANES), F32),
                            pltpu.SemaphoreType.DMA(()), pltpu.SemaphoreType.DMA(())],
        ),
        out_shape=jax.ShapeDtypeStruct((sorted_rows * ROW_CHUNKS, LANES), F32),
        compiler_params=pltpu.CompilerParams(dimension_semantics=("arbitrary",), vmem_limit_bytes=VMEM_LIMIT,
                                             has_side_effects=True),
        name="moe_dispatch",
    )(pos, pad_start, pad_len, total, x_rows)


def _moe_schedule(counts, n_steps, tm):
    tiles_b = (counts + tm - 1) // tm
    step_end = jnp.cumsum(tiles_b)
    step_start = step_end - tiles_b
    starts = step_start * tm
    total = step_end[-1]
    s = jnp.minimum(jnp.arange(n_steps, dtype=jnp.int32), total - 1)
    mine = (s[:, None] >= step_start[None, :]) & (s[:, None] < step_end[None, :])
    pick = lambda v: jnp.sum(jnp.where(mine, v[None, :], 0), axis=1).astype(jnp.int32)
    ids = jnp.arange(N_BUCKETS, dtype=jnp.int32)
    pair = EXPERTS_PER_GROUP * EXPERTS_PER_GROUP
    valid = jnp.clip(pick(counts) - (s - pick(step_start)) * tm, 0, tm).astype(jnp.int32)
    schedule = (s, pick(ids // pair), pick((ids % pair) // EXPERTS_PER_GROUP), pick(ids % EXPERTS_PER_GROUP),
                valid, total.astype(jnp.int32).reshape(1))
    return starts, (starts + counts).astype(jnp.int32), (tiles_b * tm - counts).astype(jnp.int32), schedule


def _moe_kernel(tile_ref, grp_ref, elo_ref, ehi_ref, valid_ref, total_ref,
                x_ref, gf_ref, wr_ref, br_ref, wgl_ref, wul_ref, wdl_ref, wgh_ref, wuh_ref, wdh_ref, y_ref):
    s = pl.program_id(0)
    tm = x_ref.shape[0] // ROW_CHUNKS

    @pl.when(s < total_ref[0])
    def _():
        inside = lax.broadcasted_iota(jnp.int32, (tm, 1), 0) < valid_ref[s]
        x = _load_row_tiles(x_ref, tm)
        hg = (x * gf_ref[...]).astype(BF16)
        inv_rms = lax.rsqrt(jnp.mean(x * x, axis=-1, keepdims=True) + EPS)
        logits = inv_rms * jnp.dot(hg, wr_ref[...], preferred_element_type=F32) + br_ref[...]
        lane = lax.broadcasted_iota(jnp.int32, (tm, LANES), 1)
        col = ROUTER_EXPERT_COL + grp_ref[s] * EXPERTS_PER_GROUP
        lane_val = lambda idx: jnp.sum(jnp.where(lane == idx, logits, 0.0), axis=-1, keepdims=True)
        gl = jnp.where(lane < N_GROUPS, logits, -jnp.inf)
        g_w = 1.0 / jnp.sum(jnp.exp(gl - jnp.max(gl, axis=-1, keepdims=True)), axis=-1, keepdims=True)
        v_lo, v_hi = lane_val(col + elo_ref[s]), lane_val(col + ehi_ref[s])
        v_max = jnp.maximum(v_lo, v_hi)
        p_lo, p_hi = jnp.exp(v_lo - v_max), jnp.exp(v_hi - v_max)
        scale = jnp.where(inside, inv_rms * g_w / (p_lo + p_hi), 0.0)

        def expert(wg_ref, wu_ref, wd_ref, w_row):
            a = inv_rms * jnp.dot(hg, wg_ref[0], preferred_element_type=F32)
            b = jnp.dot(hg, wu_ref[0], preferred_element_type=F32)
            hid = a * jax.nn.sigmoid(a) * b * w_row
            return jnp.dot(hid.astype(BF16), wd_ref[0], preferred_element_type=F32)

        y = expert(wgl_ref, wul_ref, wdl_ref, p_lo * scale) + expert(wgh_ref, wuh_ref, wdh_ref, p_hi * scale)
        _store_row_tiles(y_ref, y)

    @pl.when(s >= total_ref[0])
    def _():
        y_ref[...] = jnp.zeros_like(y_ref)


def _moe(schedule, x_sorted, g_ffn, w_router_bf, b_router, wg_bf, wu_bf, wd_bf, tm):
    n_rows = x_sorted.shape[0]
    n_steps = schedule[0].shape[0]
    row = lambda s, tile, *_: (tile[s], 0)
    fixed = lambda s, *_: (0, 0)
    w_lo = lambda s, tile, grp, elo, ehi, *_: (grp[s] * EXPERTS_PER_GROUP + elo[s], 0, 0)
    w_hi = lambda s, tile, grp, elo, ehi, *_: (grp[s] * EXPERTS_PER_GROUP + ehi[s], 0, 0)
    up_spec = lambda sel: pl.BlockSpec((1, D_MODEL, D_EXPERT), sel)
    down_spec = lambda sel: pl.BlockSpec((1, D_EXPERT, D_MODEL), sel)
    return pl.pallas_call(
        _moe_kernel,
        grid_spec=pltpu.PrefetchScalarGridSpec(
            num_scalar_prefetch=6,
            grid=(n_steps,),
            in_specs=[pl.BlockSpec((tm * ROW_CHUNKS, LANES), row), pl.BlockSpec((1, D_MODEL), fixed),
                      pl.BlockSpec((D_MODEL, LANES), fixed), pl.BlockSpec((1, LANES), fixed),
                      up_spec(w_lo), up_spec(w_lo), down_spec(w_lo), up_spec(w_hi), up_spec(w_hi), down_spec(w_hi)],
            out_specs=pl.BlockSpec((tm * ROW_CHUNKS, LANES), lambda s, *_: (s, 0)),
        ),
        out_shape=jax.ShapeDtypeStruct((n_rows, LANES), F32),
        compiler_params=pltpu.CompilerParams(dimension_semantics=("arbitrary",), vmem_limit_bytes=VMEM_LIMIT),
        name="hier_moe",
    )(*schedule, x_sorted, g_ffn.reshape(1, -1), w_router_bf, b_router, wg_bf, wu_bf, wd_bf, wg_bf, wu_bf, wd_bf)


def _ple_final_kernel(pos_ref, x1_ref, ys_hbm, p_ref, gp_ref, wg_ref, bg_ref, wp_ref, gfin_ref, o_ref,
                      ybuf, sems):
    i = pl.program_id(0)
    n = pl.num_programs(0)
    tm = o_ref.shape[0]

    def row_copy(src, r, slot):
        return pltpu.make_async_copy(_row_tile(ys_hbm, src), _row_tile(ybuf.at[slot], r), sems.at[slot])

    def gather(tile, slot):
        def issue(g, carry):
            for u in range(DMA_GROUP):
                r = g * DMA_GROUP + u
                row_copy(pos_ref[tile * tm + r], r, slot).start(priority=u % 2)
            return carry
        lax.fori_loop(0, tm // DMA_GROUP, issue, 0)

    @pl.when(i == 0)
    def _():
        gather(0, 0)

    slot = i % 2

    def drain(r, carry):
        row_copy(0, r, slot).wait()
        return carry

    def step(prefetch_next):
        lax.fori_loop(0, tm, drain, 0, unroll=8)
        x2 = _load_row_tiles(x1_ref, tm) + _load_row_tiles(ybuf.at[slot], tm)
        if prefetch_next:
            for r in range(tm):
                row_copy(pos_ref[(i + 1) * tm + r], r, 1 - slot).start(priority=r % 2)
        hg = _rms(x2, gp_ref[...]).astype(BF16)
        gate = jax.nn.sigmoid(jnp.dot(hg, wg_ref[...], preferred_element_type=F32) + bg_ref[...])
        emb = jnp.dot(p_ref[...].astype(BF16), wp_ref[...], preferred_element_type=F32)
        x3 = x2 + emb * gate
        o_ref[...] = _rms(x3, gfin_ref[...])

    @pl.when(i + 1 < n)
    def _():
        step(True)

    @pl.when(i + 1 == n)
    def _():
        step(False)


def _ple_final(pos, x1_rows, y_sorted, p2, g_ple, w_gate_bf, b_gate, w_proj_bf, g_final, tm):
    t_tokens = p2.shape[0]
    row = lambda i, pos: (i, 0)
    fixed = lambda i, pos: (0, 0)
    vec = pl.BlockSpec((1, D_MODEL), fixed)
    return pl.pallas_call(
        _ple_final_kernel,
        grid_spec=pltpu.PrefetchScalarGridSpec(
            num_scalar_prefetch=1,
            grid=(t_tokens // tm,),
            in_specs=[pl.BlockSpec((tm * ROW_CHUNKS, LANES), row), pl.BlockSpec(memory_space=pl.ANY),
                      pl.BlockSpec((tm, PLE_DIM), row), vec, pl.BlockSpec((D_MODEL, D_MODEL), fixed), vec,
                      pl.BlockSpec((PLE_DIM, D_MODEL), fixed), vec],
            out_specs=pl.BlockSpec((tm, D_MODEL), row),
            scratch_shapes=[pltpu.VMEM((2, tm * ROW_CHUNKS, LANES), F32), pltpu.SemaphoreType.DMA((2,))],
        ),
        out_shape=jax.ShapeDtypeStruct((t_tokens, D_MODEL), F32),
        compiler_params=pltpu.CompilerParams(dimension_semantics=("arbitrary",), vmem_limit_bytes=VMEM_LIMIT),
        name="ple_final",
    )(pos, x1_rows, y_sorted, p2, g_ple.reshape(1, -1), w_gate_bf, b_gate.reshape(1, -1), w_proj_bf,
      g_final.reshape(1, -1))


def kernel(x, p, g_mix, w_in, w_dw, b_dw, conv_ln_g, conv_ln_b, beta_conv, beta_attn, w_out, g_ffn, w_group, b_group, w_erouter, b_erouter, w_gate, w_up, w_down, g_ple, w_ple_gate, b_ple_gate, w_ple_proj, g_final):
    batch, seq, _ = x.shape
    assert p.shape[0] == 1 and seq % MOBA_BLOCK == 0
    t_tokens = batch * seq
    x2 = x.reshape(t_tokens, D_MODEL)
    i = 0
    u, qp, kp, vp, wg_bf, wu_bf, wd_bf = _in_proj(x2, g_mix[i], w_in[i].astype(BF16), w_gate[i], w_up[i], w_down[i],
                                                  seq, tm=512)
    conv_n = _conv(u, w_dw[i], b_dw[i], conv_ln_g[i], conv_ln_b[i], beta_conv[i], batch, seq, ts=256)
    attn = _moba(qp, kp, vp, batch, seq)
    gap = ROUTER_EXPERT_COL - N_GROUPS
    tail = LANES - ROUTER_EXPERT_COL - N_EXPERTS
    w_router = jnp.concatenate(
        [w_group[i], jnp.zeros((D_MODEL, gap), F32),
         jnp.transpose(w_erouter[i], (1, 0, 2)).reshape(D_MODEL, N_EXPERTS),
         jnp.zeros((D_MODEL, tail), F32)], axis=1).astype(BF16)
    b_router = jnp.concatenate([b_group[i], jnp.zeros((gap,), F32), b_erouter[i].reshape(-1),
                                jnp.zeros((tail,), F32)]).reshape(1, LANES)
    x1_rows, bucket, rank, counts = _out_proj(x2, conv_n, attn, beta_attn[i], w_out[i].astype(BF16), g_ffn[i],
                                              w_router, b_router, tm=512)
    moe_tm = 1 << PAD_BITS
    moe_steps = t_tokens // moe_tm + N_PAIR_BUCKETS
    starts, pad_start, pad_len, schedule = _moe_schedule(counts[:, 0].astype(jnp.int32), moe_steps, moe_tm)
    pos = _positions(bucket, rank, starts, tm=2048)
    x1_sorted = _dispatch(pos, pad_start, pad_len, schedule[-1], x1_rows, moe_steps * moe_tm, tm=1024)
    y_sorted = _moe(schedule, x1_sorted, g_ffn[i], w_router, b_router, wg_bf, wu_bf, wd_bf, moe_tm)
    out = _ple_final(pos, x1_rows, y_sorted, p[i].reshape(t_tokens, PLE_DIM), g_ple[i],
                     w_ple_gate[i].astype(BF16), b_ple_gate[i], w_ple_proj[i].astype(BF16), g_final, tm=512)
    return out.reshape(batch, seq, D_MODEL)
```

```python
import jax
import jax.numpy as jnp
from jax import lax
from jax.experimental import pallas as pl
from jax.experimental.pallas import tpu as pltpu

F32 = jnp.float32
BF16 = jnp.bfloat16

D_MODEL = 1024
CONV_CH = 512
CONV_WIDTH = 31
ATTN_WIDTH = 512
HEAD_DIM = 64
N_HEADS = 8
ROT_DIM = 16
ROPE_THETA = 500000.0
MOBA_BLOCK = 256
MOBA_TOPK = 3
N_GROUPS = 4
EXPERTS_PER_GROUP = 8
N_EXPERTS = 32
D_EXPERT = 256
PLE_DIM = 256
N_BUCKETS = N_GROUPS * EXPERTS_PER_GROUP * EXPERTS_PER_GROUP
N_PAIR_BUCKETS = N_GROUPS * EXPERTS_PER_GROUP * (EXPERTS_PER_GROUP - 1) // 2
IN_COLS = 2 * CONV_CH + 3 * ATTN_WIDTH
EPS = 1e-6

LANES = 128
SUBLANES = 8
HEAD_PAD = LANES
QKV_PAD = N_HEADS * HEAD_PAD
ROW_CHUNKS = D_MODEL // LANES
assert ROW_CHUNKS == SUBLANES
MASK_NEG = -1e30
Q_SCALE = HEAD_DIM ** -0.5 * 1.4426950408889634
CONV_HALO = 32
CONV_ROWS = 32
DMA_GROUP = 8
PAD_BITS = 8
ROUTER_EXPERT_COL = SUBLANES
VMEM_LIMIT = 56 * 1024 * 1024


def _rms(x, g):
    return x * lax.rsqrt(jnp.mean(x * x, axis=-1, keepdims=True) + EPS) * g


def _load_row_tiles(ref, n_rows):
    return jnp.concatenate([ref[pl.ds(c, n_rows, stride=ROW_CHUNKS), :] for c in range(ROW_CHUNKS)], axis=1)


def _store_row_tiles(ref, val, accumulate=False):
    n_rows = val.shape[0]
    for c in range(ROW_CHUNKS):
        idx = (pl.ds(c, n_rows, stride=ROW_CHUNKS), slice(None))
        piece = val[:, c * LANES:(c + 1) * LANES]
        ref[idx] = ref[idx] + piece if accumulate else piece


def _router_logits(h_bf16, wr_ref, br_ref):
    return jnp.dot(h_bf16, wr_ref[...], preferred_element_type=F32) + br_ref[...]


def _in_proj_kernel(x_ref, g_ref, w_ref, rc_ref, ra_ref, rb_ref, wg32_ref, wu32_ref, wd32_ref,
                    u_ref, q_ref, k_ref, v_ref, wg16_ref, wu16_ref, wd16_ref):
    wg16_ref[...] = wg32_ref[...].astype(BF16)
    wu16_ref[...] = wu32_ref[...].astype(BF16)
    wd16_ref[...] = wd32_ref[...].astype(BF16)
    h = _rms(x_ref[...], g_ref[...]).astype(BF16)
    proj = jnp.dot(h, w_ref[...], preferred_element_type=F32)
    u_ref[...] = proj[:, :CONV_CH] * jax.nn.sigmoid(proj[:, CONV_CH:2 * CONV_CH])
    rc, ra, rb = rc_ref[...], ra_ref[...], rb_ref[...]
    low = lax.broadcasted_iota(jnp.int32, (x_ref.shape[0], LANES), 1) < HEAD_DIM

    def put(dst_ref, base, rope, scale, spare):
        for c in range(ATTN_WIDTH // LANES):
            t = proj[:, base + c * LANES: base + (c + 1) * LANES]
            if rope:
                t = t * rc + pltpu.roll(t, LANES - ROT_DIM // 2, 1) * ra + pltpu.roll(t, ROT_DIM // 2, 1) * rb
            if scale != 1.0:
                t = t * scale
            even = jnp.where(low, t, spare)
            odd = jnp.where(low, pltpu.roll(t, HEAD_DIM, 1), spare)
            dst_ref[:, (2 * c) * HEAD_PAD:(2 * c + 1) * HEAD_PAD] = even.astype(dst_ref.dtype)
            dst_ref[:, (2 * c + 1) * HEAD_PAD:(2 * c + 2) * HEAD_PAD] = odd.astype(dst_ref.dtype)

    ones_lane = jnp.where(lax.broadcasted_iota(jnp.int32, (x_ref.shape[0], LANES), 1) == HEAD_DIM, 1.0, 0.0)
    put(q_ref, 2 * CONV_CH, True, Q_SCALE, 0.0)
    put(k_ref, 2 * CONV_CH + ATTN_WIDTH, True, 1.0, 0.0)
    put(v_ref, 2 * CONV_CH + 2 * ATTN_WIDTH, False, 1.0, ones_lane)


def _rope_coeff_tables(seq):
    pos = jnp.arange(seq, dtype=F32)
    inv_freq = jnp.power(jnp.float32(ROPE_THETA), -jnp.arange(0, ROT_DIM, 2, dtype=F32) / ROT_DIM)
    ang = pos[:, None] * inv_freq[None, :]
    cos, sin = jnp.cos(ang), jnp.sin(ang)
    zeros = jnp.zeros((seq, HEAD_DIM - ROT_DIM), F32)
    c_head = jnp.concatenate([cos, cos, jnp.ones_like(zeros)], axis=1)
    a_head = jnp.concatenate([-sin, jnp.zeros_like(sin), zeros], axis=1)
    b_head = jnp.concatenate([jnp.zeros_like(sin), sin, zeros], axis=1)
    rep = LANES // HEAD_DIM
    return jnp.tile(c_head, (1, rep)), jnp.tile(a_head, (1, rep)), jnp.tile(b_head, (1, rep))


def _expert_slice_spec(shape, n_steps):
    n_e, rows, cols = shape
    if n_steps >= n_e:
        per = n_steps // n_e
        assert n_steps == per * n_e and rows % (per * 2 * SUBLANES) == 0
        return pl.BlockSpec((1, rows // per, cols), lambda i: (i // per, i % per, 0))
    assert n_e % n_steps == 0
    return pl.BlockSpec((n_e // n_steps, rows, cols), lambda i: (i, 0, 0))


def _in_proj(x2, g_mix, w_in_bf, w_gate, w_up, w_down, seq, tm):
    t_tokens = x2.shape[0]
    n_steps = t_tokens // tm
    rc, ra, rb = _rope_coeff_tables(seq)
    n_seq_tiles = seq // tm
    row = lambda i: (i, 0)
    fixed = lambda i: (0, 0)
    tab = lambda i: (i % n_seq_tiles, 0)
    w_specs = [_expert_slice_spec(w.shape, n_steps) for w in (w_gate, w_up, w_down)]
    return pl.pallas_call(
        _in_proj_kernel,
        grid=(n_steps,),
        in_specs=[pl.BlockSpec((tm, D_MODEL), row), pl.BlockSpec((1, D_MODEL), fixed),
                  pl.BlockSpec((D_MODEL, IN_COLS), fixed, pipeline_mode=pl.Buffered(1)),
                  pl.BlockSpec((tm, LANES), tab), pl.BlockSpec((tm, LANES), tab), pl.BlockSpec((tm, LANES), tab)]
        + w_specs,
        out_specs=[pl.BlockSpec((tm, CONV_CH), row), pl.BlockSpec((tm, QKV_PAD), row),
                   pl.BlockSpec((tm, QKV_PAD), row), pl.BlockSpec((tm, QKV_PAD), row)] + w_specs,
        out_shape=[jax.ShapeDtypeStruct((t_tokens, CONV_CH), F32),
                   jax.ShapeDtypeStruct((t_tokens, QKV_PAD), BF16),
                   jax.ShapeDtypeStruct((t_tokens, QKV_PAD), BF16),
                   jax.ShapeDtypeStruct((t_tokens, QKV_PAD), BF16)]
        + [jax.ShapeDtypeStruct(w.shape, BF16) for w in (w_gate, w_up, w_down)],
        compiler_params=pltpu.CompilerParams(dimension_semantics=("arbitrary",), vmem_limit_bytes=VMEM_LIMIT),
        name="in_proj",
    )(x2, g_mix.reshape(1, D_MODEL), w_in_bf, rc, ra, rb, w_gate, w_up, w_down)


def _conv_begin(u_ref, pad_ref, shift_ref, first_tile):
    ts = u_ref.shape[0]

    @pl.when(first_tile)
    def _():
        pad_ref[0:CONV_HALO, :] = jnp.zeros((CONV_HALO, CONV_CH), F32)

    pad_ref[CONV_HALO:CONV_HALO + ts, :] = u_ref[...]
    kept = ts + CONV_HALO - SUBLANES
    for sh in range(1, SUBLANES):
        shift_ref[sh - 1, 0:kept, :] = pad_ref[sh:sh + kept, :]


def _conv_rows(r0, w, pad_ref, shift_ref, b_ref, lg_ref, lb_ref, beta_ref, o_ref):
    first = CONV_HALO - (CONV_WIDTH - 1)

    def window(start):
        sh, base = start % SUBLANES, start - start % SUBLANES
        if sh == 0:
            return pad_ref[base:base + CONV_ROWS, :]
        return shift_ref[sh - 1, base:base + CONV_ROWS, :]

    acc = jnp.zeros((CONV_ROWS, CONV_CH), F32)
    for tap in range(CONV_WIDTH):
        acc = acc + window(r0 + first + tap) * w[tap:tap + 1, :]
    y = acc + b_ref[...]
    mu = jnp.mean(y, axis=-1, keepdims=True)
    d = y - mu
    var = jnp.mean(d * d, axis=-1, keepdims=True)
    y = d * lax.rsqrt(var + EPS) * lg_ref[...] + lb_ref[...]
    y = y * jax.nn.sigmoid(y)
    o_ref[r0:r0 + CONV_ROWS, :] = _rms(y, beta_ref[...]).astype(o_ref.dtype)


def _conv_end(pad_ref, ts):
    pad_ref[0:CONV_HALO, :] = pad_ref[ts:ts + CONV_HALO, :]


def _conv_kernel(u_ref, w_ref, b_ref, lg_ref, lb_ref, beta_ref, o_ref, pad_ref, shift_ref):
    ts = u_ref.shape[0]
    _conv_begin(u_ref, pad_ref, shift_ref, pl.program_id(1) == 0)
    w = w_ref[...]
    for r0 in range(0, ts, CONV_ROWS):
        _conv_rows(r0, w, pad_ref, shift_ref, b_ref, lg_ref, lb_ref, beta_ref, o_ref)
    _conv_end(pad_ref, ts)


def _conv(u2, w_dw, b_dw, ln_g, ln_b, beta, batch, seq, ts):
    n_s = seq // ts
    row = lambda b, s: (b * n_s + s, 0)
    fixed = lambda b, s: (0, 0)
    vec = pl.BlockSpec((1, CONV_CH), fixed)
    return pl.pallas_call(
        _conv_kernel,
        grid=(batch, n_s),
        in_specs=[pl.BlockSpec((ts, CONV_CH), row), pl.BlockSpec((CONV_WIDTH, CONV_CH), fixed), vec, vec, vec, vec],
        out_specs=pl.BlockSpec((ts, CONV_CH), row),
        out_shape=jax.ShapeDtypeStruct((batch * seq, CONV_CH), BF16),
        scratch_shapes=[pltpu.VMEM((ts + CONV_HALO, CONV_CH), F32),
                        pltpu.VMEM((SUBLANES - 1, ts + CONV_HALO, CONV_CH), F32)],
        compiler_params=pltpu.CompilerParams(dimension_semantics=("arbitrary", "arbitrary"),
                                             vmem_limit_bytes=VMEM_LIMIT),
        name="conformer_conv",
    )(u2, w_dw, b_dw.reshape(1, -1), ln_g.reshape(1, -1), ln_b.reshape(1, -1), beta.reshape(1, -1))


def _moba_kernel(q_ref, k_ref, v_ref, blk_ref, place_ref, o_ref):
    seq = q_ref.shape[0]
    nb = seq // MOBA_BLOCK
    blk_onehot = blk_ref[...]
    place = place_ref[...]
    sub = lax.broadcasted_iota(jnp.int32, (nb, seq), 0)
    q_blk = lax.broadcasted_iota(jnp.int32, (nb, seq), 1) // MOBA_BLOCK
    past = sub < q_blk
    r_io = lax.broadcasted_iota(jnp.int32, (MOBA_BLOCK, MOBA_BLOCK), 0)
    c_io = lax.broadcasted_iota(jnp.int32, (MOBA_BLOCK, MOBA_BLOCK), 1)
    causal = c_io <= r_io
    low = lax.broadcasted_iota(jnp.int32, (MOBA_BLOCK, LANES), 1) < HEAD_DIM
    contract_lanes = (((1,), (1,)), ((), ()))
    q_aug, k_aug = [], []
    for hh in range(2):
        lo, hi = hh * HEAD_PAD, (hh + 1) * HEAD_PAD
        k = k_ref[:, lo:hi]
        q = q_ref[:, lo:hi]
        kf = k.astype(F32)
        k_mean = jnp.concatenate(
            [jnp.mean(kf[j * MOBA_BLOCK:(j + 1) * MOBA_BLOCK], axis=0, keepdims=True) for j in range(nb)], axis=0)
        km_hi = k_mean.astype(BF16)
        km_lo = (k_mean - km_hi.astype(F32)).astype(BF16)
        gate = (lax.dot_general(km_hi, q, contract_lanes, preferred_element_type=F32)
                + lax.dot_general(km_lo, q, contract_lanes, preferred_element_type=F32))
        bias_t = jnp.where(sub == q_blk, 0.0, MASK_NEG)
        for j in range(nb - 1):
            gj = gate[j:j + 1, :]
            beats = ((gate > gj) | ((gate == gj) & (sub < j))) & past
            cnt = jnp.sum(beats.astype(F32), axis=0, keepdims=True)
            bias_t = jnp.where((sub == j) & past & (cnt < float(MOBA_TOPK)), 0.0, bias_t)
        bias = jnp.dot(bias_t.T.astype(BF16), place, preferred_element_type=F32)
        q_aug.append(q + bias.astype(BF16))
        k_aug.append(k + blk_onehot)

    def scores(hh, i):
        return lax.dot_general(q_aug[hh][i * MOBA_BLOCK:(i + 1) * MOBA_BLOCK], k_aug[hh][:(i + 1) * MOBA_BLOCK],
                               contract_lanes, preferred_element_type=F32)

    def attend(s, hh, i):
        v_lo = hh * HEAD_PAD
        s_own = jnp.where(causal, s[:, i * MOBA_BLOCK:], MASK_NEG)
        m = jnp.max(s_own, axis=-1, keepdims=True)
        if i > 0:
            s_past = s[:, :i * MOBA_BLOCK]
            m = jnp.maximum(m, jnp.max(s_past, axis=-1, keepdims=True))
        acc = jnp.dot(jnp.exp2((s_own - m).astype(BF16)),
                      v_ref[i * MOBA_BLOCK:(i + 1) * MOBA_BLOCK, v_lo:v_lo + HEAD_PAD], preferred_element_type=F32)
        if i > 0:
            acc = acc + jnp.dot(jnp.exp2((s_past - m).astype(BF16)), v_ref[:i * MOBA_BLOCK, v_lo:v_lo + HEAD_PAD],
                                preferred_element_type=F32)
        return jnp.where(low, acc / acc[:, HEAD_DIM:HEAD_DIM + 1], 0.0)

    items = [(hh, i) for i in range(nb) for hh in range(2)]
    s_cur = scores(*items[0])
    even = None
    for n, (hh, i) in enumerate(items):
        s_next = scores(*items[n + 1]) if n + 1 < len(items) else None
        out = attend(s_cur, hh, i)
        if hh == 0:
            even = out
        else:
            o_ref[i * MOBA_BLOCK:(i + 1) * MOBA_BLOCK, :] = even + pltpu.roll(out, HEAD_DIM, 1)
        s_cur = s_next


def _moba(qp, kp, vp, batch, seq):
    nb = seq // MOBA_BLOCK
    lane = jnp.arange(LANES)[None, :]
    blk = (jnp.arange(seq) // MOBA_BLOCK)[:, None]
    blk_onehot = (lane == HEAD_DIM + blk).astype(BF16)
    place = (lane == HEAD_DIM + jnp.arange(nb)[:, None]).astype(BF16)
    pair = lambda b, c: (b, c)
    fixed = lambda b, c: (0, 0)
    spec = pl.BlockSpec((seq, 2 * HEAD_PAD), pair)
    return pl.pallas_call(
        _moba_kernel,
        grid=(batch, N_HEADS // 2),
        in_specs=[spec, spec, spec, pl.BlockSpec((seq, LANES), fixed), pl.BlockSpec((nb, LANES), fixed)],
        out_specs=pl.BlockSpec((seq, LANES), pair),
        out_shape=jax.ShapeDtypeStruct((batch * seq, ATTN_WIDTH), F32),
        compiler_params=pltpu.CompilerParams(dimension_semantics=("arbitrary", "arbitrary"),
                                             vmem_limit_bytes=VMEM_LIMIT),
        name="moba_attention",
    )(qp, kp, vp, blk_onehot, place)


def _out_proj_kernel(x_ref, c_ref, a_ref, beta_ref, wc_ref, wa_ref, gf_ref, wr_ref, br_ref, tri_ref,
                     x1_ref, bucket_ref, rank_ref, cnt_ref, carry_ref):
    @pl.when(pl.program_id(0) == 0)
    def _():
        carry_ref[...] = jnp.zeros_like(carry_ref)

    an = _rms(a_ref[...], beta_ref[...]).astype(BF16)
    x1 = (x_ref[...] + jnp.dot(c_ref[...], wc_ref[...], preferred_element_type=F32)
          + jnp.dot(an, wa_ref[...], preferred_element_type=F32))
    _store_row_tiles(x1_ref, x1)
    logits = _router_logits(_rms(x1, gf_ref[...]).astype(BF16), wr_ref, br_ref)
    tm = logits.shape[0]
    lt = logits.T
    sub = lax.broadcasted_iota(jnp.int32, (SUBLANES, tm), 0)
    gl = jnp.where(sub < N_GROUPS, lt[0:SUBLANES, :], -jnp.inf)
    g_idx = jnp.min(jnp.where(gl == jnp.max(gl, axis=0, keepdims=True), sub, N_GROUPS), axis=0, keepdims=True)
    el = jnp.zeros((EXPERTS_PER_GROUP, tm), F32)
    for g in range(N_GROUPS):
        lo = ROUTER_EXPERT_COL + g * EXPERTS_PER_GROUP
        el = jnp.where(g_idx == g, lt[lo:lo + EXPERTS_PER_GROUP, :], el)
    i1 = jnp.min(jnp.where(el == jnp.max(el, axis=0, keepdims=True), sub, EXPERTS_PER_GROUP), axis=0, keepdims=True)
    el2 = jnp.where(sub == i1, -jnp.inf, el)
    i2 = jnp.min(jnp.where(el2 == jnp.max(el2, axis=0, keepdims=True), sub, EXPERTS_PER_GROUP), axis=0,
                 keepdims=True)
    bucket = (g_idx * (EXPERTS_PER_GROUP * EXPERTS_PER_GROUP) + jnp.minimum(i1, i2) * EXPERTS_PER_GROUP
              + jnp.maximum(i1, i2))
    onehot = (lax.broadcasted_iota(jnp.int32, (N_BUCKETS, tm), 0) == bucket)
    onehot_bf = onehot.astype(BF16)
    before = jnp.dot(onehot_bf, tri_ref[...], preferred_element_type=F32)
    counts = jnp.dot(onehot_bf, jnp.ones((tm, LANES), BF16), preferred_element_type=F32)
    carry = carry_ref[...]
    carry_wide = jnp.concatenate([carry] * (tm // LANES), axis=1)
    rank = jnp.sum(jnp.where(onehot, before + carry_wide, 0.0), axis=0, keepdims=True)
    carry_ref[...] = carry + counts
    cnt_ref[...] = carry + counts
    bucket_ref[0] = bucket
    rank_ref[0] = rank.astype(jnp.int32)


def _out_proj(x2, conv_n, attn, beta_attn, w_out_bf, g_ffn, w_router_bf, b_router, tm):
    t_tokens = x2.shape[0]
    row = lambda i: (i, 0)
    fixed = lambda i: (0, 0)
    tri = (jnp.arange(tm)[:, None] < jnp.arange(tm)[None, :]).astype(BF16)
    lane_row = pl.BlockSpec((1, 1, tm), lambda i: (i, 0, 0))
    return pl.pallas_call(
        _out_proj_kernel,
        grid=(t_tokens // tm,),
        in_specs=[pl.BlockSpec((tm, D_MODEL), row), pl.BlockSpec((tm, CONV_CH), row),
                  pl.BlockSpec((tm, ATTN_WIDTH), row), pl.BlockSpec((1, ATTN_WIDTH), fixed),
                  pl.BlockSpec((CONV_CH, D_MODEL), fixed), pl.BlockSpec((ATTN_WIDTH, D_MODEL), fixed),
                  pl.BlockSpec((1, D_MODEL), fixed), pl.BlockSpec((D_MODEL, LANES), fixed),
                  pl.BlockSpec((1, LANES), fixed), pl.BlockSpec((tm, tm), fixed)],
        out_specs=[pl.BlockSpec((tm * ROW_CHUNKS, LANES), row), lane_row, lane_row,
                   pl.BlockSpec((N_BUCKETS, LANES), fixed)],
        out_shape=[jax.ShapeDtypeStruct((t_tokens * ROW_CHUNKS, LANES), F32),
                   jax.ShapeDtypeStruct((t_tokens // tm, 1, tm), jnp.int32),
                   jax.ShapeDtypeStruct((t_tokens // tm, 1, tm), jnp.int32),
                   jax.ShapeDtypeStruct((N_BUCKETS, LANES), F32)],
        scratch_shapes=[pltpu.VMEM((N_BUCKETS, LANES), F32)],
        compiler_params=pltpu.CompilerParams(dimension_semantics=("arbitrary",), vmem_limit_bytes=VMEM_LIMIT),
        name="out_proj_router",
    )(x2, conv_n, attn, beta_attn.reshape(1, -1), w_out_bf[:CONV_CH], w_out_bf[CONV_CH:],
      g_ffn.reshape(1, -1), w_router_bf, b_router, tri)


def _positions_kernel(bucket_ref, rank_ref, starts_ref, pos_ref):
    bucket = bucket_ref[0]
    ids = lax.broadcasted_iota(jnp.int32, (N_BUCKETS, bucket.shape[1]), 0)
    pos_ref[0] = jnp.sum(jnp.where(ids == bucket, starts_ref[...], 0), axis=0, keepdims=True) + rank_ref[0]


def _positions(bucket, rank, starts, tm):
    t_tokens = bucket.size
    lane_row = pl.BlockSpec((1, 1, tm), lambda i: (i, 0, 0))
    shape = (t_tokens // tm, 1, tm)
    pos = pl.pallas_call(
        _positions_kernel,
        grid=(t_tokens // tm,),
        in_specs=[lane_row, lane_row, pl.BlockSpec((N_BUCKETS, 1), lambda i: (0, 0))],
        out_specs=lane_row,
        out_shape=jax.ShapeDtypeStruct(shape, jnp.int32),
        compiler_params=pltpu.CompilerParams(dimension_semantics=("arbitrary",), vmem_limit_bytes=VMEM_LIMIT),
        name="moe_positions",
    )(bucket.reshape(shape), rank.reshape(shape), starts.reshape(N_BUCKETS, 1))
    return pos.reshape(t_tokens)


def _row_tile(ref, r):
    return ref.at[pl.ds(pl.multiple_of(r * ROW_CHUNKS, ROW_CHUNKS), ROW_CHUNKS), :]


def _dispatch_kernel(pos_ref, pad_start_ref, pad_len_ref, total_ref, x_ref, out_hbm, zero_ref, sem, pad_sem):
    tm = x_ref.shape[0] // ROW_CHUNKS
    step, n_steps = pl.program_id(0), pl.num_programs(0)
    base = step * tm
    half_tile = zero_ref.shape[0] // ROW_CHUNKS
    n_tiles = out_hbm.shape[0] // (2 * zero_ref.shape[0])

    def zero_copy(dst_hbm, row, rows, wait):
        cp = pltpu.make_async_copy(
            zero_ref.at[pl.ds(0, rows * ROW_CHUNKS), :],
            dst_hbm.at[pl.ds(pl.multiple_of(row * ROW_CHUNKS, ROW_CHUNKS), rows * ROW_CHUNKS), :], pad_sem)
        cp.wait() if wait else cp.start()

    def fill(wait):
        def pad(k, carry):
            b = k * n_steps + step
            off, n = pad_start_ref[b], pad_len_ref[b]
            for bit in range(PAD_BITS):
                take = ((n >> bit) & 1) == 1
                pl.when(take)(lambda off=off, bit=bit: zero_copy(out_hbm, off, 1 << bit, wait))
                off = off + jnp.where(take, 1 << bit, 0)
            return carry

        def unused_tile(k, carry):
            row = (total_ref[0] + step + k * n_steps) * 2 * half_tile
            zero_copy(out_hbm, row, half_tile, wait)
            zero_copy(out_hbm, row + half_tile, half_tile, wait)
            return carry

        lax.fori_loop(0, N_BUCKETS // n_steps, pad, 0)
        n_unused = n_tiles - total_ref[0] - step
        lax.fori_loop(0, (jnp.maximum(n_unused, 0) + n_steps - 1) // n_steps, unused_tile, 0)

    @pl.when(step == 0)
    def _():
        zero_ref[...] = jnp.zeros_like(zero_ref)

    fill(wait=False)

    def row_copy(r, dst):
        return pltpu.make_async_copy(_row_tile(x_ref, r), _row_tile(out_hbm, dst), sem)

    def issue(g, carry):
        for u in range(DMA_GROUP):
            r = g * DMA_GROUP + u
            row_copy(r, pos_ref[base + r]).start(priority=u % 2)
        return carry

    lax.fori_loop(0, tm // DMA_GROUP, issue, 0)

    def drain(r, carry):
        row_copy(0, 0).wait()
        return carry

    lax.fori_loop(0, tm, drain, 0, unroll=8)
    fill(wait=True)


def _dispatch(pos, pad_start, pad_len, total, x_rows, sorted_rows, tm):
    n_rows = x_rows.shape[0]
    n_steps = n_rows // (tm * ROW_CHUNKS)
    assert N_BUCKETS % n_steps == 0
    return pl.pallas_call(
        _dispatch_kernel,
        grid_spec=pltpu.PrefetchScalarGridSpec(
            num_scalar_prefetch=4,
            grid=(n_steps,),
            in_specs=[pl.BlockSpec((tm * ROW_CHUNKS, LANES), lambda i, *_: (i, 0))],
            out_specs=pl.BlockSpec(memory_space=pl.ANY),
            scratch_shapes=[pltpu.VMEM(((1 << (PAD_BITS - 1)) * ROW_CHUNKS, LANES), F32),
                            pltpu.SemaphoreType.DMA(()), pltpu.SemaphoreType.DMA(())],
        ),
        out_shape=jax.ShapeDtypeStruct((sorted_rows * ROW_CHUNKS, LANES), F32),
        compiler_params=pltpu.CompilerParams(dimension_semantics=("arbitrary",), vmem_limit_bytes=VMEM_LIMIT,
                                             has_side_effects=True),
        name="moe_dispatch",
    )(pos, pad_start, pad_len, total, x_rows)


def _moe_schedule(counts, n_steps, tm):
    tiles_b = (counts + tm - 1) // tm
    step_end = jnp.cumsum(tiles_b)
    step_start = step_end - tiles_b
    starts = step_start * tm
    total = step_end[-1]
    s = jnp.minimum(jnp.arange(n_steps, dtype=jnp.int32), total - 1)
    mine = (s[:, None] >= step_start[None, :]) & (s[:, None] < step_end[None, :])
    pick = lambda v: jnp.sum(jnp.where(mine, v[None, :], 0), axis=1).astype(jnp.int32)
    ids = jnp.arange(N_BUCKETS, dtype=jnp.int32)
    pair = EXPERTS_PER_GROUP * EXPERTS_PER_GROUP
    valid = jnp.clip(pick(counts) - (s - pick(step_start)) * tm, 0, tm).astype(jnp.int32)
    schedule = (s, pick(ids // pair), pick((ids % pair) // EXPERTS_PER_GROUP), pick(ids % EXPERTS_PER_GROUP),
                valid, total.astype(jnp.int32).reshape(1))
    return starts, (starts + counts).astype(jnp.int32), (tiles_b * tm - counts).astype(jnp.int32), schedule


def _moe_kernel(tile_ref, grp_ref, elo_ref, ehi_ref, valid_ref, total_ref,
                x_ref, gf_ref, wr_ref, br_ref, wgl_ref, wul_ref, wdl_ref, wgh_ref, wuh_ref, wdh_ref, y_ref):
    s = pl.program_id(0)
    tm = x_ref.shape[0] // ROW_CHUNKS

    @pl.when(s < total_ref[0])
    def _():
        inside = lax.broadcasted_iota(jnp.int32, (tm, 1), 0) < valid_ref[s]
        x = _load_row_tiles(x_ref, tm)
        hg = (x * gf_ref[...]).astype(BF16)
        inv_rms = lax.rsqrt(jnp.mean(x * x, axis=-1, keepdims=True) + EPS)
        logits = inv_rms * jnp.dot(hg, wr_ref[...], preferred_element_type=F32) + br_ref[...]
        lane = lax.broadcasted_iota(jnp.int32, (tm, LANES), 1)
        col = ROUTER_EXPERT_COL + grp_ref[s] * EXPERTS_PER_GROUP
        lane_val = lambda idx: jnp.sum(jnp.where(lane == idx, logits, 0.0), axis=-1, keepdims=True)
        gl = jnp.where(lane < N_GROUPS, logits, -jnp.inf)
        g_w = 1.0 / jnp.sum(jnp.exp(gl - jnp.max(gl, axis=-1, keepdims=True)), axis=-1, keepdims=True)
        v_lo, v_hi = lane_val(col + elo_ref[s]), lane_val(col + ehi_ref[s])
        v_max = jnp.maximum(v_lo, v_hi)
        p_lo, p_hi = jnp.exp(v_lo - v_max), jnp.exp(v_hi - v_max)
        scale = jnp.where(inside, inv_rms * g_w / (p_lo + p_hi), 0.0)

        def expert(wg_ref, wu_ref, wd_ref, w_row):
            a = inv_rms * jnp.dot(hg, wg_ref[0], preferred_element_type=F32)
            b = jnp.dot(hg, wu_ref[0], preferred_element_type=F32)
            hid = a * jax.nn.sigmoid(a) * b * w_row
            return jnp.dot(hid.astype(BF16), wd_ref[0], preferred_element_type=F32)

        y = expert(wgl_ref, wul_ref, wdl_ref, p_lo * scale) + expert(wgh_ref, wuh_ref, wdh_ref, p_hi * scale)
        _store_row_tiles(y_ref, y)

    @pl.when(s >= total_ref[0])
    def _():
        y_ref[...] = jnp.zeros_like(y_ref)


def _moe(schedule, x_sorted, g_ffn, w_router_bf, b_router, wg_bf, wu_bf, wd_bf, tm):
    n_rows = x_sorted.shape[0]
    n_steps = schedule[0].shape[0]
    row = lambda s, tile, *_: (tile[s], 0)
    fixed = lambda s, *_: (0, 0)
    w_lo = lambda s, tile, grp, elo, ehi, *_: (grp[s] * EXPERTS_PER_GROUP + elo[s], 0, 0)
    w_hi = lambda s, tile, grp, elo, ehi, *_: (grp[s] * EXPERTS_PER_GROUP + ehi[s], 0, 0)
    up_spec = lambda sel: pl.BlockSpec((1, D_MODEL, D_EXPERT), sel)
    down_spec = lambda sel: pl.BlockSpec((1, D_EXPERT, D_MODEL), sel)
    return pl.pallas_call(
        _moe_kernel,
        grid_spec=pltpu.PrefetchScalarGridSpec(
            num_scalar_prefetch=6,
            grid=(n_steps,),
            in_specs=[pl.BlockSpec((tm * ROW_CHUNKS, LANES), row), pl.BlockSpec((1, D_MODEL), fixed),
                      pl.BlockSpec((D_MODEL, LANES), fixed), pl.BlockSpec((1, LANES), fixed),
                      up_spec(w_lo), up_spec(w_lo), down_spec(w_lo), up_spec(w_hi), up_spec(w_hi), down_spec(w_hi)],
            out_specs=pl.BlockSpec((tm * ROW_CHUNKS, LANES), lambda s, *_: (s, 0)),
        ),
        out_shape=jax.ShapeDtypeStruct((n_rows, LANES), F32),
        compiler_params=pltpu.CompilerParams(dimension_semantics=("arbitrary",), vmem_limit_bytes=VMEM_LIMIT),
        name="hier_moe",
    )(*schedule, x_sorted, g_ffn.reshape(1, -1), w_router_bf, b_router, wg_bf, wu_bf, wd_bf, wg_bf, wu_bf, wd_bf)


def _ple_final_kernel(pos_ref, x1_ref, ys_hbm, p_ref, gp_ref, wg_ref, bg_ref, wp_ref, gfin_ref, o_ref,
                      ybuf, sems):
    i = pl.program_id(0)
    n = pl.num_programs(0)
    tm = o_ref.shape[0]

    def row_copy(src, r, slot):
        return pltpu.make_async_copy(_row_tile(ys_hbm, src), _row_tile(ybuf.at[slot], r), sems.at[slot])

    def gather(tile, slot):
        def issue(g, carry):
            for u in range(DMA_GROUP):
                r = g * DMA_GROUP + u
                row_copy(pos_ref[tile * tm + r], r, slot).start(priority=u % 2)
            return carry
        lax.fori_loop(0, tm // DMA_GROUP, issue, 0)

    @pl.when(i == 0)
    def _():
        gather(0, 0)

    slot = i % 2

    def drain(r, carry):
        row_copy(0, r, slot).wait()
        return carry

    def step(prefetch_next):
        lax.fori_loop(0, tm, drain, 0, unroll=8)
        x2 = _load_row_tiles(x1_ref, tm) + _load_row_tiles(ybuf.at[slot], tm)
        if prefetch_next:
            for r in range(tm):
                row_copy(pos_ref[(i + 1) * tm + r], r, 1 - slot).start(priority=r % 2)
        hg = _rms(x2, gp_ref[...]).astype(BF16)
        gate = jax.nn.sigmoid(jnp.dot(hg, wg_ref[...], preferred_element_type=F32) + bg_ref[...])
        emb = jnp.dot(p_ref[...].astype(BF16), wp_ref[...], preferred_element_type=F32)
        x3 = x2 + emb * gate
        o_ref[...] = _rms(x3, gfin_ref[...])

    @pl.when(i + 1 < n)
    def _():
        step(True)

    @pl.when(i + 1 == n)
    def _():
        step(False)


def _ple_final(pos, x1_rows, y_sorted, p2, g_ple, w_gate_bf, b_gate, w_proj_bf, g_final, tm):
    t_tokens = p2.shape[0]
    row = lambda i, pos: (i, 0)
    fixed = lambda i, pos: (0, 0)
    vec = pl.BlockSpec((1, D_MODEL), fixed)
    return pl.pallas_call(
        _ple_final_kernel,
        grid_spec=pltpu.PrefetchScalarGridSpec(
            num_scalar_prefetch=1,
            grid=(t_tokens // tm,),
            in_specs=[pl.BlockSpec((tm * ROW_CHUNKS, LANES), row), pl.BlockSpec(memory_space=pl.ANY),
                      pl.BlockSpec((tm, PLE_DIM), row), vec, pl.BlockSpec((D_MODEL, D_MODEL), fixed), vec,
                      pl.BlockSpec((PLE_DIM, D_MODEL), fixed), vec],
            out_specs=pl.BlockSpec((tm, D_MODEL), row),
            scratch_shapes=[pltpu.VMEM((2, tm * ROW_CHUNKS, LANES), F32), pltpu.SemaphoreType.DMA((2,))],
        ),
        out_shape=jax.ShapeDtypeStruct((t_tokens, D_MODEL), F32),
        compiler_params=pltpu.CompilerParams(dimension_semantics=("arbitrary",), vmem_limit_bytes=VMEM_LIMIT),
        name="ple_final",
    )(pos, x1_rows, y_sorted, p2, g_ple.reshape(1, -1), w_gate_bf, b_gate.reshape(1, -1), w_proj_bf,
      g_final.reshape(1, -1))


def kernel(x, p, g_mix, w_in, w_dw, b_dw, conv_ln_g, conv_ln_b, beta_conv, beta_attn, w_out, g_ffn, w_group, b_group, w_erouter, b_erouter, w_gate, w_up, w_down, g_ple, w_ple_gate, b_ple_gate, w_ple_proj, g_final):
    batch, seq, _ = x.shape
    assert p.shape[0] == 1 and seq % MOBA_BLOCK == 0
    t_tokens = batch * seq
    x2 = x.reshape(t_tokens, D_MODEL)
    i = 0
    u, qp, kp, vp, wg_bf, wu_bf, wd_bf = _in_proj(x2, g_mix[i], w_in[i].astype(BF16), w_gate[i], w_up[i], w_down[i],
                                                  seq, tm=1024)
    conv_n = _conv(u, w_dw[i], b_dw[i], conv_ln_g[i], conv_ln_b[i], beta_conv[i], batch, seq, ts=256)
    attn = _moba(qp, kp, vp, batch, seq)
    gap = ROUTER_EXPERT_COL - N_GROUPS
    tail = LANES - ROUTER_EXPERT_COL - N_EXPERTS
    w_router = jnp.concatenate(
        [w_group[i], jnp.zeros((D_MODEL, gap), F32),
         jnp.transpose(w_erouter[i], (1, 0, 2)).reshape(D_MODEL, N_EXPERTS),
         jnp.zeros((D_MODEL, tail), F32)], axis=1).astype(BF16)
    b_router = jnp.concatenate([b_group[i], jnp.zeros((gap,), F32), b_erouter[i].reshape(-1),
                                jnp.zeros((tail,), F32)]).reshape(1, LANES)
    x1_rows, bucket, rank, counts = _out_proj(x2, conv_n, attn, beta_attn[i], w_out[i].astype(BF16), g_ffn[i],
                                              w_router, b_router, tm=1024)
    moe_tm = 1 << PAD_BITS
    moe_steps = t_tokens // moe_tm + N_PAIR_BUCKETS
    starts, pad_start, pad_len, schedule = _moe_schedule(counts[:, 0].astype(jnp.int32), moe_steps, moe_tm)
    pos = _positions(bucket, rank, starts, tm=2048)
    x1_sorted = _dispatch(pos, pad_start, pad_len, schedule[-1], x1_rows, moe_steps * moe_tm, tm=1024)
    y_sorted = _moe(schedule, x1_sorted, g_ffn[i], w_router, b_router, wg_bf, wu_bf, wd_bf, moe_tm)
    out = _ple_final(pos, x1_rows, y_sorted, p[i].reshape(t_tokens, PLE_DIM), g_ple[i],
                     w_ple_gate[i].astype(BF16), b_ple_gate[i], w_ple_proj[i].astype(BF16), g_final, tm=1024)
    return out.reshape(batch, seq, D_MODEL)
```

```python
from typing import NamedTuple

import jax
import jax.numpy as jnp
from jax import lax
from jax.experimental import pallas as pl
from jax.experimental.pallas import tpu as pltpu

F32 = jnp.float32
BF16 = jnp.bfloat16

D_MODEL = 1024
CONV_CH = 512
CONV_WIDTH = 31
ATTN_WIDTH = 512
HEAD_DIM = 64
N_HEADS = 8
ROT_DIM = 16
ROPE_THETA = 500000.0
MOBA_BLOCK = 256
MOBA_TOPK = 3
N_GROUPS = 4
EXPERTS_PER_GROUP = 8
N_EXPERTS = 32
D_EXPERT = 256
PLE_DIM = 256
N_BUCKETS = N_GROUPS * EXPERTS_PER_GROUP * EXPERTS_PER_GROUP
N_PAIR_BUCKETS = N_GROUPS * EXPERTS_PER_GROUP * (EXPERTS_PER_GROUP - 1) // 2
IN_COLS = 2 * CONV_CH + 3 * ATTN_WIDTH
EPS = 1e-6

LANES = 128
SUBLANES = 8
HEAD_PAD = LANES
QKV_PAD = N_HEADS * HEAD_PAD
ROW_CHUNKS = D_MODEL // LANES
assert ROW_CHUNKS == SUBLANES
MASK_NEG = -1e30
Q_SCALE = HEAD_DIM ** -0.5 * 1.4426950408889634
CONV_HALO = 32
CONV_ROWS = 32
DMA_GROUP = 8
PAD_BITS = 8
ROUTER_EXPERT_COL = SUBLANES
VMEM_LIMIT = 56 * 1024 * 1024


class _Tiles(NamedTuple):
    seq_rows: int
    token_rows: int
    position_rows: int


def _choose_tiles(t_tokens, seq):
    seq_rows = min(1024, seq)
    token_rows = min(1024, t_tokens)
    position_rows = min(2048, t_tokens)
    assert seq % seq_rows == 0 and t_tokens % token_rows == 0 and t_tokens % position_rows == 0
    return _Tiles(seq_rows, token_rows, position_rows)


def _rms(x, g):
    return x * lax.rsqrt(jnp.mean(x * x, axis=-1, keepdims=True) + EPS) * g


def _load_row_tiles(ref, n_rows):
    return jnp.concatenate([ref[pl.ds(c, n_rows, stride=ROW_CHUNKS), :] for c in range(ROW_CHUNKS)], axis=1)


def _store_row_tiles(ref, val, accumulate=False):
    n_rows = val.shape[0]
    for c in range(ROW_CHUNKS):
        idx = (pl.ds(c, n_rows, stride=ROW_CHUNKS), slice(None))
        piece = val[:, c * LANES:(c + 1) * LANES]
        ref[idx] = ref[idx] + piece if accumulate else piece


def _router_logits(h_bf16, wr_ref, br_ref):
    return jnp.dot(h_bf16, wr_ref[...], preferred_element_type=F32) + br_ref[...]


def _in_proj_kernel(x_ref, g_ref, w_ref, rc_ref, ra_ref, rb_ref, wg32_ref, wu32_ref, wd32_ref,
                    u_ref, q_ref, k_ref, v_ref, wg16_ref, wu16_ref, wd16_ref):
    wg16_ref[...] = wg32_ref[...].astype(BF16)
    wu16_ref[...] = wu32_ref[...].astype(BF16)
    wd16_ref[...] = wd32_ref[...].astype(BF16)
    h = _rms(x_ref[...], g_ref[...]).astype(BF16)
    proj = jnp.dot(h, w_ref[...], preferred_element_type=F32)
    u_ref[...] = proj[:, :CONV_CH] * jax.nn.sigmoid(proj[:, CONV_CH:2 * CONV_CH])
    rc, ra, rb = rc_ref[...], ra_ref[...], rb_ref[...]
    low = lax.broadcasted_iota(jnp.int32, (x_ref.shape[0], LANES), 1) < HEAD_DIM

    def put(dst_ref, base, rope, scale, spare):
        for c in range(ATTN_WIDTH // LANES):
            t = proj[:, base + c * LANES: base + (c + 1) * LANES]
            if rope:
                t = t * rc + pltpu.roll(t, LANES - ROT_DIM // 2, 1) * ra + pltpu.roll(t, ROT_DIM // 2, 1) * rb
            if scale != 1.0:
                t = t * scale
            even = jnp.where(low, t, spare)
            odd = jnp.where(low, pltpu.roll(t, HEAD_DIM, 1), spare)
            dst_ref[:, (2 * c) * HEAD_PAD:(2 * c + 1) * HEAD_PAD] = even.astype(dst_ref.dtype)
            dst_ref[:, (2 * c + 1) * HEAD_PAD:(2 * c + 2) * HEAD_PAD] = odd.astype(dst_ref.dtype)

    ones_lane = jnp.where(lax.broadcasted_iota(jnp.int32, (x_ref.shape[0], LANES), 1) == HEAD_DIM, 1.0, 0.0)
    put(q_ref, 2 * CONV_CH, True, Q_SCALE, 0.0)
    put(k_ref, 2 * CONV_CH + ATTN_WIDTH, True, 1.0, 0.0)
    put(v_ref, 2 * CONV_CH + 2 * ATTN_WIDTH, False, 1.0, ones_lane)


def _rope_coeff_tables(seq):
    pos = jnp.arange(seq, dtype=F32)
    inv_freq = jnp.power(jnp.float32(ROPE_THETA), -jnp.arange(0, ROT_DIM, 2, dtype=F32) / ROT_DIM)
    ang = pos[:, None] * inv_freq[None, :]
    cos, sin = jnp.cos(ang), jnp.sin(ang)
    zeros = jnp.zeros((seq, HEAD_DIM - ROT_DIM), F32)
    c_head = jnp.concatenate([cos, cos, jnp.ones_like(zeros)], axis=1)
    a_head = jnp.concatenate([-sin, jnp.zeros_like(sin), zeros], axis=1)
    b_head = jnp.concatenate([jnp.zeros_like(sin), sin, zeros], axis=1)
    rep = LANES // HEAD_DIM
    return jnp.tile(c_head, (1, rep)), jnp.tile(a_head, (1, rep)), jnp.tile(b_head, (1, rep))


def _expert_slice_spec(shape, n_steps):
    n_e, rows, cols = shape
    if n_steps >= n_e:
        per = n_steps // n_e
        assert n_steps == per * n_e and rows % (per * 2 * SUBLANES) == 0
        return pl.BlockSpec((1, rows // per, cols), lambda i: (i // per, i % per, 0))
    assert n_e % n_steps == 0
    return pl.BlockSpec((n_e // n_steps, rows, cols), lambda i: (i, 0, 0))


def _in_proj(x2, g_mix, w_in_bf, w_gate, w_up, w_down, seq, tm):
    t_tokens = x2.shape[0]
    n_steps = t_tokens // tm
    rc, ra, rb = _rope_coeff_tables(seq)
    n_seq_tiles = seq // tm
    row = lambda i: (i, 0)
    fixed = lambda i: (0, 0)
    tab = lambda i: (i % n_seq_tiles, 0)
    w_specs = [_expert_slice_spec(w.shape, n_steps) for w in (w_gate, w_up, w_down)]
    return pl.pallas_call(
        _in_proj_kernel,
        grid=(n_steps,),
        in_specs=[pl.BlockSpec((tm, D_MODEL), row), pl.BlockSpec((1, D_MODEL), fixed),
                  pl.BlockSpec((D_MODEL, IN_COLS), fixed, pipeline_mode=pl.Buffered(1)),
                  pl.BlockSpec((tm, LANES), tab), pl.BlockSpec((tm, LANES), tab), pl.BlockSpec((tm, LANES), tab)]
        + w_specs,
        out_specs=[pl.BlockSpec((tm, CONV_CH), row), pl.BlockSpec((tm, QKV_PAD), row),
                   pl.BlockSpec((tm, QKV_PAD), row), pl.BlockSpec((tm, QKV_PAD), row)] + w_specs,
        out_shape=[jax.ShapeDtypeStruct((t_tokens, CONV_CH), F32),
                   jax.ShapeDtypeStruct((t_tokens, QKV_PAD), BF16),
                   jax.ShapeDtypeStruct((t_tokens, QKV_PAD), BF16),
                   jax.ShapeDtypeStruct((t_tokens, QKV_PAD), BF16)]
        + [jax.ShapeDtypeStruct(w.shape, BF16) for w in (w_gate, w_up, w_down)],
        compiler_params=pltpu.CompilerParams(dimension_semantics=("arbitrary",), vmem_limit_bytes=VMEM_LIMIT),
        name="in_proj",
    )(x2, g_mix.reshape(1, D_MODEL), w_in_bf, rc, ra, rb, w_gate, w_up, w_down)


def _conv_begin(u_ref, pad_ref, shift_ref, first_tile):
    ts = u_ref.shape[0]

    @pl.when(first_tile)
    def _():
        pad_ref[0:CONV_HALO, :] = jnp.zeros((CONV_HALO, CONV_CH), F32)

    pad_ref[CONV_HALO:CONV_HALO + ts, :] = u_ref[...]
    kept = ts + CONV_HALO - SUBLANES
    for sh in range(1, SUBLANES):
        shift_ref[sh - 1, 0:kept, :] = pad_ref[sh:sh + kept, :]


def _conv_rows(r0, w, pad_ref, shift_ref, b_ref, lg_ref, lb_ref, beta_ref, o_ref):
    first = CONV_HALO - (CONV_WIDTH - 1)

    def window(start):
        sh, base = start % SUBLANES, start - start % SUBLANES
        if sh == 0:
            return pad_ref[base:base + CONV_ROWS, :]
        return shift_ref[sh - 1, base:base + CONV_ROWS, :]

    acc = jnp.zeros((CONV_ROWS, CONV_CH), F32)
    for tap in range(CONV_WIDTH):
        acc = acc + window(r0 + first + tap) * w[tap:tap + 1, :]
    y = acc + b_ref[...]
    mu = jnp.mean(y, axis=-1, keepdims=True)
    d = y - mu
    var = jnp.mean(d * d, axis=-1, keepdims=True)
    y = d * lax.rsqrt(var + EPS) * lg_ref[...] + lb_ref[...]
    y = y * jax.nn.sigmoid(y)
    o_ref[r0:r0 + CONV_ROWS, :] = _rms(y, beta_ref[...]).astype(o_ref.dtype)


def _conv_end(pad_ref, ts):
    pad_ref[0:CONV_HALO, :] = pad_ref[ts:ts + CONV_HALO, :]


def _conv_kernel(u_ref, w_ref, b_ref, lg_ref, lb_ref, beta_ref, o_ref, pad_ref, shift_ref):
    ts = u_ref.shape[0]
    _conv_begin(u_ref, pad_ref, shift_ref, pl.program_id(1) == 0)
    w = w_ref[...]
    for r0 in range(0, ts, CONV_ROWS):
        _conv_rows(r0, w, pad_ref, shift_ref, b_ref, lg_ref, lb_ref, beta_ref, o_ref)
    _conv_end(pad_ref, ts)


def _conv(u2, w_dw, b_dw, ln_g, ln_b, beta, batch, seq, ts):
    n_s = seq // ts
    row = lambda b, s: (b * n_s + s, 0)
    fixed = lambda b, s: (0, 0)
    vec = pl.BlockSpec((1, CONV_CH), fixed)
    return pl.pallas_call(
        _conv_kernel,
        grid=(batch, n_s),
        in_specs=[pl.BlockSpec((ts, CONV_CH), row), pl.BlockSpec((CONV_WIDTH, CONV_CH), fixed), vec, vec, vec, vec],
        out_specs=pl.BlockSpec((ts, CONV_CH), row),
        out_shape=jax.ShapeDtypeStruct((batch * seq, CONV_CH), BF16),
        scratch_shapes=[pltpu.VMEM((ts + CONV_HALO, CONV_CH), F32),
                        pltpu.VMEM((SUBLANES - 1, ts + CONV_HALO, CONV_CH), F32)],
        compiler_params=pltpu.CompilerParams(dimension_semantics=("arbitrary", "arbitrary"),
                                             vmem_limit_bytes=VMEM_LIMIT),
        name="conformer_conv",
    )(u2, w_dw, b_dw.reshape(1, -1), ln_g.reshape(1, -1), ln_b.reshape(1, -1), beta.reshape(1, -1))


def _moba_kernel(q_ref, k_ref, v_ref, blk_ref, place_ref, o_ref):
    seq = q_ref.shape[0]
    nb = seq // MOBA_BLOCK
    blk_onehot = blk_ref[...]
    place = place_ref[...]
    sub = lax.broadcasted_iota(jnp.int32, (nb, seq), 0)
    q_blk = lax.broadcasted_iota(jnp.int32, (nb, seq), 1) // MOBA_BLOCK
    past = sub < q_blk
    r_io = lax.broadcasted_iota(jnp.int32, (MOBA_BLOCK, MOBA_BLOCK), 0)
    c_io = lax.broadcasted_iota(jnp.int32, (MOBA_BLOCK, MOBA_BLOCK), 1)
    causal = c_io <= r_io
    low = lax.broadcasted_iota(jnp.int32, (MOBA_BLOCK, LANES), 1) < HEAD_DIM
    contract_lanes = (((1,), (1,)), ((), ()))
    q_aug, k_aug = [], []
    for hh in range(2):
        lo, hi = hh * HEAD_PAD, (hh + 1) * HEAD_PAD
        k = k_ref[:, lo:hi]
        q = q_ref[:, lo:hi]
        kf = k.astype(F32)
        k_mean = jnp.concatenate(
            [jnp.mean(kf[j * MOBA_BLOCK:(j + 1) * MOBA_BLOCK], axis=0, keepdims=True) for j in range(nb)], axis=0)
        km_hi = k_mean.astype(BF16)
        km_lo = (k_mean - km_hi.astype(F32)).astype(BF16)
        gate = (lax.dot_general(km_hi, q, contract_lanes, preferred_element_type=F32)
                + lax.dot_general(km_lo, q, contract_lanes, preferred_element_type=F32))
        bias_t = jnp.where(sub == q_blk, 0.0, MASK_NEG)
        for j in range(nb - 1):
            gj = gate[j:j + 1, :]
            beats = ((gate > gj) | ((gate == gj) & (sub < j))) & past
            cnt = jnp.sum(beats.astype(F32), axis=0, keepdims=True)
            bias_t = jnp.where((sub == j) & past & (cnt < float(MOBA_TOPK)), 0.0, bias_t)
        bias = jnp.dot(bias_t.T.astype(BF16), place, preferred_element_type=F32)
        q_aug.append(q + bias.astype(BF16))
        k_aug.append(k + blk_onehot)

    def scores(hh, i):
        return lax.dot_general(q_aug[hh][i * MOBA_BLOCK:(i + 1) * MOBA_BLOCK], k_aug[hh][:(i + 1) * MOBA_BLOCK],
                               contract_lanes, preferred_element_type=F32)

    def attend(s, hh, i):
        v_lo = hh * HEAD_PAD
        s_own = jnp.where(causal, s[:, i * MOBA_BLOCK:], MASK_NEG)
        m = jnp.max(s_own, axis=-1, keepdims=True)
        if i > 0:
            s_past = s[:, :i * MOBA_BLOCK]
            m = jnp.maximum(m, jnp.max(s_past, axis=-1, keepdims=True))
        acc = jnp.dot(jnp.exp2((s_own - m).astype(BF16)),
                      v_ref[i * MOBA_BLOCK:(i + 1) * MOBA_BLOCK, v_lo:v_lo + HEAD_PAD], preferred_element_type=F32)
        if i > 0:
            acc = acc + jnp.dot(jnp.exp2((s_past - m).astype(BF16)), v_ref[:i * MOBA_BLOCK, v_lo:v_lo + HEAD_PAD],
                                preferred_element_type=F32)
        return jnp.where(low, acc / acc[:, HEAD_DIM:HEAD_DIM + 1], 0.0)

    items = [(hh, i) for i in range(nb) for hh in range(2)]
    s_cur = scores(*items[0])
    even = None
    for n, (hh, i) in enumerate(items):
        s_next = scores(*items[n + 1]) if n + 1 < len(items) else None
        out = attend(s_cur, hh, i)
        if hh == 0:
            even = out
        else:
            o_ref[i * MOBA_BLOCK:(i + 1) * MOBA_BLOCK, :] = even + pltpu.roll(out, HEAD_DIM, 1)
        s_cur = s_next


def _moba(qp, kp, vp, batch, seq):
    nb = seq // MOBA_BLOCK
    lane = jnp.arange(LANES)[None, :]
    blk = (jnp.arange(seq) // MOBA_BLOCK)[:, None]
    blk_onehot = (lane == HEAD_DIM + blk).astype(BF16)
    place = (lane == HEAD_DIM + jnp.arange(nb)[:, None]).astype(BF16)
    pair = lambda b, c: (b, c)
    fixed = lambda b, c: (0, 0)
    spec = pl.BlockSpec((seq, 2 * HEAD_PAD), pair)
    return pl.pallas_call(
        _moba_kernel,
        grid=(batch, N_HEADS // 2),
        in_specs=[spec, spec, spec, pl.BlockSpec((seq, LANES), fixed), pl.BlockSpec((nb, LANES), fixed)],
        out_specs=pl.BlockSpec((seq, LANES), pair),
        out_shape=jax.ShapeDtypeStruct((batch * seq, ATTN_WIDTH), F32),
        compiler_params=pltpu.CompilerParams(dimension_semantics=("arbitrary", "arbitrary"),
                                             vmem_limit_bytes=VMEM_LIMIT),
        name="moba_attention",
    )(qp, kp, vp, blk_onehot, place)


def _out_proj_kernel(x_ref, c_ref, a_ref, beta_ref, wc_ref, wa_ref, gf_ref, wr_ref, br_ref, tri_ref,
                     x1_ref, bucket_ref, rank_ref, cnt_ref, carry_ref):
    @pl.when(pl.program_id(0) == 0)
    def _():
        carry_ref[...] = jnp.zeros_like(carry_ref)

    an = _rms(a_ref[...], beta_ref[...]).astype(BF16)
    x1 = (x_ref[...] + jnp.dot(c_ref[...], wc_ref[...], preferred_element_type=F32)
          + jnp.dot(an, wa_ref[...], preferred_element_type=F32))
    _store_row_tiles(x1_ref, x1)
    logits = _router_logits(_rms(x1, gf_ref[...]).astype(BF16), wr_ref, br_ref)
    tm = logits.shape[0]
    lt = logits.T
    sub = lax.broadcasted_iota(jnp.int32, (SUBLANES, tm), 0)
    gl = jnp.where(sub < N_GROUPS, lt[0:SUBLANES, :], -jnp.inf)
    g_idx = jnp.min(jnp.where(gl == jnp.max(gl, axis=0, keepdims=True), sub, N_GROUPS), axis=0, keepdims=True)
    el = jnp.zeros((EXPERTS_PER_GROUP, tm), F32)
    for g in range(N_GROUPS):
        lo = ROUTER_EXPERT_COL + g * EXPERTS_PER_GROUP
        el = jnp.where(g_idx == g, lt[lo:lo + EXPERTS_PER_GROUP, :], el)
    i1 = jnp.min(jnp.where(el == jnp.max(el, axis=0, keepdims=True), sub, EXPERTS_PER_GROUP), axis=0, keepdims=True)
    el2 = jnp.where(sub == i1, -jnp.inf, el)
    i2 = jnp.min(jnp.where(el2 == jnp.max(el2, axis=0, keepdims=True), sub, EXPERTS_PER_GROUP), axis=0,
                 keepdims=True)
    bucket = (g_idx * (EXPERTS_PER_GROUP * EXPERTS_PER_GROUP) + jnp.minimum(i1, i2) * EXPERTS_PER_GROUP
              + jnp.maximum(i1, i2))
    onehot = (lax.broadcasted_iota(jnp.int32, (N_BUCKETS, tm), 0) == bucket)
    onehot_bf = onehot.astype(BF16)
    before = jnp.dot(onehot_bf, tri_ref[...], preferred_element_type=F32)
    counts = jnp.dot(onehot_bf, jnp.ones((tm, LANES), BF16), preferred_element_type=F32)
    carry = carry_ref[...]
    carry_wide = jnp.concatenate([carry] * (tm // LANES), axis=1)
    rank = jnp.sum(jnp.where(onehot, before + carry_wide, 0.0), axis=0, keepdims=True)
    carry_ref[...] = carry + counts
    cnt_ref[...] = carry + counts
    bucket_ref[0] = bucket
    rank_ref[0] = rank.astype(jnp.int32)


def _out_proj(x2, conv_n, attn, beta_attn, w_out_bf, g_ffn, w_router_bf, b_router, tm):
    t_tokens = x2.shape[0]
    row = lambda i: (i, 0)
    fixed = lambda i: (0, 0)
    tri = (jnp.arange(tm)[:, None] < jnp.arange(tm)[None, :]).astype(BF16)
    lane_row = pl.BlockSpec((1, 1, tm), lambda i: (i, 0, 0))
    return pl.pallas_call(
        _out_proj_kernel,
        grid=(t_tokens // tm,),
        in_specs=[pl.BlockSpec((tm, D_MODEL), row), pl.BlockSpec((tm, CONV_CH), row),
                  pl.BlockSpec((tm, ATTN_WIDTH), row), pl.BlockSpec((1, ATTN_WIDTH), fixed),
                  pl.BlockSpec((CONV_CH, D_MODEL), fixed), pl.BlockSpec((ATTN_WIDTH, D_MODEL), fixed),
                  pl.BlockSpec((1, D_MODEL), fixed), pl.BlockSpec((D_MODEL, LANES), fixed),
                  pl.BlockSpec((1, LANES), fixed), pl.BlockSpec((tm, tm), fixed)],
        out_specs=[pl.BlockSpec((tm * ROW_CHUNKS, LANES), row), lane_row, lane_row,
                   pl.BlockSpec((N_BUCKETS, LANES), fixed)],
        out_shape=[jax.ShapeDtypeStruct((t_tokens * ROW_CHUNKS, LANES), F32),
                   jax.ShapeDtypeStruct((t_tokens // tm, 1, tm), jnp.int32),
                   jax.ShapeDtypeStruct((t_tokens // tm, 1, tm), jnp.int32),
                   jax.ShapeDtypeStruct((N_BUCKETS, LANES), F32)],
        scratch_shapes=[pltpu.VMEM((N_BUCKETS, LANES), F32)],
        compiler_params=pltpu.CompilerParams(dimension_semantics=("arbitrary",), vmem_limit_bytes=VMEM_LIMIT),
        name="out_proj_router",
    )(x2, conv_n, attn, beta_attn.reshape(1, -1), w_out_bf[:CONV_CH], w_out_bf[CONV_CH:],
      g_ffn.reshape(1, -1), w_router_bf, b_router, tri)


def _positions_kernel(bucket_ref, rank_ref, starts_ref, pos_ref):
    bucket = bucket_ref[0]
    ids = lax.broadcasted_iota(jnp.int32, (N_BUCKETS, bucket.shape[1]), 0)
    pos_ref[0] = jnp.sum(jnp.where(ids == bucket, starts_ref[...], 0), axis=0, keepdims=True) + rank_ref[0]


def _positions(bucket, rank, starts, tm):
    t_tokens = bucket.size
    lane_row = pl.BlockSpec((1, 1, tm), lambda i: (i, 0, 0))
    shape = (t_tokens // tm, 1, tm)
    pos = pl.pallas_call(
        _positions_kernel,
        grid=(t_tokens // tm,),
        in_specs=[lane_row, lane_row, pl.BlockSpec((N_BUCKETS, 1), lambda i: (0, 0))],
        out_specs=lane_row,
        out_shape=jax.ShapeDtypeStruct(shape, jnp.int32),
        compiler_params=pltpu.CompilerParams(dimension_semantics=("arbitrary",), vmem_limit_bytes=VMEM_LIMIT),
        name="moe_positions",
    )(bucket.reshape(shape), rank.reshape(shape), starts.reshape(N_BUCKETS, 1))
    return pos.reshape(t_tokens)


def _row_tile(ref, r):
    return ref.at[pl.ds(pl.multiple_of(r * ROW_CHUNKS, ROW_CHUNKS), ROW_CHUNKS), :]


def _dispatch_kernel(pos_ref, pad_start_ref, pad_len_ref, total_ref, x_ref, out_hbm, zero_ref, sem, pad_sem):
    tm = x_ref.shape[0] // ROW_CHUNKS
    step, n_steps = pl.program_id(0), pl.num_programs(0)
    base = step * tm
    half_tile = zero_ref.shape[0] // ROW_CHUNKS
    n_tiles = out_hbm.shape[0] // (2 * zero_ref.shape[0])

    def zero_copy(dst_hbm, row, rows, wait):
        cp = pltpu.make_async_copy(
            zero_ref.at[pl.ds(0, rows * ROW_CHUNKS), :],
            dst_hbm.at[pl.ds(pl.multiple_of(row * ROW_CHUNKS, ROW_CHUNKS), rows * ROW_CHUNKS), :], pad_sem)
        cp.wait() if wait else cp.start()

    def fill(wait):
        def pad(k, carry):
            b = k * n_steps + step
            off, n = pad_start_ref[b], pad_len_ref[b]
            for bit in range(PAD_BITS):
                take = ((n >> bit) & 1) == 1
                pl.when(take)(lambda off=off, bit=bit: zero_copy(out_hbm, off, 1 << bit, wait))
                off = off + jnp.where(take, 1 << bit, 0)
            return carry

        def unused_tile(k, carry):
            row = (total_ref[0] + step + k * n_steps) * 2 * half_tile
            zero_copy(out_hbm, row, half_tile, wait)
            zero_copy(out_hbm, row + half_tile, half_tile, wait)
            return carry

        lax.fori_loop(0, N_BUCKETS // n_steps, pad, 0)
        n_unused = n_tiles - total_ref[0] - step
        lax.fori_loop(0, (jnp.maximum(n_unused, 0) + n_steps - 1) // n_steps, unused_tile, 0)

    @pl.when(step == 0)
    def _():
        zero_ref[...] = jnp.zeros_like(zero_ref)

    fill(wait=False)

    def row_copy(r, dst):
        return pltpu.make_async_copy(_row_tile(x_ref, r), _row_tile(out_hbm, dst), sem)

    def issue(g, carry):
        for u in range(DMA_GROUP):
            r = g * DMA_GROUP + u
            row_copy(r, pos_ref[base + r]).start(priority=u % 2)
        return carry

    lax.fori_loop(0, tm // DMA_GROUP, issue, 0)

    def drain(r, carry):
        row_copy(0, 0).wait()
        return carry

    lax.fori_loop(0, tm, drain, 0, unroll=8)
    fill(wait=True)


def _dispatch(pos, pad_start, pad_len, total, x_rows, sorted_rows, tm):
    n_rows = x_rows.shape[0]
    n_steps = n_rows // (tm * ROW_CHUNKS)
    assert N_BUCKETS % n_steps == 0
    return pl.pallas_call(
        _dispatch_kernel,
        grid_spec=pltpu.PrefetchScalarGridSpec(
            num_scalar_prefetch=4,
            grid=(n_steps,),
            in_specs=[pl.BlockSpec((tm * ROW_CHUNKS, LANES), lambda i, *_: (i, 0))],
            out_specs=pl.BlockSpec(memory_space=pl.ANY),
            scratch_shapes=[pltpu.VMEM(((1 << (PAD_BITS - 1)) * ROW_CHUNKS, LANES), F32),
                            pltpu.SemaphoreType.DMA(()), pltpu.SemaphoreType.DMA(())],
        ),
        out_shape=jax.ShapeDtypeStruct((sorted_rows * ROW_CHUNKS, LANES), F32),
        compiler_params=pltpu.CompilerParams(dimension_semantics=("arbitrary",), vmem_limit_bytes=VMEM_LIMIT,
                                             has_side_effects=True),
        name="moe_dispatch",
    )(pos, pad_start, pad_len, total, x_rows)


def _moe_schedule(counts, n_steps, tm):
    tiles_b = (counts + tm - 1) // tm
    step_end = jnp.cumsum(tiles_b)
    step_start = step_end - tiles_b
    starts = step_start * tm
    total = step_end[-1]
    s = jnp.minimum(jnp.arange(n_steps, dtype=jnp.int32), total - 1)
    mine = (s[:, None] >= step_start[None, :]) & (s[:, None] < step_end[None, :])
    pick = lambda v: jnp.sum(jnp.where(mine, v[None, :], 0), axis=1).astype(jnp.int32)
    ids = jnp.arange(N_BUCKETS, dtype=jnp.int32)
    pair = EXPERTS_PER_GROUP * EXPERTS_PER_GROUP
    valid = jnp.clip(pick(counts) - (s - pick(step_start)) * tm, 0, tm).astype(jnp.int32)
    schedule = (s, pick(ids // pair), pick((ids % pair) // EXPERTS_PER_GROUP), pick(ids % EXPERTS_PER_GROUP),
                valid, total.astype(jnp.int32).reshape(1))
    return starts, (starts + counts).astype(jnp.int32), (tiles_b * tm - counts).astype(jnp.int32), schedule


def _moe_kernel(tile_ref, grp_ref, elo_ref, ehi_ref, valid_ref, total_ref,
                x_ref, gf_ref, wr_ref, br_ref, wgl_ref, wul_ref, wdl_ref, wgh_ref, wuh_ref, wdh_ref, y_ref):
    s = pl.program_id(0)
    tm = x_ref.shape[0] // ROW_CHUNKS

    @pl.when(s < total_ref[0])
    def _():
        inside = lax.broadcasted_iota(jnp.int32, (tm, 1), 0) < valid_ref[s]
        x = _load_row_tiles(x_ref, tm)
        hg = (x * gf_ref[...]).astype(BF16)
        inv_rms = lax.rsqrt(jnp.mean(x * x, axis=-1, keepdims=True) + EPS)
        logits = inv_rms * jnp.dot(hg, wr_ref[...], preferred_element_type=F32) + br_ref[...]
        lane = lax.broadcasted_iota(jnp.int32, (tm, LANES), 1)
        col = ROUTER_EXPERT_COL + grp_ref[s] * EXPERTS_PER_GROUP
        lane_val = lambda idx: jnp.sum(jnp.where(lane == idx, logits, 0.0), axis=-1, keepdims=True)
        gl = jnp.where(lane < N_GROUPS, logits, -jnp.inf)
        g_w = 1.0 / jnp.sum(jnp.exp(gl - jnp.max(gl, axis=-1, keepdims=True)), axis=-1, keepdims=True)
        v_lo, v_hi = lane_val(col + elo_ref[s]), lane_val(col + ehi_ref[s])
        v_max = jnp.maximum(v_lo, v_hi)
        p_lo, p_hi = jnp.exp(v_lo - v_max), jnp.exp(v_hi - v_max)
        scale = jnp.where(inside, inv_rms * g_w / (p_lo + p_hi), 0.0)

        def expert(wg_ref, wu_ref, wd_ref, w_row):
            a = inv_rms * jnp.dot(hg, wg_ref[0], preferred_element_type=F32)
            b = jnp.dot(hg, wu_ref[0], preferred_element_type=F32)
            hid = a * jax.nn.sigmoid(a) * b * w_row
            return jnp.dot(hid.astype(BF16), wd_ref[0], preferred_element_type=F32)

        y = expert(wgl_ref, wul_ref, wdl_ref, p_lo * scale) + expert(wgh_ref, wuh_ref, wdh_ref, p_hi * scale)
        _store_row_tiles(y_ref, y)

    @pl.when(s >= total_ref[0])
    def _():
        y_ref[...] = jnp.zeros_like(y_ref)


def _moe(schedule, x_sorted, g_ffn, w_router_bf, b_router, wg_bf, wu_bf, wd_bf, tm):
    n_rows = x_sorted.shape[0]
    n_steps = schedule[0].shape[0]
    row = lambda s, tile, *_: (tile[s], 0)
    fixed = lambda s, *_: (0, 0)
    w_lo = lambda s, tile, grp, elo, ehi, *_: (grp[s] * EXPERTS_PER_GROUP + elo[s], 0, 0)
    w_hi = lambda s, tile, grp, elo, ehi, *_: (grp[s] * EXPERTS_PER_GROUP + ehi[s], 0, 0)
    up_spec = lambda sel: pl.BlockSpec((1, D_MODEL, D_EXPERT), sel)
    down_spec = lambda sel: pl.BlockSpec((1, D_EXPERT, D_MODEL), sel)
    return pl.pallas_call(
        _moe_kernel,
        grid_spec=pltpu.PrefetchScalarGridSpec(
            num_scalar_prefetch=6,
            grid=(n_steps,),
            in_specs=[pl.BlockSpec((tm * ROW_CHUNKS, LANES), row), pl.BlockSpec((1, D_MODEL), fixed),
                      pl.BlockSpec((D_MODEL, LANES), fixed), pl.BlockSpec((1, LANES), fixed),
                      up_spec(w_lo), up_spec(w_lo), down_spec(w_lo), up_spec(w_hi), up_spec(w_hi), down_spec(w_hi)],
            out_specs=pl.BlockSpec((tm * ROW_CHUNKS, LANES), lambda s, *_: (s, 0)),
        ),
        out_shape=jax.ShapeDtypeStruct((n_rows, LANES), F32),
        compiler_params=pltpu.CompilerParams(dimension_semantics=("arbitrary",), vmem_limit_bytes=VMEM_LIMIT),
        name="hier_moe",
    )(*schedule, x_sorted, g_ffn.reshape(1, -1), w_router_bf, b_router, wg_bf, wu_bf, wd_bf, wg_bf, wu_bf, wd_bf)


def _ple_final_kernel(pos_ref, x1_ref, ys_hbm, p_ref, gp_ref, wg_ref, bg_ref, wp_ref, gfin_ref, o_ref,
                      ybuf, sems):
    i = pl.program_id(0)
    n = pl.num_programs(0)
    tm = o_ref.shape[0]

    def row_copy(src, r, slot):
        return pltpu.make_async_copy(_row_tile(ys_hbm, src), _row_tile(ybuf.at[slot], r), sems.at[slot])

    def gather(tile, slot):
        def issue(g, carry):
            for u in range(DMA_GROUP):
                r = g * DMA_GROUP + u
                row_copy(pos_ref[tile * tm + r], r, slot).start(priority=u % 2)
            return carry
        lax.fori_loop(0, tm // DMA_GROUP, issue, 0)

    @pl.when(i == 0)
    def _():
        gather(0, 0)

    slot = i % 2

    def drain(r, carry):
        row_copy(0, r, slot).wait()
        return carry

    def step(prefetch_next):
        lax.fori_loop(0, tm, drain, 0, unroll=8)
        x2 = _load_row_tiles(x1_ref, tm) + _load_row_tiles(ybuf.at[slot], tm)
        if prefetch_next:
            for r in range(tm):
                row_copy(pos_ref[(i + 1) * tm + r], r, 1 - slot).start(priority=r % 2)
        hg = _rms(x2, gp_ref[...]).astype(BF16)
        gate = jax.nn.sigmoid(jnp.dot(hg, wg_ref[...], preferred_element_type=F32) + bg_ref[...])
        emb = jnp.dot(p_ref[...].astype(BF16), wp_ref[...], preferred_element_type=F32)
        x3 = x2 + emb * gate
        o_ref[...] = _rms(x3, gfin_ref[...])

    @pl.when(i + 1 < n)
    def _():
        step(True)

    @pl.when(i + 1 == n)
    def _():
        step(False)


def _ple_final(pos, x1_rows, y_sorted, p2, g_ple, w_gate_bf, b_gate, w_proj_bf, g_final, tm):
    t_tokens = p2.shape[0]
    row = lambda i, pos: (i, 0)
    fixed = lambda i, pos: (0, 0)
    vec = pl.BlockSpec((1, D_MODEL), fixed)
    return pl.pallas_call(
        _ple_final_kernel,
        grid_spec=pltpu.PrefetchScalarGridSpec(
            num_scalar_prefetch=1,
            grid=(t_tokens // tm,),
            in_specs=[pl.BlockSpec((tm * ROW_CHUNKS, LANES), row), pl.BlockSpec(memory_space=pl.ANY),
                      pl.BlockSpec((tm, PLE_DIM), row), vec, pl.BlockSpec((D_MODEL, D_MODEL), fixed), vec,
                      pl.BlockSpec((PLE_DIM, D_MODEL), fixed), vec],
            out_specs=pl.BlockSpec((tm, D_MODEL), row),
            scratch_shapes=[pltpu.VMEM((2, tm * ROW_CHUNKS, LANES), F32), pltpu.SemaphoreType.DMA((2,))],
        ),
        out_shape=jax.ShapeDtypeStruct((t_tokens, D_MODEL), F32),
        compiler_params=pltpu.CompilerParams(dimension_semantics=("arbitrary",), vmem_limit_bytes=VMEM_LIMIT),
        name="ple_final",
    )(pos, x1_rows, y_sorted, p2, g_ple.reshape(1, -1), w_gate_bf, b_gate.reshape(1, -1), w_proj_bf,
      g_final.reshape(1, -1))


def kernel(x, p, g_mix, w_in, w_dw, b_dw, conv_ln_g, conv_ln_b, beta_conv, beta_attn, w_out, g_ffn, w_group, b_group, w_erouter, b_erouter, w_gate, w_up, w_down, g_ple, w_ple_gate, b_ple_gate, w_ple_proj, g_final):
    batch, seq, _ = x.shape
    assert p.shape[0] == 1 and seq % MOBA_BLOCK == 0
    t_tokens = batch * seq
    tiles = _choose_tiles(t_tokens, seq)
    x2 = x.reshape(t_tokens, D_MODEL)
    i = 0
    u, qp, kp, vp, wg_bf, wu_bf, wd_bf = _in_proj(x2, g_mix[i], w_in[i].astype(BF16), w_gate[i], w_up[i], w_down[i],
                                                  seq, tm=tiles.seq_rows)
    conv_n = _conv(u, w_dw[i], b_dw[i], conv_ln_g[i], conv_ln_b[i], beta_conv[i], batch, seq, ts=tiles.seq_rows)
    attn = _moba(qp, kp, vp, batch, seq)
    gap = ROUTER_EXPERT_COL - N_GROUPS
    tail = LANES - ROUTER_EXPERT_COL - N_EXPERTS
    w_router = jnp.concatenate(
        [w_group[i], jnp.zeros((D_MODEL, gap), F32),
         jnp.transpose(w_erouter[i], (1, 0, 2)).reshape(D_MODEL, N_EXPERTS),
         jnp.zeros((D_MODEL, tail), F32)], axis=1).astype(BF16)
    b_router = jnp.concatenate([b_group[i], jnp.zeros((gap,), F32), b_erouter[i].reshape(-1),
                                jnp.zeros((tail,), F32)]).reshape(1, LANES)
    x1_rows, bucket, rank, counts = _out_proj(x2, conv_n, attn, beta_attn[i], w_out[i].astype(BF16), g_ffn[i],
                                              w_router, b_router, tm=tiles.token_rows)
    moe_tm = 1 << PAD_BITS
    moe_steps = t_tokens // moe_tm + N_PAIR_BUCKETS
    starts, pad_start, pad_len, schedule = _moe_schedule(counts[:, 0].astype(jnp.int32), moe_steps, moe_tm)
    pos = _positions(bucket, rank, starts, tm=tiles.position_rows)
    x1_sorted = _dispatch(pos, pad_start, pad_len, schedule[-1], x1_rows, moe_steps * moe_tm,
                          tm=tiles.token_rows)
    y_sorted = _moe(schedule, x1_sorted, g_ffn[i], w_router, b_router, wg_bf, wu_bf, wd_bf, moe_tm)
    out = _ple_final(pos, x1_rows, y_sorted, p[i].reshape(t_tokens, PLE_DIM), g_ple[i],
                     w_ple_gate[i].astype(BF16), b_ple_gate[i], w_ple_proj[i].astype(BF16), g_final,
                     tm=tiles.token_rows)
    return out.reshape(batch, seq, D_MODEL)
```

```python
from typing import NamedTuple

import jax
import jax.numpy as jnp
from jax import lax
from jax.experimental import pallas as pl
from jax.experimental.pallas import tpu as pltpu

F32 = jnp.float32
BF16 = jnp.bfloat16

D_MODEL = 1024
CONV_CH = 512
CONV_WIDTH = 31
ATTN_WIDTH = 512
HEAD_DIM = 64
N_HEADS = 8
ROT_DIM = 16
ROPE_THETA = 500000.0
MOBA_BLOCK = 256
MOBA_TOPK = 3
N_GROUPS = 4
EXPERTS_PER_GROUP = 8
N_EXPERTS = 32
D_EXPERT = 256
PLE_DIM = 256
N_BUCKETS = N_GROUPS * EXPERTS_PER_GROUP * EXPERTS_PER_GROUP
N_PAIR_BUCKETS = N_GROUPS * EXPERTS_PER_GROUP * (EXPERTS_PER_GROUP - 1) // 2
IN_COLS = 2 * CONV_CH + 3 * ATTN_WIDTH
EPS = 1e-6

LANES = 128
SUBLANES = 8
HEAD_PAD = LANES
QKV_PAD = N_HEADS * HEAD_PAD
ROW_CHUNKS = D_MODEL // LANES
assert ROW_CHUNKS == SUBLANES
MASK_NEG = -1e30
Q_SCALE = HEAD_DIM ** -0.5 * 1.4426950408889634
CONV_HALO = 32
CONV_ROWS = 64
DMA_GROUP = 8
PAD_BITS = 8
ROUTER_EXPERT_COL = SUBLANES
VMEM_LIMIT = 56 * 1024 * 1024


class _Tiles(NamedTuple):
    seq_rows: int
    token_rows: int
    position_rows: int


def _choose_tiles(t_tokens, seq):
    seq_rows = min(1024, seq)
    token_rows = min(1024, t_tokens)
    position_rows = min(2048, t_tokens)
    assert seq % seq_rows == 0 and t_tokens % token_rows == 0 and t_tokens % position_rows == 0
    return _Tiles(seq_rows, token_rows, position_rows)


def _rms(x, g):
    return x * lax.rsqrt(jnp.mean(x * x, axis=-1, keepdims=True) + EPS) * g


def _load_row_tiles(ref, n_rows):
    return jnp.concatenate([ref[pl.ds(c, n_rows, stride=ROW_CHUNKS), :] for c in range(ROW_CHUNKS)], axis=1)


def _store_row_tiles(ref, val, accumulate=False):
    n_rows = val.shape[0]
    for c in range(ROW_CHUNKS):
        idx = (pl.ds(c, n_rows, stride=ROW_CHUNKS), slice(None))
        piece = val[:, c * LANES:(c + 1) * LANES]
        ref[idx] = ref[idx] + piece if accumulate else piece


def _router_logits(h_bf16, wr_ref, br_ref):
    return jnp.dot(h_bf16, wr_ref[...], preferred_element_type=F32) + br_ref[...]


def _in_proj_kernel(x_ref, g_ref, w_ref, rc_ref, ra_ref, rb_ref, wg32_ref, wu32_ref, wd32_ref,
                    u_ref, q_ref, k_ref, v_ref, wg16_ref, wu16_ref, wd16_ref):
    wg16_ref[...] = wg32_ref[...].astype(BF16)
    wu16_ref[...] = wu32_ref[...].astype(BF16)
    wd16_ref[...] = wd32_ref[...].astype(BF16)
    h = _rms(x_ref[...], g_ref[...]).astype(BF16)
    proj = jnp.dot(h, w_ref[...], preferred_element_type=F32)
    u_ref[...] = proj[:, :CONV_CH] * jax.nn.sigmoid(proj[:, CONV_CH:2 * CONV_CH])
    rc, ra, rb = rc_ref[...], ra_ref[...], rb_ref[...]
    low = lax.broadcasted_iota(jnp.int32, (x_ref.shape[0], LANES), 1) < HEAD_DIM

    def put(dst_ref, base, rope, scale, spare):
        for c in range(ATTN_WIDTH // LANES):
            t = proj[:, base + c * LANES: base + (c + 1) * LANES]
            if rope:
                t = t * rc + pltpu.roll(t, LANES - ROT_DIM // 2, 1) * ra + pltpu.roll(t, ROT_DIM // 2, 1) * rb
            if scale != 1.0:
                t = t * scale
            even = jnp.where(low, t, spare)
            odd = jnp.where(low, pltpu.roll(t, HEAD_DIM, 1), spare)
            dst_ref[:, (2 * c) * HEAD_PAD:(2 * c + 1) * HEAD_PAD] = even.astype(dst_ref.dtype)
            dst_ref[:, (2 * c + 1) * HEAD_PAD:(2 * c + 2) * HEAD_PAD] = odd.astype(dst_ref.dtype)

    ones_lane = jnp.where(lax.broadcasted_iota(jnp.int32, (x_ref.shape[0], LANES), 1) == HEAD_DIM, 1.0, 0.0)
    put(q_ref, 2 * CONV_CH, True, Q_SCALE, 0.0)
    put(k_ref, 2 * CONV_CH + ATTN_WIDTH, True, 1.0, 0.0)
    put(v_ref, 2 * CONV_CH + 2 * ATTN_WIDTH, False, 1.0, ones_lane)


def _rope_coeff_tables(seq):
    pos = jnp.arange(seq, dtype=F32)
    inv_freq = jnp.power(jnp.float32(ROPE_THETA), -jnp.arange(0, ROT_DIM, 2, dtype=F32) / ROT_DIM)
    ang = pos[:, None] * inv_freq[None, :]
    cos, sin = jnp.cos(ang), jnp.sin(ang)
    zeros = jnp.zeros((seq, HEAD_DIM - ROT_DIM), F32)
    c_head = jnp.concatenate([cos, cos, jnp.ones_like(zeros)], axis=1)
    a_head = jnp.concatenate([-sin, jnp.zeros_like(sin), zeros], axis=1)
    b_head = jnp.concatenate([jnp.zeros_like(sin), sin, zeros], axis=1)
    rep = LANES // HEAD_DIM
    return jnp.tile(c_head, (1, rep)), jnp.tile(a_head, (1, rep)), jnp.tile(b_head, (1, rep))


def _expert_slice_spec(shape, n_steps):
    n_e, rows, cols = shape
    if n_steps >= n_e:
        per = n_steps // n_e
        assert n_steps == per * n_e and rows % (per * 2 * SUBLANES) == 0
        return pl.BlockSpec((1, rows // per, cols), lambda i: (i // per, i % per, 0))
    assert n_e % n_steps == 0
    return pl.BlockSpec((n_e // n_steps, rows, cols), lambda i: (i, 0, 0))


def _in_proj(x2, g_mix, w_in_bf, w_gate, w_up, w_down, seq, tm):
    t_tokens = x2.shape[0]
    n_steps = t_tokens // tm
    rc, ra, rb = _rope_coeff_tables(seq)
    n_seq_tiles = seq // tm
    row = lambda i: (i, 0)
    fixed = lambda i: (0, 0)
    tab = lambda i: (i % n_seq_tiles, 0)
    w_specs = [_expert_slice_spec(w.shape, n_steps) for w in (w_gate, w_up, w_down)]
    return pl.pallas_call(
        _in_proj_kernel,
        grid=(n_steps,),
        in_specs=[pl.BlockSpec((tm, D_MODEL), row), pl.BlockSpec((1, D_MODEL), fixed),
                  pl.BlockSpec((D_MODEL, IN_COLS), fixed, pipeline_mode=pl.Buffered(1)),
                  pl.BlockSpec((tm, LANES), tab), pl.BlockSpec((tm, LANES), tab), pl.BlockSpec((tm, LANES), tab)]
        + w_specs,
        out_specs=[pl.BlockSpec((tm, CONV_CH), row), pl.BlockSpec((tm, QKV_PAD), row),
                   pl.BlockSpec((tm, QKV_PAD), row), pl.BlockSpec((tm, QKV_PAD), row)] + w_specs,
        out_shape=[jax.ShapeDtypeStruct((t_tokens, CONV_CH), F32),
                   jax.ShapeDtypeStruct((t_tokens, QKV_PAD), BF16),
                   jax.ShapeDtypeStruct((t_tokens, QKV_PAD), BF16),
                   jax.ShapeDtypeStruct((t_tokens, QKV_PAD), BF16)]
        + [jax.ShapeDtypeStruct(w.shape, BF16) for w in (w_gate, w_up, w_down)],
        compiler_params=pltpu.CompilerParams(dimension_semantics=("arbitrary",), vmem_limit_bytes=VMEM_LIMIT),
        name="in_proj",
    )(x2, g_mix.reshape(1, D_MODEL), w_in_bf, rc, ra, rb, w_gate, w_up, w_down)


def _conv_begin(u_ref, pad_ref, shift_ref, first_tile):
    ts = u_ref.shape[0]

    @pl.when(first_tile)
    def _():
        pad_ref[0:CONV_HALO, :] = jnp.zeros((CONV_HALO, CONV_CH), F32)

    pad_ref[CONV_HALO:CONV_HALO + ts, :] = u_ref[...]
    kept = ts + CONV_HALO - SUBLANES
    for sh in range(1, SUBLANES):
        shift_ref[sh - 1, 0:kept, :] = pad_ref[sh:sh + kept, :]


def _conv_rows(r0, w, pad_ref, shift_ref, b_ref, lg_ref, lb_ref, beta_ref, o_ref):
    first = CONV_HALO - (CONV_WIDTH - 1)

    def window(start):
        sh, base = start % SUBLANES, start - start % SUBLANES
        if sh == 0:
            return pad_ref[base:base + CONV_ROWS, :]
        return shift_ref[sh - 1, base:base + CONV_ROWS, :]

    acc = jnp.zeros((CONV_ROWS, CONV_CH), F32)
    for tap in range(CONV_WIDTH):
        acc = acc + window(r0 + first + tap) * w[tap:tap + 1, :]
    y = acc + b_ref[...]
    mu = jnp.mean(y, axis=-1, keepdims=True)
    d = y - mu
    var = jnp.mean(d * d, axis=-1, keepdims=True)
    y = d * lax.rsqrt(var + EPS) * lg_ref[...] + lb_ref[...]
    y = y * jax.nn.sigmoid(y)
    o_ref[r0:r0 + CONV_ROWS, :] = _rms(y, beta_ref[...]).astype(o_ref.dtype)


def _conv_end(pad_ref, ts):
    pad_ref[0:CONV_HALO, :] = pad_ref[ts:ts + CONV_HALO, :]


def _conv_kernel(u_ref, w_ref, b_ref, lg_ref, lb_ref, beta_ref, o_ref, pad_ref, shift_ref):
    ts = u_ref.shape[0]
    _conv_begin(u_ref, pad_ref, shift_ref, pl.program_id(1) == 0)
    w = w_ref[...]
    for r0 in range(0, ts, CONV_ROWS):
        _conv_rows(r0, w, pad_ref, shift_ref, b_ref, lg_ref, lb_ref, beta_ref, o_ref)
    _conv_end(pad_ref, ts)


def _conv(u2, w_dw, b_dw, ln_g, ln_b, beta, batch, seq, ts):
    n_s = seq // ts
    row = lambda b, s: (b * n_s + s, 0)
    fixed = lambda b, s: (0, 0)
    vec = pl.BlockSpec((1, CONV_CH), fixed)
    return pl.pallas_call(
        _conv_kernel,
        grid=(batch, n_s),
        in_specs=[pl.BlockSpec((ts, CONV_CH), row), pl.BlockSpec((CONV_WIDTH, CONV_CH), fixed), vec, vec, vec, vec],
        out_specs=pl.BlockSpec((ts, CONV_CH), row),
        out_shape=jax.ShapeDtypeStruct((batch * seq, CONV_CH), BF16),
        scratch_shapes=[pltpu.VMEM((ts + CONV_HALO, CONV_CH), F32),
                        pltpu.VMEM((SUBLANES - 1, ts + CONV_HALO, CONV_CH), F32)],
        compiler_params=pltpu.CompilerParams(dimension_semantics=("arbitrary", "arbitrary"),
                                             vmem_limit_bytes=VMEM_LIMIT),
        name="conformer_conv",
    )(u2, w_dw, b_dw.reshape(1, -1), ln_g.reshape(1, -1), ln_b.reshape(1, -1), beta.reshape(1, -1))


def _moba_kernel(q_ref, k_ref, v_ref, blk_ref, place_ref, o_ref):
    seq = q_ref.shape[0]
    nb = seq // MOBA_BLOCK
    blk_onehot = blk_ref[...]
    place = place_ref[...]
    sub = lax.broadcasted_iota(jnp.int32, (nb, seq), 0)
    q_blk = lax.broadcasted_iota(jnp.int32, (nb, seq), 1) // MOBA_BLOCK
    past = sub < q_blk
    r_io = lax.broadcasted_iota(jnp.int32, (MOBA_BLOCK, MOBA_BLOCK), 0)
    c_io = lax.broadcasted_iota(jnp.int32, (MOBA_BLOCK, MOBA_BLOCK), 1)
    causal = c_io <= r_io
    low = lax.broadcasted_iota(jnp.int32, (MOBA_BLOCK, LANES), 1) < HEAD_DIM
    contract_lanes = (((1,), (1,)), ((), ()))
    q_aug, k_aug = [], []
    for hh in range(2):
        lo, hi = hh * HEAD_PAD, (hh + 1) * HEAD_PAD
        k = k_ref[:, lo:hi]
        q = q_ref[:, lo:hi]
        kf = k.astype(F32)
        k_mean = jnp.concatenate(
            [jnp.mean(kf[j * MOBA_BLOCK:(j + 1) * MOBA_BLOCK], axis=0, keepdims=True) for j in range(nb)], axis=0)
        km_hi = k_mean.astype(BF16)
        km_lo = (k_mean - km_hi.astype(F32)).astype(BF16)
        gate = (lax.dot_general(km_hi, q, contract_lanes, preferred_element_type=F32)
                + lax.dot_general(km_lo, q, contract_lanes, preferred_element_type=F32))
        bias_t = jnp.where(sub == q_blk, 0.0, MASK_NEG)
        for j in range(nb - 1):
            gj = gate[j:j + 1, :]
            beats = ((gate > gj) | ((gate == gj) & (sub < j))) & past
            cnt = jnp.sum(beats.astype(F32), axis=0, keepdims=True)
            bias_t = jnp.where((sub == j) & past & (cnt < float(MOBA_TOPK)), 0.0, bias_t)
        bias = jnp.dot(bias_t.T.astype(BF16), place, preferred_element_type=F32)
        q_aug.append(q + bias.astype(BF16))
        k_aug.append(k + blk_onehot)

    def scores(hh, i):
        return lax.dot_general(q_aug[hh][i * MOBA_BLOCK:(i + 1) * MOBA_BLOCK], k_aug[hh][:(i + 1) * MOBA_BLOCK],
                               contract_lanes, preferred_element_type=F32)

    def attend(s, hh, i):
        v_lo = hh * HEAD_PAD
        s_own = jnp.where(causal, s[:, i * MOBA_BLOCK:], MASK_NEG)
        m = jnp.max(s_own, axis=-1, keepdims=True)
        if i > 0:
            s_past = s[:, :i * MOBA_BLOCK]
            m = jnp.maximum(m, jnp.max(s_past, axis=-1, keepdims=True))
        acc = jnp.dot(jnp.exp2((s_own - m).astype(BF16)),
                      v_ref[i * MOBA_BLOCK:(i + 1) * MOBA_BLOCK, v_lo:v_lo + HEAD_PAD], preferred_element_type=F32)
        if i > 0:
            acc = acc + jnp.dot(jnp.exp2((s_past - m).astype(BF16)), v_ref[:i * MOBA_BLOCK, v_lo:v_lo + HEAD_PAD],
                                preferred_element_type=F32)
        return jnp.where(low, acc / acc[:, HEAD_DIM:HEAD_DIM + 1], 0.0)

    items = [(hh, i) for i in range(nb) for hh in range(2)]
    s_cur = scores(*items[0])
    even = None
    for n, (hh, i) in enumerate(items):
        s_next = scores(*items[n + 1]) if n + 1 < len(items) else None
        out = attend(s_cur, hh, i)
        if hh == 0:
            even = out
        else:
            o_ref[i * MOBA_BLOCK:(i + 1) * MOBA_BLOCK, :] = even + pltpu.roll(out, HEAD_DIM, 1)
        s_cur = s_next


def _moba(qp, kp, vp, batch, seq):
    nb = seq // MOBA_BLOCK
    lane = jnp.arange(LANES)[None, :]
    blk = (jnp.arange(seq) // MOBA_BLOCK)[:, None]
    blk_onehot = (lane == HEAD_DIM + blk).astype(BF16)
    place = (lane == HEAD_DIM + jnp.arange(nb)[:, None]).astype(BF16)
    pair = lambda b, c: (b, c)
    fixed = lambda b, c: (0, 0)
    spec = pl.BlockSpec((seq, 2 * HEAD_PAD), pair)
    return pl.pallas_call(
        _moba_kernel,
        grid=(batch, N_HEADS // 2),
        in_specs=[spec, spec, spec, pl.BlockSpec((seq, LANES), fixed), pl.BlockSpec((nb, LANES), fixed)],
        out_specs=pl.BlockSpec((seq, LANES), pair),
        out_shape=jax.ShapeDtypeStruct((batch * seq, ATTN_WIDTH), F32),
        compiler_params=pltpu.CompilerParams(dimension_semantics=("arbitrary", "arbitrary"),
                                             vmem_limit_bytes=VMEM_LIMIT),
        name="moba_attention",
    )(qp, kp, vp, blk_onehot, place)


def _out_proj_kernel(x_ref, c_ref, a_ref, beta_ref, wc_ref, wa_ref, gf_ref, wr_ref, br_ref, tri_ref,
                     x1_ref, bucket_ref, rank_ref, cnt_ref, carry_ref):
    @pl.when(pl.program_id(0) == 0)
    def _():
        carry_ref[...] = jnp.zeros_like(carry_ref)

    an = _rms(a_ref[...], beta_ref[...]).astype(BF16)
    x1 = (x_ref[...] + jnp.dot(c_ref[...], wc_ref[...], preferred_element_type=F32)
          + jnp.dot(an, wa_ref[...], preferred_element_type=F32))
    _store_row_tiles(x1_ref, x1)
    logits = _router_logits(_rms(x1, gf_ref[...]).astype(BF16), wr_ref, br_ref)
    tm = logits.shape[0]
    lt = logits.T
    sub = lax.broadcasted_iota(jnp.int32, (SUBLANES, tm), 0)
    gl = jnp.where(sub < N_GROUPS, lt[0:SUBLANES, :], -jnp.inf)
    g_idx = jnp.min(jnp.where(gl == jnp.max(gl, axis=0, keepdims=True), sub, N_GROUPS), axis=0, keepdims=True)
    el = jnp.zeros((EXPERTS_PER_GROUP, tm), F32)
    for g in range(N_GROUPS):
        lo = ROUTER_EXPERT_COL + g * EXPERTS_PER_GROUP
        el = jnp.where(g_idx == g, lt[lo:lo + EXPERTS_PER_GROUP, :], el)
    i1 = jnp.min(jnp.where(el == jnp.max(el, axis=0, keepdims=True), sub, EXPERTS_PER_GROUP), axis=0, keepdims=True)
    el2 = jnp.where(sub == i1, -jnp.inf, el)
    i2 = jnp.min(jnp.where(el2 == jnp.max(el2, axis=0, keepdims=True), sub, EXPERTS_PER_GROUP), axis=0,
                 keepdims=True)
    bucket = (g_idx * (EXPERTS_PER_GROUP * EXPERTS_PER_GROUP) + jnp.minimum(i1, i2) * EXPERTS_PER_GROUP
              + jnp.maximum(i1, i2))
    onehot = (lax.broadcasted_iota(jnp.int32, (N_BUCKETS, tm), 0) == bucket)
    onehot_bf = onehot.astype(BF16)
    before = jnp.dot(onehot_bf, tri_ref[...], preferred_element_type=F32)
    counts = jnp.dot(onehot_bf, jnp.ones((tm, LANES), BF16), preferred_element_type=F32)
    carry = carry_ref[...]
    carry_wide = jnp.concatenate([carry] * (tm // LANES), axis=1)
    rank = jnp.sum(jnp.where(onehot, before + carry_wide, 0.0), axis=0, keepdims=True)
    carry_ref[...] = carry + counts
    cnt_ref[...] = carry + counts
    bucket_ref[0] = bucket
    rank_ref[0] = rank.astype(jnp.int32)


def _out_proj(x2, conv_n, attn, beta_attn, w_out_bf, g_ffn, w_router_bf, b_router, tm):
    t_tokens = x2.shape[0]
    row = lambda i: (i, 0)
    fixed = lambda i: (0, 0)
    tri = (jnp.arange(tm)[:, None] < jnp.arange(tm)[None, :]).astype(BF16)
    lane_row = pl.BlockSpec((1, 1, tm), lambda i: (i, 0, 0))
    return pl.pallas_call(
        _out_proj_kernel,
        grid=(t_tokens // tm,),
        in_specs=[pl.BlockSpec((tm, D_MODEL), row), pl.BlockSpec((tm, CONV_CH), row),
                  pl.BlockSpec((tm, ATTN_WIDTH), row), pl.BlockSpec((1, ATTN_WIDTH), fixed),
                  pl.BlockSpec((CONV_CH, D_MODEL), fixed), pl.BlockSpec((ATTN_WIDTH, D_MODEL), fixed),
                  pl.BlockSpec((1, D_MODEL), fixed), pl.BlockSpec((D_MODEL, LANES), fixed),
                  pl.BlockSpec((1, LANES), fixed), pl.BlockSpec((tm, tm), fixed)],
        out_specs=[pl.BlockSpec((tm * ROW_CHUNKS, LANES), row), lane_row, lane_row,
                   pl.BlockSpec((N_BUCKETS, LANES), fixed)],
        out_shape=[jax.ShapeDtypeStruct((t_tokens * ROW_CHUNKS, LANES), F32),
                   jax.ShapeDtypeStruct((t_tokens // tm, 1, tm), jnp.int32),
                   jax.ShapeDtypeStruct((t_tokens // tm, 1, tm), jnp.int32),
                   jax.ShapeDtypeStruct((N_BUCKETS, LANES), F32)],
        scratch_shapes=[pltpu.VMEM((N_BUCKETS, LANES), F32)],
        compiler_params=pltpu.CompilerParams(dimension_semantics=("arbitrary",), vmem_limit_bytes=VMEM_LIMIT),
        name="out_proj_router",
    )(x2, conv_n, attn, beta_attn.reshape(1, -1), w_out_bf[:CONV_CH], w_out_bf[CONV_CH:],
      g_ffn.reshape(1, -1), w_router_bf, b_router, tri)


def _positions_kernel(bucket_ref, rank_ref, starts_ref, pos_ref):
    bucket = bucket_ref[0]
    ids = lax.broadcasted_iota(jnp.int32, (N_BUCKETS, bucket.shape[1]), 0)
    pos_ref[0] = jnp.sum(jnp.where(ids == bucket, starts_ref[...], 0), axis=0, keepdims=True) + rank_ref[0]


def _positions(bucket, rank, starts, tm):
    t_tokens = bucket.size
    lane_row = pl.BlockSpec((1, 1, tm), lambda i: (i, 0, 0))
    shape = (t_tokens // tm, 1, tm)
    pos = pl.pallas_call(
        _positions_kernel,
        grid=(t_tokens // tm,),
        in_specs=[lane_row, lane_row, pl.BlockSpec((N_BUCKETS, 1), lambda i: (0, 0))],
        out_specs=lane_row,
        out_shape=jax.ShapeDtypeStruct(shape, jnp.int32),
        compiler_params=pltpu.CompilerParams(dimension_semantics=("arbitrary",), vmem_limit_bytes=VMEM_LIMIT),
        name="moe_positions",
    )(bucket.reshape(shape), rank.reshape(shape), starts.reshape(N_BUCKETS, 1))
    return pos.reshape(t_tokens)


def _row_tile(ref, r):
    return ref.at[pl.ds(pl.multiple_of(r * ROW_CHUNKS, ROW_CHUNKS), ROW_CHUNKS), :]


def _dispatch_kernel(pos_ref, pad_start_ref, pad_len_ref, total_ref, x_ref, out_hbm, zero_ref, sem, pad_sem):
    tm = x_ref.shape[0] // ROW_CHUNKS
    step, n_steps = pl.program_id(0), pl.num_programs(0)
    base = step * tm
    half_tile = zero_ref.shape[0] // ROW_CHUNKS
    n_tiles = out_hbm.shape[0] // (2 * zero_ref.shape[0])

    def zero_copy(dst_hbm, row, rows, wait):
        cp = pltpu.make_async_copy(
            zero_ref.at[pl.ds(0, rows * ROW_CHUNKS), :],
            dst_hbm.at[pl.ds(pl.multiple_of(row * ROW_CHUNKS, ROW_CHUNKS), rows * ROW_CHUNKS), :], pad_sem)
        cp.wait() if wait else cp.start()

    def fill(wait):
        def pad(k, carry):
            b = k * n_steps + step
            off, n = pad_start_ref[b], pad_len_ref[b]
            for bit in range(PAD_BITS):
                take = ((n >> bit) & 1) == 1
                pl.when(take)(lambda off=off, bit=bit: zero_copy(out_hbm, off, 1 << bit, wait))
                off = off + jnp.where(take, 1 << bit, 0)
            return carry

        def unused_tile(k, carry):
            row = (total_ref[0] + step + k * n_steps) * 2 * half_tile
            zero_copy(out_hbm, row, half_tile, wait)
            zero_copy(out_hbm, row + half_tile, half_tile, wait)
            return carry

        lax.fori_loop(0, N_BUCKETS // n_steps, pad, 0)
        n_unused = n_tiles - total_ref[0] - step
        lax.fori_loop(0, (jnp.maximum(n_unused, 0) + n_steps - 1) // n_steps, unused_tile, 0)

    @pl.when(step == 0)
    def _():
        zero_ref[...] = jnp.zeros_like(zero_ref)

    fill(wait=False)

    def row_copy(r, dst):
        return pltpu.make_async_copy(_row_tile(x_ref, r), _row_tile(out_hbm, dst), sem)

    def issue(g, carry):
        for u in range(DMA_GROUP):
            r = g * DMA_GROUP + u
            row_copy(r, pos_ref[base + r]).start(priority=u % 2)
        return carry

    lax.fori_loop(0, tm // DMA_GROUP, issue, 0)

    def drain(r, carry):
        row_copy(0, 0).wait()
        return carry

    lax.fori_loop(0, tm, drain, 0, unroll=8)
    fill(wait=True)


def _dispatch(pos, pad_start, pad_len, total, x_rows, sorted_rows, tm):
    n_rows = x_rows.shape[0]
    n_steps = n_rows // (tm * ROW_CHUNKS)
    assert N_BUCKETS % n_steps == 0
    return pl.pallas_call(
        _dispatch_kernel,
        grid_spec=pltpu.PrefetchScalarGridSpec(
            num_scalar_prefetch=4,
            grid=(n_steps,),
            in_specs=[pl.BlockSpec((tm * ROW_CHUNKS, LANES), lambda i, *_: (i, 0))],
            out_specs=pl.BlockSpec(memory_space=pl.ANY),
            scratch_shapes=[pltpu.VMEM(((1 << (PAD_BITS - 1)) * ROW_CHUNKS, LANES), F32),
                            pltpu.SemaphoreType.DMA(()), pltpu.SemaphoreType.DMA(())],
        ),
        out_shape=jax.ShapeDtypeStruct((sorted_rows * ROW_CHUNKS, LANES), F32),
        compiler_params=pltpu.CompilerParams(dimension_semantics=("arbitrary",), vmem_limit_bytes=VMEM_LIMIT,
                                             has_side_effects=True),
        name="moe_dispatch",
    )(pos, pad_start, pad_len, total, x_rows)


def _moe_schedule(counts, n_steps, tm):
    tiles_b = (counts + tm - 1) // tm
    step_end = jnp.cumsum(tiles_b)
    step_start = step_end - tiles_b
    starts = step_start * tm
    total = step_end[-1]
    s = jnp.minimum(jnp.arange(n_steps, dtype=jnp.int32), total - 1)
    mine = (s[:, None] >= step_start[None, :]) & (s[:, None] < step_end[None, :])
    pick = lambda v: jnp.sum(jnp.where(mine, v[None, :], 0), axis=1).astype(jnp.int32)
    ids = jnp.arange(N_BUCKETS, dtype=jnp.int32)
    pair = EXPERTS_PER_GROUP * EXPERTS_PER_GROUP
    valid = jnp.clip(pick(counts) - (s - pick(step_start)) * tm, 0, tm).astype(jnp.int32)
    schedule = (s, pick(ids // pair), pick((ids % pair) // EXPERTS_PER_GROUP), pick(ids % EXPERTS_PER_GROUP),
                valid, total.astype(jnp.int32).reshape(1))
    return starts, (starts + counts).astype(jnp.int32), (tiles_b * tm - counts).astype(jnp.int32), schedule


def _moe_kernel(tile_ref, grp_ref, elo_ref, ehi_ref, valid_ref, total_ref,
                x_ref, gf_ref, wr_ref, br_ref, wgl_ref, wul_ref, wdl_ref, wgh_ref, wuh_ref, wdh_ref, y_ref):
    s = pl.program_id(0)
    tm = x_ref.shape[0] // ROW_CHUNKS

    @pl.when(s < total_ref[0])
    def _():
        inside = lax.broadcasted_iota(jnp.int32, (tm, 1), 0) < valid_ref[s]
        x = _load_row_tiles(x_ref, tm)
        hg = (x * gf_ref[...]).astype(BF16)
        inv_rms = lax.rsqrt(jnp.mean(x * x, axis=-1, keepdims=True) + EPS)
        logits = inv_rms * jnp.dot(hg, wr_ref[...], preferred_element_type=F32) + br_ref[...]
        lane = lax.broadcasted_iota(jnp.int32, (tm, LANES), 1)
        col = ROUTER_EXPERT_COL + grp_ref[s] * EXPERTS_PER_GROUP
        lane_val = lambda idx: jnp.sum(jnp.where(lane == idx, logits, 0.0), axis=-1, keepdims=True)
        gl = jnp.where(lane < N_GROUPS, logits, -jnp.inf)
        g_w = 1.0 / jnp.sum(jnp.exp(gl - jnp.max(gl, axis=-1, keepdims=True)), axis=-1, keepdims=True)
        v_lo, v_hi = lane_val(col + elo_ref[s]), lane_val(col + ehi_ref[s])
        v_max = jnp.maximum(v_lo, v_hi)
        p_lo, p_hi = jnp.exp(v_lo - v_max), jnp.exp(v_hi - v_max)
        scale = jnp.where(inside, inv_rms * g_w / (p_lo + p_hi), 0.0)

        def expert(wg_ref, wu_ref, wd_ref, w_row):
            a = inv_rms * jnp.dot(hg, wg_ref[0], preferred_element_type=F32)
            b = jnp.dot(hg, wu_ref[0], preferred_element_type=F32)
            hid = a * jax.nn.sigmoid(a) * b * w_row
            return jnp.dot(hid.astype(BF16), wd_ref[0], preferred_element_type=F32)

        y = expert(wgl_ref, wul_ref, wdl_ref, p_lo * scale) + expert(wgh_ref, wuh_ref, wdh_ref, p_hi * scale)
        _store_row_tiles(y_ref, y)

    @pl.when(s >= total_ref[0])
    def _():
        y_ref[...] = jnp.zeros_like(y_ref)


def _moe(schedule, x_sorted, g_ffn, w_router_bf, b_router, wg_bf, wu_bf, wd_bf, tm):
    n_rows = x_sorted.shape[0]
    n_steps = schedule[0].shape[0]
    row = lambda s, tile, *_: (tile[s], 0)
    fixed = lambda s, *_: (0, 0)
    w_lo = lambda s, tile, grp, elo, ehi, *_: (grp[s] * EXPERTS_PER_GROUP + elo[s], 0, 0)
    w_hi = lambda s, tile, grp, elo, ehi, *_: (grp[s] * EXPERTS_PER_GROUP + ehi[s], 0, 0)
    up_spec = lambda sel: pl.BlockSpec((1, D_MODEL, D_EXPERT), sel)
    down_spec = lambda sel: pl.BlockSpec((1, D_EXPERT, D_MODEL), sel)
    return pl.pallas_call(
        _moe_kernel,
        grid_spec=pltpu.PrefetchScalarGridSpec(
            num_scalar_prefetch=6,
            grid=(n_steps,),
            in_specs=[pl.BlockSpec((tm * ROW_CHUNKS, LANES), row), pl.BlockSpec((1, D_MODEL), fixed),
                      pl.BlockSpec((D_MODEL, LANES), fixed), pl.BlockSpec((1, LANES), fixed),
                      up_spec(w_lo), up_spec(w_lo), down_spec(w_lo), up_spec(w_hi), up_spec(w_hi), down_spec(w_hi)],
            out_specs=pl.BlockSpec((tm * ROW_CHUNKS, LANES), lambda s, *_: (s, 0)),
        ),
        out_shape=jax.ShapeDtypeStruct((n_rows, LANES), F32),
        compiler_params=pltpu.CompilerParams(dimension_semantics=("arbitrary",), vmem_limit_bytes=VMEM_LIMIT),
        name="hier_moe",
    )(*schedule, x_sorted, g_ffn.reshape(1, -1), w_router_bf, b_router, wg_bf, wu_bf, wd_bf, wg_bf, wu_bf, wd_bf)


def _ple_final_kernel(pos_ref, x1_ref, ys_hbm, p_ref, gp_ref, wg_ref, bg_ref, wp_ref, gfin_ref, o_ref,
                      ybuf, sems):
    i = pl.program_id(0)
    n = pl.num_programs(0)
    tm = o_ref.shape[0]

    def row_copy(src, r, slot):
        return pltpu.make_async_copy(_row_tile(ys_hbm, src), _row_tile(ybuf.at[slot], r), sems.at[slot])

    def gather(tile, slot):
        def issue(g, carry):
            for u in range(DMA_GROUP):
                r = g * DMA_GROUP + u
                row_copy(pos_ref[tile * tm + r], r, slot).start(priority=u % 2)
            return carry
        lax.fori_loop(0, tm // DMA_GROUP, issue, 0)

    @pl.when(i == 0)
    def _():
        gather(0, 0)

    slot = i % 2

    def drain(r, carry):
        row_copy(0, r, slot).wait()
        return carry

    def step(prefetch_next):
        lax.fori_loop(0, tm, drain, 0, unroll=8)
        x2 = _load_row_tiles(x1_ref, tm) + _load_row_tiles(ybuf.at[slot], tm)
        if prefetch_next:
            for r in range(tm):
                row_copy(pos_ref[(i + 1) * tm + r], r, 1 - slot).start(priority=r % 2)
        hg = _rms(x2, gp_ref[...]).astype(BF16)
        gate = jax.nn.sigmoid(jnp.dot(hg, wg_ref[...], preferred_element_type=F32) + bg_ref[...])
        emb = jnp.dot(p_ref[...].astype(BF16), wp_ref[...], preferred_element_type=F32)
        x3 = x2 + emb * gate
        o_ref[...] = _rms(x3, gfin_ref[...])

    @pl.when(i + 1 < n)
    def _():
        step(True)

    @pl.when(i + 1 == n)
    def _():
        step(False)


def _ple_final(pos, x1_rows, y_sorted, p2, g_ple, w_gate_bf, b_gate, w_proj_bf, g_final, tm):
    t_tokens = p2.shape[0]
    row = lambda i, pos: (i, 0)
    fixed = lambda i, pos: (0, 0)
    vec = pl.BlockSpec((1, D_MODEL), fixed)
    return pl.pallas_call(
        _ple_final_kernel,
        grid_spec=pltpu.PrefetchScalarGridSpec(
            num_scalar_prefetch=1,
            grid=(t_tokens // tm,),
            in_specs=[pl.BlockSpec((tm * ROW_CHUNKS, LANES), row), pl.BlockSpec(memory_space=pl.ANY),
                      pl.BlockSpec((tm, PLE_DIM), row), vec, pl.BlockSpec((D_MODEL, D_MODEL), fixed), vec,
                      pl.BlockSpec((PLE_DIM, D_MODEL), fixed), vec],
            out_specs=pl.BlockSpec((tm, D_MODEL), row),
            scratch_shapes=[pltpu.VMEM((2, tm * ROW_CHUNKS, LANES), F32), pltpu.SemaphoreType.DMA((2,))],
        ),
        out_shape=jax.ShapeDtypeStruct((t_tokens, D_MODEL), F32),
        compiler_params=pltpu.CompilerParams(dimension_semantics=("arbitrary",), vmem_limit_bytes=VMEM_LIMIT),
        name="ple_final",
    )(pos, x1_rows, y_sorted, p2, g_ple.reshape(1, -1), w_gate_bf, b_gate.reshape(1, -1), w_proj_bf,
      g_final.reshape(1, -1))


def kernel(x, p, g_mix, w_in, w_dw, b_dw, conv_ln_g, conv_ln_b, beta_conv, beta_attn, w_out, g_ffn, w_group, b_group, w_erouter, b_erouter, w_gate, w_up, w_down, g_ple, w_ple_gate, b_ple_gate, w_ple_proj, g_final):
    batch, seq, _ = x.shape
    assert p.shape[0] == 1 and seq % MOBA_BLOCK == 0
    t_tokens = batch * seq
    tiles = _choose_tiles(t_tokens, seq)
    x2 = x.reshape(t_tokens, D_MODEL)
    i = 0
    u, qp, kp, vp, wg_bf, wu_bf, wd_bf = _in_proj(x2, g_mix[i], w_in[i].astype(BF16), w_gate[i], w_up[i], w_down[i],
                                                  seq, tm=tiles.seq_rows)
    conv_n = _conv(u, w_dw[i], b_dw[i], conv_ln_g[i], conv_ln_b[i], beta_conv[i], batch, seq, ts=tiles.seq_rows)
    attn = _moba(qp, kp, vp, batch, seq)
    gap = ROUTER_EXPERT_COL - N_GROUPS
    tail = LANES - ROUTER_EXPERT_COL - N_EXPERTS
    w_router = jnp.concatenate(
        [w_group[i], jnp.zeros((D_MODEL, gap), F32),
         jnp.transpose(w_erouter[i], (1, 0, 2)).reshape(D_MODEL, N_EXPERTS),
         jnp.zeros((D_MODEL, tail), F32)], axis=1).astype(BF16)
    b_router = jnp.concatenate([b_group[i], jnp.zeros((gap,), F32), b_erouter[i].reshape(-1),
                                jnp.zeros((tail,), F32)]).reshape(1, LANES)
    x1_rows, bucket, rank, counts = _out_proj(x2, conv_n, attn, beta_attn[i], w_out[i].astype(BF16), g_ffn[i],
                                              w_router, b_router, tm=tiles.token_rows)
    moe_tm = 1 << PAD_BITS
    moe_steps = t_tokens // moe_tm + N_PAIR_BUCKETS
    starts, pad_start, pad_len, schedule = _moe_schedule(counts[:, 0].astype(jnp.int32), moe_steps, moe_tm)
    pos = _positions(bucket, rank, starts, tm=tiles.position_rows)
    x1_sorted = _dispatch(pos, pad_start, pad_len, schedule[-1], x1_rows, moe_steps * moe_tm,
                          tm=tiles.position_rows)
    y_sorted = _moe(schedule, x1_sorted, g_ffn[i], w_router, b_router, wg_bf, wu_bf, wd_bf, moe_tm)
    out = _ple_final(pos, x1_rows, y_sorted, p[i].reshape(t_tokens, PLE_DIM), g_ple[i],
                     w_ple_gate[i].astype(BF16), b_ple_gate[i], w_ple_proj[i].astype(BF16), g_final,
                     tm=tiles.token_rows)
    return out.reshape(batch, seq, D_MODEL)
```
